```python
import jax, jax.numpy as jnp
from jax import lax
import numpy as np

D_MODEL = 2048
BATCH = 4
SEQ = 2048
DEPTH = 1
DEC_BATCH = 32
DEC_SEQ = 1
PAST_LEN = 16384
PAGE_SIZE = 128

N_META = 16
GLA_HEADS = 4
GLA_DK = 128
GLA_DV = 256
GLA_RANK = 16
GLA_TAU = 16.0
GLA_CHUNK = 64
SWA_HEADS = 16
SWA_KV_HEADS = 4
SWA_GROUP = SWA_HEADS // SWA_KV_HEADS
SWA_HD = 64
WINDOW = 128
D_FF = 5632
EPS = 1e-6

GLA_QK = GLA_HEADS * GLA_DK
GLA_V = GLA_HEADS * GLA_DV
SWA_Q = SWA_HEADS * SWA_HD
SWA_KV = SWA_KV_HEADS * SWA_HD
SPLITS = (GLA_QK, GLA_QK, GLA_V, GLA_V, GLA_RANK, SWA_Q, SWA_KV, SWA_KV, D_MODEL, D_MODEL)
IN_PROJ = 2 * GLA_QK + 2 * GLA_V + GLA_RANK + SWA_Q + 2 * SWA_KV + 2 * D_MODEL

kernel_name = "hybrid_gla_swa_macaron_meta_step"

F32 = jnp.float32


def rmsnorm(x, g):
    xf = x.astype(F32)
    y = xf * lax.rsqrt(jnp.mean(xf * xf, axis=-1, keepdims=True) + EPS)
    return (y * g.astype(F32)).astype(x.dtype)


def half_ffn(h, g, w_in, w_out):
    u = rmsnorm(h, g) @ w_in
    a, b = jnp.split(u, 2, axis=-1)
    return h + 0.5 * ((jax.nn.silu(a) * b) @ w_out)


def in_projection(h, g, w_in):
    u = rmsnorm(h, g) @ w_in
    parts = []
    off = 0
    for n in SPLITS:
        parts.append(u[..., off:off + n])
        off += n
    return parts


def alibi_slopes():
    h = jnp.arange(1, SWA_HEADS + 1, dtype=F32)
    return jnp.exp2(-8.0 * h / SWA_HEADS).reshape(SWA_KV_HEADS, SWA_GROUP)


def sink_softmax(s, sinks):
    sk = sinks.astype(F32).reshape(SWA_KV_HEADS, SWA_GROUP, 1)
    m = jnp.maximum(jnp.max(s, axis=-1), sk)
    p = jnp.exp(s - m[..., None])
    return p / (jnp.sum(p, axis=-1, keepdims=True) + jnp.exp(sk - m)[..., None])


def gla_inputs(gq, gk, gv, gad, a_up, a_bias):
    lead = gq.shape[:-1]
    q = gq.astype(F32).reshape(*lead, GLA_HEADS, GLA_DK) * (GLA_DK ** -0.5)
    k = gk.astype(F32).reshape(*lead, GLA_HEADS, GLA_DK)
    v = gv.astype(F32).reshape(*lead, GLA_HEADS, GLA_DV)
    logg = jax.nn.log_sigmoid((gad @ a_up + a_bias).astype(F32)) / GLA_TAU
    return q, k, v, logg.reshape(*lead, GLA_HEADS, GLA_DK)


def gla_scan(q, k, v, logg, s0, chunk):
    B, L, H, _ = q.shape
    n = L // chunk

    def to_chunks(t):
        return jnp.moveaxis(t.reshape(B, n, chunk, H, t.shape[-1]), 1, 0)

    causal = jnp.tril(jnp.ones((chunk, chunk), dtype=bool))[None, :, :, None, None]

    def step(S, xs):
        qc, kc, vc, gc = xs
        b = jnp.cumsum(gc, axis=1)
        diff = b[:, :, None] - b[:, None, :]
        decay = jnp.exp(jnp.where(causal, diff, -jnp.inf))
        att = jnp.einsum('bthd,bshd,btshd->bhts', qc, kc, decay)
        o = (jnp.einsum('bhts,bshv->bthv', att, vc)
             + jnp.einsum('bthd,bhdv->bthv', qc * jnp.exp(b), S))
        b_last = b[:, -1]
        S = (S * jnp.exp(b_last)[..., None]
             + jnp.einsum('bshd,bshv->bhdv', kc * jnp.exp(b_last[:, None] - b), vc))
        return S, o

    S, o = lax.scan(step, s0, (to_chunks(q), to_chunks(k), to_chunks(v), to_chunks(logg)))
    o = jnp.moveaxis(o, 0, 1).reshape(B, L, H, GLA_DV)
    return o, S


def gla_branch_out(o, r, onorm, w_o):
    of = o * lax.rsqrt(jnp.mean(o * o, axis=-1, keepdims=True) + EPS)
    of = of * onorm.astype(F32).reshape(GLA_HEADS, GLA_DV)
    of = of.reshape(*o.shape[:-2], GLA_V) * jax.nn.silu(r.astype(F32))
    return of.astype(r.dtype) @ w_o


def swa_prompt(q, k, v, sinks):
    B, LT, _ = q.shape
    S = LT - N_META
    nb = S // WINDOW
    q = q.astype(F32).reshape(B, LT, SWA_KV_HEADS, SWA_GROUP, SWA_HD) * (SWA_HD ** -0.5)
    k = k.astype(F32).reshape(B, LT, SWA_KV_HEADS, SWA_HD)
    v = v.astype(F32).reshape(B, LT, SWA_KV_HEADS, SWA_HD)
    qm, qr = q[:, :N_META], q[:, N_META:]
    km, kr = k[:, :N_META], k[:, N_META:]
    vm, vr = v[:, :N_META], v[:, N_META:]
    causal_m = jnp.tril(jnp.ones((N_META, N_META), dtype=bool))
    s_mm = jnp.where(causal_m, jnp.einsum('bqkgd,bskd->bkgqs', qm, km), -jnp.inf)
    o_m = jnp.einsum('bkgqs,bskd->bqkgd', sink_softmax(s_mm, sinks), vm)
    qb = qr.reshape(B, nb, WINDOW, SWA_KV_HEADS, SWA_GROUP, SWA_HD)
    kb = kr.reshape(B, nb, WINDOW, SWA_KV_HEADS, SWA_HD)
    vb = vr.reshape(B, nb, WINDOW, SWA_KV_HEADS, SWA_HD)
    pad = ((0, 0), (1, 0), (0, 0), (0, 0), (0, 0))
    kw = jnp.concatenate([jnp.pad(kb, pad)[:, :-1], kb], axis=2)
    vw = jnp.concatenate([jnp.pad(vb, pad)[:, :-1], vb], axis=2)
    i = jnp.arange(WINDOW)[:, None]
    j = jnp.arange(2 * WINDOW)[None, :]
    dist = i + WINDOW - j
    blk = jnp.arange(nb)[:, None, None]
    valid = (dist >= 0) & (dist < WINDOW) & (blk * WINDOW + j[None] - WINDOW >= 0)
    slopes = alibi_slopes()[:, :, None, None]
    s_win = jnp.einsum('bnqkgd,bnskd->bnkgqs', qb, kw) - slopes * dist.astype(F32)
    s_win = jnp.where(valid[None, :, None, None], s_win, -jnp.inf)
    s_meta = jnp.einsum('bnqkgd,bskd->bnkgqs', qb, km)
    p = sink_softmax(jnp.concatenate([s_meta, s_win], axis=-1), sinks)
    o_r = (jnp.einsum('bnkgqs,bskd->bnqkgd', p[..., :N_META], vm)
           + jnp.einsum('bnkgqs,bnskd->bnqkgd', p[..., N_META:], vw))
    return jnp.concatenate([o_m.reshape(B, N_META, SWA_Q), o_r.reshape(B, S, SWA_Q)], axis=1)


def swa_sample(q, k, v, meta_k, meta_v, win_k, win_v, sinks):
    DB, T, _ = q.shape
    WB = win_k.shape[1]
    q = q.astype(F32).reshape(DB, T, SWA_KV_HEADS, SWA_GROUP, SWA_HD) * (SWA_HD ** -0.5)
    kw = jnp.concatenate([win_k.astype(F32), k.astype(F32).reshape(DB, T, SWA_KV_HEADS, SWA_HD)], axis=1)
    vw = jnp.concatenate([win_v.astype(F32), v.astype(F32).reshape(DB, T, SWA_KV_HEADS, SWA_HD)], axis=1)
    i = jnp.arange(T)[:, None]
    c = jnp.arange(WB + T)[None, :]
    dist = i + WB - c
    valid = (dist >= 0) & (dist < WINDOW) & (PAST_LEN - WB + c >= N_META)
    slopes = alibi_slopes()[:, :, None, None]
    s_win = jnp.einsum('bqkgd,bskd->bkgqs', q, kw) - slopes * dist.astype(F32)
    s_win = jnp.where(valid, s_win, -jnp.inf)
    mk = meta_k.astype(F32)
    mv = meta_v.astype(F32)
    s_meta = jnp.einsum('bqkgd,bskd->bkgqs', q, mk)
    p = sink_softmax(jnp.concatenate([s_meta, s_win], axis=-1), sinks)
    o = (jnp.einsum('bkgqs,bskd->bqkgd', p[..., :N_META], mv)
         + jnp.einsum('bkgqs,bskd->bqkgd', p[..., N_META:], vw))
    new_k = kw[:, -WB:].astype(win_k.dtype)
    new_v = vw[:, -WB:].astype(win_v.dtype)
    return o.reshape(DB, T, SWA_Q), new_k, new_v


def merge(gla_b, swa_b, ga, gb, gate_bias, w_out):
    m = (jax.nn.sigmoid(ga + gate_bias[:D_MODEL]) * gla_b
         + jax.nn.sigmoid(gb + gate_bias[D_MODEL:]) * swa_b)
    return m @ w_out


def setup_inputs(seed: int = 0) -> dict:
    key = jax.random.key(seed)
    ks = jax.random.split(key, 32)
    WB = min(WINDOW, PAST_LEN)

    def nrm(k, shape, scale):
        return jax.random.normal(k, shape, F32) * scale

    def gain(k, shape):
        return 1.0 + 0.02 * jax.random.normal(k, shape, F32)

    return {
        "x_prompt": nrm(ks[0], (BATCH, SEQ, D_MODEL), 1.0),
        "x_sample": nrm(ks[1], (DEC_BATCH, DEC_SEQ, D_MODEL), 1.0),
        "cache_meta_k": nrm(ks[2], (DEPTH, DEC_BATCH, N_META, SWA_KV_HEADS, SWA_HD), 1.0),
        "cache_meta_v": nrm(ks[3], (DEPTH, DEC_BATCH, N_META, SWA_KV_HEADS, SWA_HD), 1.0),
        "cache_win_k": nrm(ks[4], (DEPTH, DEC_BATCH, WB, SWA_KV_HEADS, SWA_HD), 1.0),
        "cache_win_v": nrm(ks[5], (DEPTH, DEC_BATCH, WB, SWA_KV_HEADS, SWA_HD), 1.0),
        "state_gla": nrm(ks[6], (DEPTH, DEC_BATCH, GLA_HEADS, GLA_DK, GLA_DV), 1.0),
        "meta_tokens": nrm(ks[7], (N_META, D_MODEL), 1.0),
        "ffn1_norm": gain(ks[8], (DEPTH, D_MODEL)),
        "ffn1_w_in": nrm(ks[9], (DEPTH, D_MODEL, 2 * D_FF), D_MODEL ** -0.5),
        "ffn1_w_out": nrm(ks[10], (DEPTH, D_FF, D_MODEL), D_FF ** -0.5),
        "mix_norm": gain(ks[11], (DEPTH, D_MODEL)),
        "w_in": nrm(ks[12], (DEPTH, D_MODEL, IN_PROJ), D_MODEL ** -0.5),
        "gla_a_up": nrm(ks[13], (DEPTH, GLA_RANK, GLA_QK), GLA_RANK ** -0.5),
        "gla_a_bias": nrm(ks[14], (DEPTH, GLA_QK), 0.1),
        "gla_out_norm": gain(ks[15], (DEPTH, GLA_V)),
        "w_gla_o": nrm(ks[16], (DEPTH, GLA_V, D_MODEL), GLA_V ** -0.5),
        "swa_sinks": nrm(ks[17], (DEPTH, SWA_HEADS), 0.5),
        "w_swa_o": nrm(ks[18], (DEPTH, SWA_Q, D_MODEL), SWA_Q ** -0.5),
        "gate_bias": nrm(ks[19], (DEPTH, 2 * D_MODEL), 0.1),
        "w_out": nrm(ks[20], (DEPTH, D_MODEL, D_MODEL), D_MODEL ** -0.5),
        "ffn2_norm": gain(ks[21], (DEPTH, D_MODEL)),
        "ffn2_w_in": nrm(ks[22], (DEPTH, D_MODEL, 2 * D_FF), D_MODEL ** -0.5),
        "ffn2_w_out": nrm(ks[23], (DEPTH, D_FF, D_MODEL), D_FF ** -0.5),
        "final_norm": gain(ks[24], (D_MODEL,)),
    }


def reference(x_prompt, x_sample, cache_meta_k, cache_meta_v, cache_win_k, cache_win_v, state_gla,
              meta_tokens, ffn1_norm, ffn1_w_in, ffn1_w_out, mix_norm, w_in, gla_a_up, gla_a_bias,
              gla_out_norm, w_gla_o, swa_sinks, w_swa_o, gate_bias, w_out, ffn2_norm, ffn2_w_in,
              ffn2_w_out, final_norm):
    B = x_prompt.shape[0]
    DB = x_sample.shape[0]
    meta = jnp.broadcast_to(meta_tokens.astype(x_prompt.dtype)[None], (B, N_META, D_MODEL))
    hp = jnp.concatenate([meta, x_prompt], axis=1)
    hs = x_sample
    p_meta_k, p_meta_v, p_win_k, p_win_v, p_gla = [], [], [], [], []
    s_win_k, s_win_v, s_gla = [], [], []
    for l in range(DEPTH):
        hp = half_ffn(hp, ffn1_norm[l], ffn1_w_in[l], ffn1_w_out[l])
        hs = half_ffn(hs, ffn1_norm[l], ffn1_w_in[l], ffn1_w_out[l])

        gq, gk, gv, gr, gad, sq, sk, sv, ga, gb = in_projection(hp, mix_norm[l], w_in[l])
        q, k, v, logg = gla_inputs(gq, gk, gv, gad, gla_a_up[l], gla_a_bias[l])
        s0 = jnp.zeros((B, GLA_HEADS, GLA_DK, GLA_DV), F32)
        o_meta, st_meta = gla_scan(q[:, :N_META], k[:, :N_META], v[:, :N_META], logg[:, :N_META], s0, N_META)
        o_real, st_fin = gla_scan(q[:, N_META:], k[:, N_META:], v[:, N_META:], logg[:, N_META:], st_meta, GLA_CHUNK)
        gla_b = gla_branch_out(jnp.concatenate([o_meta, o_real], axis=1), gr, gla_out_norm[l], w_gla_o[l])
        swa_b = swa_prompt(sq, sk, sv, swa_sinks[l]).astype(hp.dtype) @ w_swa_o[l]
        hp = hp + merge(gla_b, swa_b, ga, gb, gate_bias[l], w_out[l])
        p_meta_k.append(sk[:, :N_META].reshape(B, N_META, SWA_KV_HEADS, SWA_HD))
        p_meta_v.append(sv[:, :N_META].reshape(B, N_META, SWA_KV_HEADS, SWA_HD))
        p_win_k.append(sk[:, -WINDOW:].reshape(B, WINDOW, SWA_KV_HEADS, SWA_HD))
        p_win_v.append(sv[:, -WINDOW:].reshape(B, WINDOW, SWA_KV_HEADS, SWA_HD))
        p_gla.append(st_fin.astype(hp.dtype))

        gq, gk, gv, gr, gad, sq, sk, sv, ga, gb = in_projection(hs, mix_norm[l], w_in[l])
        q, k, v, logg = gla_inputs(gq, gk, gv, gad, gla_a_up[l], gla_a_bias[l])
        o_s, st_s = gla_scan(q, k, v, logg, state_gla[l].astype(F32), q.shape[1])
        gla_b = gla_branch_out(o_s, gr, gla_out_norm[l], w_gla_o[l])
        o_w, nwk, nwv = swa_sample(sq, sk, sv, cache_meta_k[l], cache_meta_v[l],
                                   cache_win_k[l], cache_win_v[l], swa_sinks[l])
        swa_b = o_w.astype(hs.dtype) @ w_swa_o[l]
        hs = hs + merge(gla_b, swa_b, ga, gb, gate_bias[l], w_out[l])
        s_win_k.append(nwk)
        s_win_v.append(nwv)
        s_gla.append(st_s.astype(state_gla.dtype))

        hp = half_ffn(hp, ffn2_norm[l], ffn2_w_in[l], ffn2_w_out[l])
        hs = half_ffn(hs, ffn2_norm[l], ffn2_w_in[l], ffn2_w_out[l])

    y_prompt = rmsnorm(hp[:, N_META:], final_norm)
    y_sample = rmsnorm(hs, final_norm)
    return (y_prompt, y_sample,
            jnp.stack(p_meta_k), jnp.stack(p_meta_v), jnp.stack(p_win_k), jnp.stack(p_win_v), jnp.stack(p_gla),
            jnp.stack(s_win_k), jnp.stack(s_win_v), jnp.stack(s_gla))
```

```python
import functools

import jax
import jax.numpy as jnp
from jax import lax
from jax.experimental import pallas as pl
from jax.experimental.pallas import tpu as pltpu

F32 = jnp.float32
BF16 = jnp.bfloat16

D_MODEL = 2048
N_META = 16
GLA_HEADS = 4
GLA_DK = 128
GLA_DV = 256
GLA_RANK = 16
GLA_TAU = 16.0
SWA_HEADS = 16
SWA_KV_HEADS = 4
SWA_GROUP = SWA_HEADS // SWA_KV_HEADS
SWA_HD = 64
WINDOW = 128
D_FF = 5632
EPS = 1e-6

GLA_QK = GLA_HEADS * GLA_DK
GLA_V = GLA_HEADS * GLA_DV
SWA_Q = SWA_HEADS * SWA_HD
SWA_KV = SWA_KV_HEADS * SWA_HD

COL_GA = 0
COL_GB = COL_GA + D_MODEL
COL_GQ = COL_GB + D_MODEL
COL_GK = COL_GQ + GLA_QK
COL_GV = COL_GK + GLA_QK
COL_GR = COL_GV + GLA_V
COL_SQ = COL_GR + GLA_V
COL_SK = COL_SQ + SWA_Q
COL_SV = COL_SK + SWA_KV
N_MAIN = COL_SV + SWA_KV

GLA_SUB = 16
MASK_NEG = -1e30
MIB = 1024 * 1024


def _rms_scale(x):
    return x * lax.rsqrt(jnp.mean(x * x, axis=-1, keepdims=True) + EPS)


def _log_sigmoid(x):
    return jnp.minimum(x, 0.0) - jnp.log(1.0 + jnp.exp(-jnp.abs(x)))


def _dot(a, b):
    return jnp.dot(a, b, preferred_element_type=F32)


def _dot_nt(a, b):
    return lax.dot_general(a, b, (((1,), (1,)), ((), ())), preferred_element_type=F32)


def _row_to_col(row, n):
    r = lax.broadcasted_iota(jnp.int32, (n, n), 0)
    c = lax.broadcasted_iota(jnp.int32, (n, n), 1)
    return jnp.sum(jnp.where(r == c, jnp.broadcast_to(row, (n, n)), 0.0), axis=1, keepdims=True)


def _ffn_kernel(x_ref, g_ref, wa_ref, wb_ref, wo_ref, fg_ref, o_ref, xn_ref, *, n_ff, final_norm):
    j = pl.program_id(1)

    @pl.when(j == 0)
    def _():
        x = x_ref[...]
        xn_ref[...] = (_rms_scale(x) * g_ref[...]).astype(BF16)
        o_ref[...] = x

    xn = xn_ref[...]
    a = _dot(xn, wa_ref[...])
    b = _dot(xn, wb_ref[...])
    h = (0.5 * a) * jax.nn.sigmoid(a) * b
    o_ref[...] += _dot(h.astype(BF16), wo_ref[...])

    if final_norm:
        @pl.when(j == n_ff - 1)
        def _():
            o_ref[...] = _rms_scale(o_ref[...]) * fg_ref[...]


def _ffn(x, gain, w_in, w_out, final_gain, *, tm, tf, final_norm, name):
    rows = x.shape[0]
    n_ff = D_FF // tf
    return pl.pallas_call(
        functools.partial(_ffn_kernel, n_ff=n_ff, final_norm=final_norm),
        grid=(rows // tm, n_ff),
        in_specs=[
            pl.BlockSpec((tm, D_MODEL), lambda i, j: (i, 0)),
            pl.BlockSpec((1, D_MODEL), lambda i, j: (0, 0)),
            pl.BlockSpec((D_MODEL, tf), lambda i, j: (0, j)),
            pl.BlockSpec((D_MODEL, tf), lambda i, j: (0, j + n_ff)),
            pl.BlockSpec((tf, D_MODEL), lambda i, j: (j, 0)),
            pl.BlockSpec((1, D_MODEL), lambda i, j: (0, 0)),
        ],
        out_specs=pl.BlockSpec((tm, D_MODEL), lambda i, j: (i, 0)),
        out_shape=jax.ShapeDtypeStruct((rows // tm * tm, D_MODEL), F32),
        scratch_shapes=[pltpu.VMEM((tm, D_MODEL), BF16)],
        compiler_params=pltpu.CompilerParams(
            dimension_semantics=("parallel", "arbitrary"), vmem_limit_bytes=48 * MIB),
        name=name,
    )(x, gain, w_in, w_in, w_out, final_gain)


def _inproj_kernel(x_ref, g_ref, w_ref, wg_ref, u_ref, gad_ref, xn_ref):
    j = pl.program_id(1)

    @pl.when(j == 0)
    def _():
        xn = (_rms_scale(x_ref[...]) * g_ref[...]).astype(BF16)
        xn_ref[...] = xn
        gad_ref[...] = _dot(xn, wg_ref[...])

    u_ref[...] = _dot(xn_ref[...], w_ref[...])


def _inproj(x, gain, w_main, w_gad, *, tm, tn, name):
    rows = x.shape[0]
    return pl.pallas_call(
        _inproj_kernel,
        grid=(rows // tm, N_MAIN // tn),
        in_specs=[
            pl.BlockSpec((tm, D_MODEL), lambda i, j: (i, 0)),
            pl.BlockSpec((1, D_MODEL), lambda i, j: (0, 0)),
            pl.BlockSpec((D_MODEL, tn), lambda i, j: (0, j)),
            pl.BlockSpec((D_MODEL, GLA_RANK), lambda i, j: (0, 0)),
        ],
        out_specs=[
            pl.BlockSpec((tm, tn), lambda i, j: (i, j)),
            pl.BlockSpec((tm, GLA_RANK), lambda i, j: (i, 0)),
        ],
        out_shape=[
            jax.ShapeDtypeStruct((rows, N_MAIN), F32),
            jax.ShapeDtypeStruct((rows, GLA_RANK), F32),
        ],
        scratch_shapes=[pltpu.VMEM((tm, D_MODEL), BF16)],
        compiler_params=pltpu.CompilerParams(
            dimension_semantics=("parallel", "arbitrary"), vmem_limit_bytes=40 * MIB),
        name=name,
    )(x, gain, w_main, w_gad)


def _gla_kernel(q_ref, k_ref, v_ref, gad_ref, aup_ref, ab_ref, s0_ref, o_ref, sfin_ref, s_ref, *, C, nc):
    n = pl.program_id(2)

    @pl.when(n == 0)
    def _():
        s_ref[...] = s0_ref[0]

    q = q_ref[...] * (GLA_DK ** -0.5)
    k = k_ref[...]
    v = v_ref[...]
    logits = _dot(gad_ref[...].astype(BF16), aup_ref[...].astype(BF16)) + ab_ref[...]
    g = _log_sigmoid(logits) / GLA_TAU

    ri = lax.broadcasted_iota(jnp.int32, (C, C), 0)
    ci = lax.broadcasted_iota(jnp.int32, (C, C), 1)
    tri = jnp.where(ri >= ci, 1.0, 0.0).astype(BF16)
    g_hi = g.astype(BF16)
    r1 = g - g_hi.astype(F32)
    g_mid = r1.astype(BF16)
    g_lo = (r1 - g_mid.astype(F32)).astype(BF16)
    b = _dot(tri, g_hi) + _dot(tri, g_mid) + _dot(tri, g_lo)

    state = s_ref[...]
    o = _dot((q * jnp.exp(b)).astype(BF16), state.astype(BF16))

    rowd = lax.broadcasted_iota(jnp.int32, (C, GLA_DK), 0)
    att = None
    m = C // 2
    while m >= GLA_SUB:
        nblk = C // (2 * m)
        pieces = [jnp.broadcast_to(b[i * 2 * m + m - 1:i * 2 * m + m, :], (2 * m, GLA_DK))
                  for i in range(nblk)]
        bref = pieces[0] if nblk == 1 else jnp.concatenate(pieces, axis=0)
        e = jnp.exp(-jnp.abs(b - bref))
        second = (rowd & m) != 0
        ql = jnp.where(second, q * e, 0.0).astype(BF16)
        kl = jnp.where(second, 0.0, k * e).astype(BF16)
        a = _dot_nt(ql, kl)
        if nblk > 1:
            a = jnp.where((ri ^ ci) < 2 * m, a, 0.0)
        att = a if att is None else att + a
        m //= 2
    if att is not None:
        o = o + _dot(att.astype(BF16), v.astype(BF16))

    r16 = lax.broadcasted_iota(jnp.int32, (GLA_SUB, GLA_DK), 0)
    outs = []
    for i in range(C // GLA_SUB):
        sl = slice(GLA_SUB * i, GLA_SUB * (i + 1))
        bb, qq, kk, vv = b[sl], q[sl], k[sl], v[sl]
        od = o[sl]
        for s in range(GLA_SUB):
            d = jnp.where(r16 >= s, bb - bb[s:s + 1], MASK_NEG)
            w = jnp.sum(qq * jnp.exp(d) * kk[s:s + 1], axis=-1, keepdims=True)
            od = od + w * vv[s:s + 1]
        outs.append(od)
    o_ref[...] = outs[0] if len(outs) == 1 else jnp.concatenate(outs, axis=0)

    bl = b[C - 1:C, :]
    kd = k * jnp.exp(bl - b)
    vp = v
    if C < GLA_DK:
        kd = jnp.concatenate([kd, jnp.zeros((GLA_DK - C, GLA_DK), F32)], axis=0)
        vp = jnp.concatenate([v, jnp.zeros((GLA_DK - C, GLA_DV), F32)], axis=0)
    new_state = state * _row_to_col(jnp.exp(bl), GLA_DK) + _dot(kd.T.astype(BF16), vp.astype(BF16))
    s_ref[...] = new_state

    @pl.when(n == nc - 1)
    def _():
        sfin_ref[0, 0] = new_state


def _gla(u, gad, a_up, a_bias, s0, *, batch, seq, C, row_off, name):
    nc = seq // C
    qb, kb, vb = COL_GQ // GLA_DK, COL_GK // GLA_DK, COL_GV // GLA_DV
    return pl.pallas_call(
        functools.partial(_gla_kernel, C=C, nc=nc),
        grid=(batch, GLA_HEADS, nc),
        in_specs=[
            pl.BlockSpec((C, GLA_DK), lambda b, h, n: (row_off + b * nc + n, qb + h)),
            pl.BlockSpec((C, GLA_DK), lambda b, h, n: (row_off + b * nc + n, kb + h)),
            pl.BlockSpec((C, GLA_DV), lambda b, h, n: (row_off + b * nc + n, vb + h)),
            pl.BlockSpec((C, GLA_RANK), lambda b, h, n: (row_off + b * nc + n, 0)),
            pl.BlockSpec((GLA_RANK, GLA_DK), lambda b, h, n: (0, h)),
            pl.BlockSpec((1, GLA_DK), lambda b, h, n: (0, h)),
            pl.BlockSpec((1, GLA_DK, GLA_DV), lambda b, h, n: (h, 0, 0)),
        ],
        out_specs=[
            pl.BlockSpec((C, GLA_DV), lambda b, h, n: (b * nc + n, h)),
            pl.BlockSpec((1, 1, GLA_DK, GLA_DV), lambda b, h, n: (b, h, 0, 0)),
        ],
        out_shape=[
            jax.ShapeDtypeStruct((batch * seq, GLA_V), F32),
            jax.ShapeDtypeStruct((batch, GLA_HEADS, GLA_DK, GLA_DV), F32),
        ],
        scratch_shapes=[pltpu.VMEM((GLA_DK, GLA_DV), F32)],
        compiler_params=pltpu.CompilerParams(
            dimension_semantics=("parallel", "parallel", "arbitrary"), vmem_limit_bytes=32 * MIB),
        name=name,
    )(u, u, u, gad, a_up, a_bias, s0)


def _gla_step_kernel(q_ref, k_ref, v_ref, gad_ref, aup_ref, ab_ref, s_ref, o_ref, sn_ref):
    q = q_ref[0] * (GLA_DK ** -0.5)
    k = k_ref[0]
    v = v_ref[0]
    gl = jnp.broadcast_to(gad_ref[0], (8, GLA_RANK)).astype(BF16)
    logits = _dot(gl, aup_ref[...].astype(BF16))[0:1] + ab_ref[...]
    dec = jnp.exp(_log_sigmoid(logits) / GLA_TAU)
    outs = []
    for h in range(GLA_HEADS):
        sl = slice(GLA_DK * h, GLA_DK * (h + 1))
        new_state = (s_ref[0, h] * _row_to_col(dec[:, sl], GLA_DK)
                     + _row_to_col(k[:, sl], GLA_DK) * v[:, GLA_DV * h:GLA_DV * (h + 1)])
        sn_ref[0, h] = new_state
        outs.append(jnp.sum(_row_to_col(q[:, sl], GLA_DK) * new_state, axis=0, keepdims=True))
    o_ref[0] = jnp.concatenate(outs, axis=1)


def _gla_step(u3, gad3, a_up, a_bias, state, *, batch, name):
    return pl.pallas_call(
        _gla_step_kernel,
        grid=(batch,),
        in_specs=[
            pl.BlockSpec((1, 1, GLA_QK), lambda b: (b, 0, COL_GQ // GLA_QK)),
            pl.BlockSpec((1, 1, GLA_QK), lambda b: (b, 0, COL_GK // GLA_QK)),
            pl.BlockSpec((1, 1, GLA_V), lambda b: (b, 0, COL_GV // GLA_V)),
            pl.BlockSpec((1, 1, GLA_RANK), lambda b: (b, 0, 0)),
            pl.BlockSpec((GLA_RANK, GLA_QK), lambda b: (0, 0)),
            pl.BlockSpec((1, GLA_QK), lambda b: (0, 0)),
            pl.BlockSpec((1, GLA_HEADS, GLA_DK, GLA_DV), lambda b: (b, 0, 0, 0)),
        ],
        out_specs=[
            pl.BlockSpec((1, 1, GLA_V), lambda b: (b, 0, 0)),
            pl.BlockSpec((1, GLA_HEADS, GLA_DK, GLA_DV), lambda b: (b, 0, 0, 0)),
        ],
        out_shape=[
            jax.ShapeDtypeStruct((batch, 1, GLA_V), F32),
            jax.ShapeDtypeStruct((batch, GLA_HEADS, GLA_DK, GLA_DV), F32),
        ],
        compiler_params=pltpu.CompilerParams(dimension_semantics=("parallel",)),
        name=name,
    )(u3, u3, u3, gad3, a_up, a_bias, state)


def _alibi_slope(head):
    return 2.0 ** (-8.0 * (head + 1) / SWA_HEADS)


def _swa_kernel(sink_ref, q_ref, kc_ref, kp_ref, vc_ref, vp_ref, mk_ref, mv_ref, o_ref):
    n = pl.program_id(1)
    W = WINDOW
    qi = lax.broadcasted_iota(jnp.int32, (W, 2 * W), 0)
    kj = lax.broadcasted_iota(jnp.int32, (W, 2 * W), 1)
    dist = qi + W - kj
    valid = (dist >= 0) & (dist < W) & ((kj >= W) | (n > 0))
    distf = dist.astype(F32)
    q = q_ref[...] * (SWA_HD ** -0.5)
    for kvh in range(SWA_KV_HEADS):
        cs = slice(SWA_HD * kvh, SWA_HD * (kvh + 1))
        kw = jnp.concatenate([kp_ref[:, cs], kc_ref[:, cs]], axis=0).astype(BF16)
        vw = jnp.concatenate([vp_ref[:, cs], vc_ref[:, cs]], axis=0).astype(BF16)
        km = mk_ref[:, cs].astype(BF16)
        vm = mv_ref[:, cs].astype(BF16)
        for grp in range(SWA_GROUP):
            head = kvh * SWA_GROUP + grp
            hs = slice(SWA_HD * head, SWA_HD * (head + 1))
            qh = q[:, hs].astype(BF16)
            s = _dot_nt(qh, kw) - _alibi_slope(head) * distf
            s = jnp.where(valid, s, MASK_NEG)
            sm = _dot_nt(qh, km)
            sink = sink_ref[head]
            mx = jnp.maximum(jnp.maximum(jnp.max(s, axis=-1, keepdims=True),
                                         jnp.max(sm, axis=-1, keepdims=True)), sink)
            p = jnp.exp(s - mx)
            pm = jnp.exp(sm - mx)
            den = (jnp.sum(p, axis=-1, keepdims=True) + jnp.sum(pm, axis=-1, keepdims=True)
                   + jnp.exp(sink - mx))
            o_ref[:, hs] = (_dot(p.astype(BF16), vw) + _dot(pm.astype(BF16), vm)) / den


def _swa(sinks, u, u_small, *, batch, seq, meta_row_block, name):
    nb = seq // WINDOW
    kcol, vcol = COL_SK // SWA_KV, COL_SV // SWA_KV
    return pl.pallas_call(
        _swa_kernel,
        grid=(batch, nb),
        in_specs=[
            pl.BlockSpec(memory_space=pltpu.SMEM),
            pl.BlockSpec((WINDOW, SWA_Q), lambda b, n: (b * nb + n, COL_SQ // SWA_Q)),
            pl.BlockSpec((WINDOW, SWA_KV), lambda b, n: (b * nb + n, kcol)),
            pl.BlockSpec((WINDOW, SWA_KV), lambda b, n: (b * nb + jnp.maximum(n - 1, 0), kcol)),
            pl.BlockSpec((WINDOW, SWA_KV), lambda b, n: (b * nb + n, vcol)),
            pl.BlockSpec((WINDOW, SWA_KV), lambda b, n: (b * nb + jnp.maximum(n - 1, 0), vcol)),
            pl.BlockSpec((N_META, SWA_KV), lambda b, n: (meta_row_block, kcol)),
            pl.BlockSpec((N_META, SWA_KV), lambda b, n: (meta_row_block, vcol)),
        ],
        out_specs=pl.BlockSpec((WINDOW, SWA_Q), lambda b, n: (b * nb + n, 0)),
        out_shape=jax.ShapeDtypeStruct((batch * seq, SWA_Q), F32),
        compiler_params=pltpu.CompilerParams(
            dimension_semantics=("parallel", "parallel"), vmem_limit_bytes=32 * MIB),
        name=name,
    )(sinks, u, u, u, u, u, u_small, u_small)


def _swa_step_kernel(q_ref, kw_ref, vw_ref, mk_ref, mv_ref, sink_ref, slope_ref, o_ref):
    q = (q_ref[0] * (SWA_HD ** -0.5)).astype(BF16)
    lane = lax.broadcasted_iota(jnp.int32, (1, WINDOW), 1)
    distf = (WINDOW - 1 - lane).astype(F32)
    sink = sink_ref[...]
    slope = slope_ref[...]
    head_row = lax.broadcasted_iota(jnp.int32, (SWA_HEADS, SWA_HD), 0)
    acc = jnp.zeros((SWA_HEADS, SWA_HD), F32)
    for kvh in range(SWA_KV_HEADS):
        cs = slice(SWA_HD * kvh, SWA_HD * (kvh + 1))
        kw = kw_ref[0][:, cs].astype(BF16)
        vw = vw_ref[0][:, cs].astype(BF16)
        km = mk_ref[0][:, cs].astype(BF16)
        vm = mv_ref[0][:, cs].astype(BF16)
        s = _dot_nt(q, kw) - slope * distf
        sm = _dot_nt(q, km)
        mx = jnp.maximum(jnp.maximum(jnp.max(s, axis=-1, keepdims=True),
                                     jnp.max(sm, axis=-1, keepdims=True)), sink)
        p = jnp.exp(s - mx)
        pm = jnp.exp(sm - mx)
        den = (jnp.sum(p, axis=-1, keepdims=True) + jnp.sum(pm, axis=-1, keepdims=True)
               + jnp.exp(sink - mx))
        oh = (_dot(p.astype(BF16), vw) + _dot(pm.astype(BF16), vm)) / den
        in_group = (head_row >= kvh * SWA_GROUP) & (head_row < (kvh + 1) * SWA_GROUP)
        acc = jnp.where(in_group, oh, acc)
    o_ref[0] = acc


def _swa_step(q3, kw, vw, mk, mv, sinks_col, slopes_col, *, batch, name):
    return pl.pallas_call(
        _swa_step_kernel,
        grid=(batch,),
        in_specs=[
            pl.BlockSpec((1, SWA_HEADS, SWA_HD), lambda b: (b, 0, 0)),
            pl.BlockSpec((1, WINDOW, SWA_KV), lambda b: (b, 0, 0)),
            pl.BlockSpec((1, WINDOW, SWA_KV), lambda b: (b, 0, 0)),
            pl.BlockSpec((1, N_META, SWA_KV), lambda b: (b, 0, 0)),
            pl.BlockSpec((1, N_META, SWA_KV), lambda b: (b, 0, 0)),
            pl.BlockSpec((SWA_HEADS, 1), lambda b: (0, 0)),
            pl.BlockSpec((SWA_HEADS, 1), lambda b: (0, 0)),
        ],
        out_specs=pl.BlockSpec((1, SWA_HEADS, SWA_HD), lambda b: (b, 0, 0)),
        out_shape=jax.ShapeDtypeStruct((batch, SWA_HEADS, SWA_HD), F32),
        compiler_params=pltpu.CompilerParams(dimension_semantics=("parallel",)),
        name=name,
    )(q3, kw, vw, mk, mv, sinks_col, slopes_col)


def _merge_kernel(h_ref, og_ref, gr_ref, os_ref, ga_ref, gb_ref, onorm_ref, gbias_ref,
                  wg_ref, ws_ref, wo_ref, out_ref):
    og = og_ref[...]
    parts = []
    for h in range(GLA_HEADS):
        oh = og[:, GLA_DV * h:GLA_DV * (h + 1)]
        parts.append(oh * lax.rsqrt(jnp.mean(oh * oh, axis=-1, keepdims=True) + EPS))
    gr = gr_ref[...]
    of = jnp.concatenate(parts, axis=1) * onorm_ref[...] * (gr * jax.nn.sigmoid(gr))
    gla_b = _dot(of.astype(BF16), wg_ref[...])
    swa_b = _dot(os_ref[...].astype(BF16), ws_ref[...])
    gbias = gbias_ref[...]
    mix = (jax.nn.sigmoid(ga_ref[...] + gbias[:, :D_MODEL]) * gla_b
           + jax.nn.sigmoid(gb_ref[...] + gbias[:, D_MODEL:]) * swa_b)
    out_ref[...] = h_ref[...] + _dot(mix.astype(BF16), wo_ref[...])


def _merge(h, o_gla, u, o_swa, onorm, gbias, w_gla_o, w_swa_o, w_out, *, rows, tm, name):
    const = dict(pipeline_mode=pl.Buffered(1))
    return pl.pallas_call(
        _merge_kernel,
        grid=(rows // tm,),
        in_specs=[
            pl.BlockSpec((tm, D_MODEL), lambda i: (i, 0)),
            pl.BlockSpec((tm, GLA_V), lambda i: (i, 0)),
            pl.BlockSpec((tm, GLA_V), lambda i: (i, COL_GR // GLA_V)),
            pl.BlockSpec((tm, SWA_Q), lambda i: (i, 0)),
            pl.BlockSpec((tm, D_MODEL), lambda i: (i, COL_GA // D_MODEL)),
            pl.BlockSpec((tm, D_MODEL), lambda i: (i, COL_GB // D_MODEL)),
            pl.BlockSpec((1, GLA_V), lambda i: (0, 0)),
            pl.BlockSpec((1, 2 * D_MODEL), lambda i: (0, 0)),
            pl.BlockSpec((GLA_V, D_MODEL), lambda i: (0, 0), **const),
            pl.BlockSpec((SWA_Q, D_MODEL), lambda i: (0, 0), **const),
            pl.BlockSpec((D_MODEL, D_MODEL), lambda i: (0, 0), **const),
        ],
        out_specs=pl.BlockSpec((tm, D_MODEL), lambda i: (i, 0)),
        out_shape=jax.ShapeDtypeStruct((rows, D_MODEL), F32),
        compiler_params=pltpu.CompilerParams(
            dimension_semantics=("parallel",), vmem_limit_bytes=52 * MIB),
        name=name,
    )(h, o_gla, u, o_swa, u, u, onorm, gbias, w_gla_o, w_swa_o, w_out)


def kernel(x_prompt, x_sample, cache_meta_k, cache_meta_v, cache_win_k, cache_win_v, state_gla,
           meta_tokens, ffn1_norm, ffn1_w_in, ffn1_w_out, mix_norm, w_in, gla_a_up, gla_a_bias,
           gla_out_norm, w_gla_o, swa_sinks, w_swa_o, gate_bias, w_out, ffn2_norm, ffn2_w_in,
           ffn2_w_out, final_norm):
    B, S, _ = x_prompt.shape
    DB = x_sample.shape[0]
    assert x_sample.shape[1] == 1 and ffn1_norm.shape[0] == 1
    n_small = DB + N_META

    w1i, w1o = ffn1_w_in[0].astype(BF16), ffn1_w_out[0].astype(BF16)
    w2i, w2o = ffn2_w_in[0].astype(BF16), ffn2_w_out[0].astype(BF16)
    wi = w_in[0]
    c_gad = 2 * GLA_QK + 2 * GLA_V
    c_sq = c_gad + GLA_RANK
    c_ga = c_sq + SWA_Q + 2 * SWA_KV
    w_main = jnp.concatenate([wi[:, c_ga:], wi[:, :c_gad], wi[:, c_sq:c_ga]], axis=1).astype(BF16)
    w_gad = wi[:, c_gad:c_sq].astype(BF16)
    wgo, wso, wout = w_gla_o[0].astype(BF16), w_swa_o[0].astype(BF16), w_out[0].astype(BF16)
    g1, gm, g2 = ffn1_norm, mix_norm, ffn2_norm
    gf = final_norm.reshape(1, D_MODEL)
    a_up, a_bias = gla_a_up[0], gla_a_bias
    onorm, gbias = gla_out_norm, gate_bias
    sinks = swa_sinks[0]

    xs = jnp.concatenate([x_sample.reshape(DB, D_MODEL), meta_tokens.astype(F32)], axis=0)
    hs1 = _ffn(xs, g1, w1i, w1o, gf, tm=n_small, tf=512, final_norm=False, name="ffn1_small")
    us, gads = _inproj(hs1, gm, w_main, w_gad, tm=n_small, tn=512, name="inproj_small")
    meta_blk = DB // N_META
    _, st_meta = _gla(us, gads, a_up, a_bias, jnp.zeros((GLA_HEADS, GLA_DK, GLA_DV), F32),
                      batch=1, seq=N_META, C=N_META, row_off=meta_blk, name="gla_meta")
    st_meta = st_meta[0]

    xp = x_prompt.reshape(B * S, D_MODEL)
    hp1 = _ffn(xp, g1, w1i, w1o, gf, tm=512, tf=512, final_norm=False, name="ffn1")
    up, gadp = _inproj(hp1, gm, w_main, w_gad, tm=1024, tn=512, name="inproj")
    o_gla, st_fin = _gla(up, gadp, a_up, a_bias, st_meta, batch=B, seq=S, C=128, row_off=0, name="gla")
    o_swa = _swa(sinks, up, us, batch=B, seq=S, meta_row_block=meta_blk, name="swa")
    hp2 = _merge(hp1, o_gla, up, o_swa, onorm, gbias, wgo, wso, wout, rows=B * S, tm=256, name="merge")
    y_prompt = _ffn(hp2, g2, w2i, w2o, gf, tm=512, tf=512, final_norm=True, name="ffn2")

    og_s, st_s = _gla_step(us.reshape(n_small, 1, N_MAIN), gads.reshape(n_small, 1, GLA_RANK),
                           a_up, a_bias, state_gla[0], batch=DB, name="gla_step")
    sk_s = us[:DB, COL_SK:COL_SK + SWA_KV]
    sv_s = us[:DB, COL_SV:COL_SV + SWA_KV]
    WB = cache_win_k.shape[2]
    new_k = jnp.concatenate([cache_win_k[0].reshape(DB, WB, SWA_KV)[:, 1:], sk_s[:, None, :]], axis=1)
    new_v = jnp.concatenate([cache_win_v[0].reshape(DB, WB, SWA_KV)[:, 1:], sv_s[:, None, :]], axis=1)
    slopes = jnp.asarray([_alibi_slope(h) for h in range(SWA_HEADS)], F32).reshape(SWA_HEADS, 1)
    os_s = _swa_step(us[:DB, COL_SQ:COL_SQ + SWA_Q].reshape(DB, SWA_HEADS, SWA_HD), new_k, new_v,
                     cache_meta_k[0].reshape(DB, N_META, SWA_KV), cache_meta_v[0].reshape(DB, N_META, SWA_KV),
                     sinks.reshape(SWA_HEADS, 1), slopes, batch=DB, name="swa_step")
    hs2 = _merge(hs1, og_s.reshape(DB, GLA_V), us, os_s.reshape(DB, SWA_Q), onorm, gbias, wgo, wso, wout,
                 rows=DB, tm=DB, name="merge_small")
    y_sample = _ffn(hs2, g2, w2i, w2o, gf, tm=DB, tf=512, final_norm=True, name="ffn2_small")

    up3 = up.reshape(B, S, N_MAIN)
    kv_shape = (SWA_KV_HEADS, SWA_HD)
    p_meta_k = jnp.broadcast_to(us[DB:, COL_SK:COL_SK + SWA_KV].reshape(1, 1, N_META, *kv_shape),
                                (1, B, N_META, *kv_shape))
    p_meta_v = jnp.broadcast_to(us[DB:, COL_SV:COL_SV + SWA_KV].reshape(1, 1, N_META, *kv_shape),
                                (1, B, N_META, *kv_shape))
    p_win_k = up3[:, S - WINDOW:, COL_SK:COL_SK + SWA_KV].reshape(1, B, WINDOW, *kv_shape)
    p_win_v = up3[:, S - WINDOW:, COL_SV:COL_SV + SWA_KV].reshape(1, B, WINDOW, *kv_shape)
    return (y_prompt.reshape(B, S, D_MODEL), y_sample.reshape(DB, 1, D_MODEL),
            p_meta_k, p_meta_v, p_win_k, p_win_v, st_fin[None],
            new_k.reshape(1, DB, WB, *kv_shape), new_v.reshape(1, DB, WB, *kv_shape), st_s[None])
```

```python
import functools

import jax
import jax.numpy as jnp
from jax import lax
from jax.experimental import pallas as pl
from jax.experimental.pallas import tpu as pltpu

F32 = jnp.float32
BF16 = jnp.bfloat16

D_MODEL = 2048
N_META = 16
GLA_HEADS = 4
GLA_DK = 128
GLA_DV = 256
GLA_RANK = 16
GLA_TAU = 16.0
SWA_HEADS = 16
SWA_KV_HEADS = 4
SWA_GROUP = SWA_HEADS // SWA_KV_HEADS
SWA_HD = 64
WINDOW = 128
D_FF = 5632
EPS = 1e-6

GLA_QK = GLA_HEADS * GLA_DK
GLA_V = GLA_HEADS * GLA_DV
SWA_Q = SWA_HEADS * SWA_HD
SWA_KV = SWA_KV_HEADS * SWA_HD

COL_GA = 0
COL_GB = COL_GA + D_MODEL
COL_GQ = COL_GB + D_MODEL
COL_GK = COL_GQ + GLA_QK
COL_GV = COL_GK + GLA_QK
COL_GR = COL_GV + GLA_V
COL_SQ = COL_GR + GLA_V
COL_SK = COL_SQ + SWA_Q
COL_SV = COL_SK + SWA_KV
N_MAIN = COL_SV + SWA_KV
N_GRP_A = COL_GQ
N_GRP_B = COL_SQ - COL_GQ
N_GRP_C = N_MAIN - COL_SQ

GLA_SUB = 16
MASK_NEG = -1e30
MIB = 1024 * 1024
ROW_CHUNK = 256
COL_CHUNK = 512

SWA_QB = 64
SWA_SPAN = WINDOW + SWA_QB
SWA_KT = 256
SWA_SINK_COL = SWA_SPAN + N_META


def _rms_scale(x):
    return x * lax.rsqrt(jnp.mean(x * x, axis=-1, keepdims=True) + EPS)


def _log_sigmoid(x):
    return jnp.minimum(x, 0.0) - jnp.log(1.0 + jnp.exp(-jnp.abs(x)))


def _dot(a, b):
    return jnp.dot(a, b, preferred_element_type=F32)


def _dot_nt(a, b):
    return lax.dot_general(a, b, (((1,), (1,)), ((), ())), preferred_element_type=F32)


def _row_to_col(row, n):
    r = lax.broadcasted_iota(jnp.int32, (n, n), 0)
    c = lax.broadcasted_iota(jnp.int32, (n, n), 1)
    return jnp.sum(jnp.where(r == c, jnp.broadcast_to(row, (n, n)), 0.0), axis=1, keepdims=True)


def _for_row_chunks(rows, fn):
    chunk = ROW_CHUNK if rows % ROW_CHUNK == 0 else rows

    def body(i, carry):
        fn(pl.ds(pl.multiple_of(i * chunk, chunk), chunk))
        return carry

    lax.fori_loop(0, rows // chunk, body, 0)


def _ffn_kernel(x_ref, g_ref, wa_ref, wb_ref, wo_ref, fg_ref, o_ref, xn_ref, *, n_ff, final_norm):
    j = pl.program_id(1)
    rows = x_ref.shape[0]

    @pl.when(j == 0)
    def _():
        def norm_rows(sl):
            x = x_ref[sl, :]
            xn_ref[sl, :] = (_rms_scale(x) * g_ref[...]).astype(BF16)
            o_ref[sl, :] = x

        _for_row_chunks(rows, norm_rows)

    xn = xn_ref[...]
    a = _dot(xn, wa_ref[...])
    b = _dot(xn, wb_ref[...])
    h = ((0.5 * a) * jax.nn.sigmoid(a) * b).astype(BF16)
    for c in range(D_MODEL // COL_CHUNK):
        cs = slice(COL_CHUNK * c, COL_CHUNK * (c + 1))
        o_ref[:, cs] += _dot(h, wo_ref[:, cs])

    if final_norm:
        @pl.when(j == n_ff - 1)
        def _():
            def final_rows(sl):
                o_ref[sl, :] = _rms_scale(o_ref[sl, :]) * fg_ref[...]

            _for_row_chunks(rows, final_rows)


def _ffn(x, gain, w_in, w_out, final_gain, *, tm, tf, final_norm, name):
    rows = x.shape[0]
    n_ff = D_FF // tf
    return pl.pallas_call(
        functools.partial(_ffn_kernel, n_ff=n_ff, final_norm=final_norm),
        grid=(rows // tm, n_ff),
        in_specs=[
            pl.BlockSpec((tm, D_MODEL), lambda i, j: (i, 0), pipeline_mode=pl.Buffered(1)),
            pl.BlockSpec((1, D_MODEL), lambda i, j: (0, 0)),
            pl.BlockSpec((D_MODEL, tf), lambda i, j: (0, j)),
            pl.BlockSpec((D_MODEL, tf), lambda i, j: (0, j + n_ff)),
            pl.BlockSpec((tf, D_MODEL), lambda i, j: (j, 0)),
            pl.BlockSpec((1, D_MODEL), lambda i, j: (0, 0)),
        ],
        out_specs=pl.BlockSpec((tm, D_MODEL), lambda i, j: (i, 0)),
        out_shape=jax.ShapeDtypeStruct((rows // tm * tm, D_MODEL), F32),
        scratch_shapes=[pltpu.VMEM((tm, D_MODEL), BF16)],
        compiler_params=pltpu.CompilerParams(
            dimension_semantics=("parallel", "arbitrary"), vmem_limit_bytes=58 * MIB),
        name=name,
    )(x, gain, w_in, w_in, w_out, final_gain)


def _inproj_kernel(x_ref, g_ref, wa_ref, wb_ref, wc_ref, wg_ref, u_ref, gad_ref, xn_ref, *, na, nb):
    j = pl.program_id(1)

    @pl.when(j == 0)
    def _():
        def norm_rows(sl):
            xn_ref[sl, :] = (_rms_scale(x_ref[sl, :]) * g_ref[...]).astype(BF16)

        _for_row_chunks(x_ref.shape[0], norm_rows)
        gad_ref[...] = _dot(xn_ref[...], wg_ref[...])

    @pl.when(j < na)
    def _():
        u_ref[...] = _dot(xn_ref[...], wa_ref[...])

    @pl.when((j >= na) & (j < na + nb))
    def _():
        u_ref[...] = _dot(xn_ref[...], wb_ref[...])

    @pl.when(j >= na + nb)
    def _():
        u_ref[...] = _dot(xn_ref[...], wc_ref[...])


def _inproj(x, gain, w_a, w_b, w_c, w_gad, *, tm, tn, name):
    rows = x.shape[0]
    na, nb, ncc = N_GRP_A // tn, N_GRP_B // tn, N_GRP_C // tn
    return pl.pallas_call(
        functools.partial(_inproj_kernel, na=na, nb=nb),
        grid=(rows // tm, na + nb + ncc),
        in_specs=[
            pl.BlockSpec((tm, D_MODEL), lambda i, j: (i, 0), pipeline_mode=pl.Buffered(1)),
            pl.BlockSpec((1, D_MODEL), lambda i, j: (0, 0)),
            pl.BlockSpec((D_MODEL, tn), lambda i, j: (0, jnp.minimum(j, na - 1))),
            pl.BlockSpec((D_MODEL, tn), lambda i, j: (0, jnp.clip(j - na, 0, nb - 1))),
            pl.BlockSpec((D_MODEL, tn), lambda i, j: (0, jnp.clip(j - na - nb, 0, ncc - 1))),
            pl.BlockSpec((D_MODEL, GLA_RANK), lambda i, j: (0, 0)),
        ],
        out_specs=[
            pl.BlockSpec((tm, tn), lambda i, j: (i, j)),
            pl.BlockSpec((tm, GLA_RANK), lambda i, j: (i, 0)),
        ],
        out_shape=[
            jax.ShapeDtypeStruct((rows, N_MAIN), F32),
            jax.ShapeDtypeStruct((rows, GLA_RANK), F32),
        ],
        scratch_shapes=[pltpu.VMEM((tm, D_MODEL), BF16)],
        compiler_params=pltpu.CompilerParams(
            dimension_semantics=("parallel", "arbitrary"), vmem_limit_bytes=52 * MIB),
        name=name,
    )(x, gain, w_a, w_b, w_c, w_gad)


def _gla_head(q, k, v, b, state, *, C):
    o = _dot((q * jnp.exp(b)).astype(BF16), state.astype(BF16))

    ri = lax.broadcasted_iota(jnp.int32, (C, C), 0)
    ci = lax.broadcasted_iota(jnp.int32, (C, C), 1)
    rowd = lax.broadcasted_iota(jnp.int32, (C, GLA_DK), 0)
    att = None
    m = C // 2
    while m >= GLA_SUB:
        nblk = C // (2 * m)
        pieces = [jnp.broadcast_to(b[i * 2 * m + m - 1:i * 2 * m + m, :], (2 * m, GLA_DK))
                  for i in range(nblk)]
        bref = pieces[0] if nblk == 1 else jnp.concatenate(pieces, axis=0)
        e = jnp.exp(-jnp.abs(b - bref))
        second = (rowd & m) != 0
        ql = jnp.where(second, q * e, 0.0).astype(BF16)
        kl = jnp.where(second, 0.0, k * e).astype(BF16)
        a = _dot_nt(ql, kl)
        if nblk > 1:
            a = jnp.where((ri ^ ci) < 2 * m, a, 0.0)
        att = a if att is None else att + a
        m //= 2
    if att is not None:
        o = o + _dot(att.astype(BF16), v.astype(BF16))

    r16 = lax.broadcasted_iota(jnp.int32, (GLA_SUB, GLA_DK), 0)
    outs = []
    for i in range(C // GLA_SUB):
        sl = slice(GLA_SUB * i, GLA_SUB * (i + 1))
        bb, qq, kk, vv = b[sl], q[sl], k[sl], v[sl]
        od = o[sl]
        for s in range(GLA_SUB):
            d = jnp.where(r16 >= s, bb - bb[s:s + 1], MASK_NEG)
            w = jnp.sum(qq * jnp.exp(d) * kk[s:s + 1], axis=-1, keepdims=True)
            od = od + w * vv[s:s + 1]
        outs.append(od)
    o = outs[0] if len(outs) == 1 else jnp.concatenate(outs, axis=0)

    bl = b[C - 1:C, :]
    kd = k * jnp.exp(bl - b)
    vp = v
    if C < GLA_DK:
        kd = jnp.concatenate([kd, jnp.zeros((GLA_DK - C, GLA_DK), F32)], axis=0)
        vp = jnp.concatenate([v, jnp.zeros((GLA_DK - C, GLA_DV), F32)], axis=0)
    new_state = state * _row_to_col(jnp.exp(bl), GLA_DK) + _dot(kd.T.astype(BF16), vp.astype(BF16))
    return o, new_state


def _gla_kernel(q_ref, k_ref, v_ref, gad_ref, aup_ref, ab_ref, s0_ref, o_ref, sfin_ref, s_ref, *, C, nc):
    n = pl.program_id(1)

    @pl.when(n == 0)
    def _():
        s_ref[...] = s0_ref[...]

    logits = _dot(gad_ref[...].astype(BF16), aup_ref[...].astype(BF16)) + ab_ref[...]
    g = _log_sigmoid(logits) / GLA_TAU

    ri = lax.broadcasted_iota(jnp.int32, (C, C), 0)
    ci = lax.broadcasted_iota(jnp.int32, (C, C), 1)
    tri = jnp.where(ri >= ci, 1.0, 0.0).astype(BF16)
    g_hi = g.astype(BF16)
    r1 = g - g_hi.astype(F32)
    g_mid = r1.astype(BF16)
    g_lo = (r1 - g_mid.astype(F32)).astype(BF16)
    b_all = _dot(tri, g_hi) + _dot(tri, g_mid) + _dot(tri, g_lo)

    for h in range(GLA_HEADS):
        ks = slice(GLA_DK * h, GLA_DK * (h + 1))
        vs = slice(GLA_DV * h, GLA_DV * (h + 1))
        o, new_state = _gla_head(q_ref[:, ks] * (GLA_DK ** -0.5), k_ref[:, ks], v_ref[:, vs],
                                 b_all[:, ks], s_ref[h], C=C)
        o_ref[:, vs] = o
        s_ref[h] = new_state

    @pl.when(n == nc - 1)
    def _():
        sfin_ref[0] = s_ref[...]


def _gla(u, gad, a_up, a_bias, s0, *, batch, seq, C, row_off, name):
    nc = seq // C
    return pl.pallas_call(
        functools.partial(_gla_kernel, C=C, nc=nc),
        grid=(batch, nc),
        in_specs=[
            pl.BlockSpec((C, GLA_QK), lambda b, n: (row_off + b * nc + n, COL_GQ // GLA_QK)),
            pl.BlockSpec((C, GLA_QK), lambda b, n: (row_off + b * nc + n, COL_GK // GLA_QK)),
            pl.BlockSpec((C, GLA_V), lambda b, n: (row_off + b * nc + n, COL_GV // GLA_V)),
            pl.BlockSpec((C, GLA_RANK), lambda b, n: (row_off + b * nc + n, 0)),
            pl.BlockSpec((GLA_RANK, GLA_QK), lambda b, n: (0, 0)),
            pl.BlockSpec((1, GLA_QK), lambda b, n: (0, 0)),
            pl.BlockSpec((GLA_HEADS, GLA_DK, GLA_DV), lambda b, n: (0, 0, 0)),
        ],
        out_specs=[
            pl.BlockSpec((C, GLA_V), lambda b, n: (b * nc + n, 0)),
            pl.BlockSpec((1, GLA_HEADS, GLA_DK, GLA_DV), lambda b, n: (b, 0, 0, 0)),
        ],
        out_shape=[
            jax.ShapeDtypeStruct((batch * seq, GLA_V), F32),
            jax.ShapeDtypeStruct((batch, GLA_HEADS, GLA_DK, GLA_DV), F32),
        ],
        scratch_shapes=[pltpu.VMEM((GLA_HEADS, GLA_DK, GLA_DV), F32)],
        compiler_params=pltpu.CompilerParams(
            dimension_semantics=("parallel", "arbitrary"), vmem_limit_bytes=32 * MIB),
        name=name,
    )(u, u, u, gad, a_up, a_bias, s0)


def _gla_step_kernel(q_ref, k_ref, v_ref, gad_ref, aup_ref, ab_ref, s_ref, o_ref, sn_ref, *, bb):
    gl = jnp.concatenate([gad_ref[i] for i in range(bb)] + [jnp.zeros((8 - bb, GLA_RANK), F32)], axis=0)
    logits = _dot(gl.astype(BF16), aup_ref[...].astype(BF16)) + ab_ref[...]
    dec_all = jnp.exp(_log_sigmoid(logits) / GLA_TAU)
    for i in range(bb):
        q = q_ref[i] * (GLA_DK ** -0.5)
        k = k_ref[i]
        v = v_ref[i]
        dec = dec_all[i:i + 1]
        outs = []
        for h in range(GLA_HEADS):
            sl = slice(GLA_DK * h, GLA_DK * (h + 1))
            new_state = (s_ref[i, h] * _row_to_col(dec[:, sl], GLA_DK)
                         + _row_to_col(k[:, sl], GLA_DK) * v[:, GLA_DV * h:GLA_DV * (h + 1)])
            sn_ref[i, h] = new_state
            outs.append(jnp.sum(_row_to_col(q[:, sl], GLA_DK) * new_state, axis=0, keepdims=True))
        o_ref[i] = jnp.concatenate(outs, axis=1)


def _gla_step(u3, gad3, a_up, a_bias, state, *, batch, bb, name):
    return pl.pallas_call(
        functools.partial(_gla_step_kernel, bb=bb),
        grid=(batch // bb,),
        in_specs=[
            pl.BlockSpec((bb, 1, GLA_QK), lambda b: (b, 0, COL_GQ // GLA_QK)),
            pl.BlockSpec((bb, 1, GLA_QK), lambda b: (b, 0, COL_GK // GLA_QK)),
            pl.BlockSpec((bb, 1, GLA_V), lambda b: (b, 0, COL_GV // GLA_V)),
            pl.BlockSpec((bb, 1, GLA_RANK), lambda b: (b, 0, 0)),
            pl.BlockSpec((GLA_RANK, GLA_QK), lambda b: (0, 0)),
            pl.BlockSpec((1, GLA_QK), lambda b: (0, 0)),
            pl.BlockSpec((bb, GLA_HEADS, GLA_DK, GLA_DV), lambda b: (b, 0, 0, 0)),
        ],
        out_specs=[
            pl.BlockSpec((bb, 1, GLA_V), lambda b: (b, 0, 0)),
            pl.BlockSpec((bb, GLA_HEADS, GLA_DK, GLA_DV), lambda b: (b, 0, 0, 0)),
        ],
        out_shape=[
            jax.ShapeDtypeStruct((batch, 1, GLA_V), F32),
            jax.ShapeDtypeStruct((batch, GLA_HEADS, GLA_DK, GLA_DV), F32),
        ],
        compiler_params=pltpu.CompilerParams(dimension_semantics=("parallel",)),
        name=name,
    )(u3, u3, u3, gad3, a_up, a_bias, state)


def _alibi_slope(head):
    return 2.0 ** (-8.0 * (head + 1) / SWA_HEADS)


def _swa_kernel(sink_ref, q_ref, kc_ref, kp_ref, vc_ref, vp_ref, mk_ref, mv_ref, o_ref, bias_ref):
    n = pl.program_id(1)
    rows = SWA_GROUP * SWA_QB

    @pl.when(n <= 1)
    def _():
        r = lax.broadcasted_iota(jnp.int32, (SWA_QB, SWA_KT), 0)
        c = lax.broadcasted_iota(jnp.int32, (SWA_QB, SWA_KT), 1)
        dist = r + WINDOW - c
        distf = dist.astype(F32)
        band = (dist >= 0) & (dist < WINDOW) & (c < SWA_SPAN)
        for a in range(WINDOW // SWA_QB):
            ok = band & ((c >= WINDOW - SWA_QB * a) | (n > 0))
            for head in range(SWA_HEADS):
                val = jnp.where(ok, -_alibi_slope(head) * distf, MASK_NEG)
                val = jnp.where((c >= SWA_SPAN) & (c < SWA_SINK_COL), 0.0, val)
                val = jnp.where(c == SWA_SINK_COL, sink_ref[head], val)
                bias_ref[a, head // SWA_GROUP, pl.ds((head % SWA_GROUP) * SWA_QB, SWA_QB), :] = val

    q = q_ref[...] * (SWA_HD ** -0.5)
    pad = jnp.zeros((SWA_KT - SWA_SPAN - N_META, SWA_HD), BF16)
    for kvh in range(SWA_KV_HEADS):
        cs = slice(SWA_HD * kvh, SWA_HD * (kvh + 1))
        kwin = jnp.concatenate([kp_ref[:, cs], kc_ref[:, cs]], axis=0).astype(BF16)
        vwin = jnp.concatenate([vp_ref[:, cs], vc_ref[:, cs]], axis=0).astype(BF16)
        km = mk_ref[:, cs].astype(BF16)
        vm = mv_ref[:, cs].astype(BF16)
        for a in range(WINDOW // SWA_QB):
            qs = slice(SWA_QB * a, SWA_QB * (a + 1))
            kt = jnp.concatenate([kwin[SWA_QB * a:SWA_QB * a + SWA_SPAN], km, pad], axis=0)
            vt = jnp.concatenate([vwin[SWA_QB * a:SWA_QB * a + SWA_SPAN], vm, pad], axis=0)
            qg = jnp.concatenate(
                [q[qs, SWA_HD * (kvh * SWA_GROUP + grp):SWA_HD * (kvh * SWA_GROUP + grp + 1)]
                 for grp in range(SWA_GROUP)], axis=0).astype(BF16)
            s = _dot_nt(qg, kt) + bias_ref[a, kvh]
            p = jnp.exp(s - jnp.max(s, axis=-1, keepdims=True))
            o = _dot(p.astype(BF16), vt) / jnp.sum(p, axis=-1, keepdims=True)
            for grp in range(SWA_GROUP):
                head = kvh * SWA_GROUP + grp
                o_ref[qs, SWA_HD * head:SWA_HD * (head + 1)] = o[SWA_QB * grp:SWA_QB * (grp + 1)]


def _swa(sinks, u, u_small, *, batch, seq, meta_row_block, name):
    nb = seq // WINDOW
    kcol, vcol = COL_SK // SWA_KV, COL_SV // SWA_KV
    return pl.pallas_call(
        _swa_kernel,
        grid=(batch, nb),
        in_specs=[
            pl.BlockSpec(memory_space=pltpu.SMEM),
            pl.BlockSpec((WINDOW, SWA_Q), lambda b, n: (b * nb + n, COL_SQ // SWA_Q)),
            pl.BlockSpec((WINDOW, SWA_KV), lambda b, n: (b * nb + n, kcol)),
            pl.BlockSpec((WINDOW, SWA_KV), lambda b, n: (b * nb + jnp.maximum(n - 1, 0), kcol)),
            pl.BlockSpec((WINDOW, SWA_KV), lambda b, n: (b * nb + n, vcol)),
            pl.BlockSpec((WINDOW, SWA_KV), lambda b, n: (b * nb + jnp.maximum(n - 1, 0), vcol)),
            pl.BlockSpec((N_META, SWA_KV), lambda b, n: (meta_row_block, kcol)),
            pl.BlockSpec((N_META, SWA_KV), lambda b, n: (meta_row_block, vcol)),
        ],
        out_specs=pl.BlockSpec((WINDOW, SWA_Q), lambda b, n: (b * nb + n, 0)),
        out_shape=jax.ShapeDtypeStruct((batch * seq, SWA_Q), F32),
        scratch_shapes=[pltpu.VMEM((WINDOW // SWA_QB, SWA_KV_HEADS, SWA_GROUP * SWA_QB, SWA_KT), F32)],
        compiler_params=pltpu.CompilerParams(
            dimension_semantics=("arbitrary", "arbitrary"), vmem_limit_bytes=32 * MIB),
        name=name,
    )(sinks, u, u, u, u, u, u_small, u_small)


def _swa_step_kernel(q_ref, kw_ref, vw_ref, mk_ref, mv_ref, sink_ref, slope_ref, o_ref, *, bb):
    lane = lax.broadcasted_iota(jnp.int32, (1, WINDOW), 1)
    distf = (WINDOW - 1 - lane).astype(F32)
    sink = sink_ref[...]
    slope = slope_ref[...]
    head_row = lax.broadcasted_iota(jnp.int32, (SWA_HEADS, SWA_HD), 0)
    for i in range(bb):
        q = (q_ref[i] * (SWA_HD ** -0.5)).astype(BF16)
        acc = jnp.zeros((SWA_HEADS, SWA_HD), F32)
        for kvh in range(SWA_KV_HEADS):
            cs = slice(SWA_HD * kvh, SWA_HD * (kvh + 1))
            kw = kw_ref[i, :, cs].astype(BF16)
            vw = vw_ref[i, :, cs].astype(BF16)
            km = mk_ref[i, :, cs].astype(BF16)
            vm = mv_ref[i, :, cs].astype(BF16)
            s = _dot_nt(q, kw) - slope * distf
            sm = _dot_nt(q, km)
            mx = jnp.maximum(jnp.maximum(jnp.max(s, axis=-1, keepdims=True),
                                         jnp.max(sm, axis=-1, keepdims=True)), sink)
            p = jnp.exp(s - mx)
            pm = jnp.exp(sm - mx)
            den = (jnp.sum(p, axis=-1, keepdims=True) + jnp.sum(pm, axis=-1, keepdims=True)
                   + jnp.exp(sink - mx))
            oh = (_dot(p.astype(BF16), vw) + _dot(pm.astype(BF16), vm)) / den
            in_group = (head_row >= kvh * SWA_GROUP) & (head_row < (kvh + 1) * SWA_GROUP)
            acc = jnp.where(in_group, oh, acc)
        o_ref[i] = acc


def _swa_step(q3, kw, vw, mk, mv, sinks_col, slopes_col, *, batch, bb, name):
    return pl.pallas_call(
        functools.partial(_swa_step_kernel, bb=bb),
        grid=(batch // bb,),
        in_specs=[
            pl.BlockSpec((bb, SWA_HEADS, SWA_HD), lambda b: (b, 0, 0)),
            pl.BlockSpec((bb, WINDOW, SWA_KV), lambda b: (b, 0, 0)),
            pl.BlockSpec((bb, WINDOW, SWA_KV), lambda b: (b, 0, 0)),
            pl.BlockSpec((bb, N_META, SWA_KV), lambda b: (b, 0, 0)),
            pl.BlockSpec((bb, N_META, SWA_KV), lambda b: (b, 0, 0)),
            pl.BlockSpec((SWA_HEADS, 1), lambda b: (0, 0)),
            pl.BlockSpec((SWA_HEADS, 1), lambda b: (0, 0)),
        ],
        out_specs=pl.BlockSpec((bb, SWA_HEADS, SWA_HD), lambda b: (b, 0, 0)),
        out_shape=jax.ShapeDtypeStruct((batch, SWA_HEADS, SWA_HD), F32),
        compiler_params=pltpu.CompilerParams(dimension_semantics=("parallel",)),
        name=name,
    )(q3, kw, vw, mk, mv, sinks_col, slopes_col)


def _merge_kernel(h_ref, og_ref, gr_ref, os_ref, ga_ref, gb_ref, onorm_ref, gbias_ref,
                  wg_ref, ws_ref, wo_ref, out_ref):
    og = og_ref[...]
    parts = []
    for h in range(GLA_HEADS):
        oh = og[:, GLA_DV * h:GLA_DV * (h + 1)]
        parts.append(oh * lax.rsqrt(jnp.mean(oh * oh, axis=-1, keepdims=True) + EPS))
    gr = gr_ref[...]
    of = jnp.concatenate(parts, axis=1) * onorm_ref[...] * (gr * jax.nn.sigmoid(gr))
    gla_b = _dot(of.astype(BF16), wg_ref[...])
    swa_b = _dot(os_ref[...].astype(BF16), ws_ref[...])
    gbias = gbias_ref[...]
    mix = (jax.nn.sigmoid(ga_ref[...] + gbias[:, :D_MODEL]) * gla_b
           + jax.nn.sigmoid(gb_ref[...] + gbias[:, D_MODEL:]) * swa_b)
    out_ref[...] = h_ref[...] + _dot(mix.astype(BF16), wo_ref[...])


def _merge(h, o_gla, u, o_swa, onorm, gbias, w_gla_o, w_swa_o, w_out, *, rows, tm, name):
    const = dict(pipeline_mode=pl.Buffered(1))
    return pl.pallas_call(
        _merge_kernel,
        grid=(rows // tm,),
        in_specs=[
            pl.BlockSpec((tm, D_MODEL), lambda i: (i, 0)),
            pl.BlockSpec((tm, GLA_V), lambda i: (i, 0)),
            pl.BlockSpec((tm, GLA_V), lambda i: (i, COL_GR // GLA_V)),
            pl.BlockSpec((tm, SWA_Q), lambda i: (i, 0)),
            pl.BlockSpec((tm, D_MODEL), lambda i: (i, COL_GA // D_MODEL)),
            pl.BlockSpec((tm, D_MODEL), lambda i: (i, COL_GB // D_MODEL)),
            pl.BlockSpec((1, GLA_V), lambda i: (0, 0)),
            pl.BlockSpec((1, 2 * D_MODEL), lambda i: (0, 0)),
            pl.BlockSpec((GLA_V, D_MODEL), lambda i: (0, 0), **const),
            pl.BlockSpec((SWA_Q, D_MODEL), lambda i: (0, 0), **const),
            pl.BlockSpec((D_MODEL, D_MODEL), lambda i: (0, 0), **const),
        ],
        out_specs=pl.BlockSpec((tm, D_MODEL), lambda i: (i, 0)),
        out_shape=jax.ShapeDtypeStruct((rows, D_MODEL), F32),
        compiler_params=pltpu.CompilerParams(
            dimension_semantics=("parallel",), vmem_limit_bytes=52 * MIB),
        name=name,
    )(h, o_gla, u, o_swa, u, u, onorm, gbias, w_gla_o, w_swa_o, w_out)


def kernel(x_prompt, x_sample, cache_meta_k, cache_meta_v, cache_win_k, cache_win_v, state_gla,
           meta_tokens, ffn1_norm, ffn1_w_in, ffn1_w_out, mix_norm, w_in, gla_a_up, gla_a_bias,
           gla_out_norm, w_gla_o, swa_sinks, w_swa_o, gate_bias, w_out, ffn2_norm, ffn2_w_in,
           ffn2_w_out, final_norm):
    B, S, _ = x_prompt.shape
    DB = x_sample.shape[0]
    assert x_sample.shape[1] == 1 and ffn1_norm.shape[0] == 1
    n_small = DB + N_META

    w1i, w1o = ffn1_w_in[0].astype(BF16), ffn1_w_out[0].astype(BF16)
    w2i, w2o = ffn2_w_in[0].astype(BF16), ffn2_w_out[0].astype(BF16)
    wi = w_in[0]
    c_gad = 2 * GLA_QK + 2 * GLA_V
    c_sq = c_gad + GLA_RANK
    c_ga = c_sq + SWA_Q + 2 * SWA_KV
    w_a = wi[:, c_ga:].astype(BF16)
    w_b = wi[:, :c_gad].astype(BF16)
    w_c = wi[:, c_sq:c_ga].astype(BF16)
    w_gad = wi[:, c_gad:c_sq].astype(BF16)
    wgo, wso, wout = w_gla_o[0].astype(BF16), w_swa_o[0].astype(BF16), w_out[0].astype(BF16)
    g1, gm, g2 = ffn1_norm, mix_norm, ffn2_norm
    gf = final_norm.reshape(1, D_MODEL)
    a_up, a_bias = gla_a_up[0], gla_a_bias
    onorm, gbias = gla_out_norm, gate_bias
    sinks = swa_sinks[0]

    xs = jnp.concatenate([x_sample.reshape(DB, D_MODEL), meta_tokens.astype(F32)], axis=0)
    hs1 = _ffn(xs, g1, w1i, w1o, gf, tm=n_small, tf=512, final_norm=False, name="ffn1_small")
    us, gads = _inproj(hs1, gm, w_a, w_b, w_c, w_gad, tm=n_small, tn=512, name="inproj_small")
    meta_blk = DB // N_META
    _, st_meta = _gla(us, gads, a_up, a_bias, jnp.zeros((GLA_HEADS, GLA_DK, GLA_DV), F32),
                      batch=1, seq=N_META, C=N_META, row_off=meta_blk, name="gla_meta")
    st_meta = st_meta[0]

    xp = x_prompt.reshape(B * S, D_MODEL)
    hp1 = _ffn(xp, g1, w1i, w1o, gf, tm=1024, tf=512, final_norm=False, name="ffn1")
    up, gadp = _inproj(hp1, gm, w_a, w_b, w_c, w_gad, tm=2048, tn=512, name="inproj")
    o_gla, st_fin = _gla(up, gadp, a_up, a_bias, st_meta, batch=B, seq=S, C=128, row_off=0, name="gla")
    o_swa = _swa(sinks, up, us, batch=B, seq=S, meta_row_block=meta_blk, name="swa")
    hp2 = _merge(hp1, o_gla, up, o_swa, onorm, gbias, wgo, wso, wout, rows=B * S, tm=256, name="merge")
    y_prompt = _ffn(hp2, g2, w2i, w2o, gf, tm=1024, tf=512, final_norm=True, name="ffn2")

    og_s, st_s = _gla_step(us.reshape(n_small, 1, N_MAIN), gads.reshape(n_small, 1, GLA_RANK),
                           a_up, a_bias, state_gla[0], batch=DB, bb=4, name="gla_step")
    sk_s = us[:DB, COL_SK:COL_SK + SWA_KV]
    sv_s = us[:DB, COL_SV:COL_SV + SWA_KV]
    WB = cache_win_k.shape[2]
    new_k = jnp.concatenate([cache_win_k[0].reshape(DB, WB, SWA_KV)[:, 1:], sk_s[:, None, :]], axis=1)
    new_v = jnp.concatenate([cache_win_v[0].reshape(DB, WB, SWA_KV)[:, 1:], sv_s[:, None, :]], axis=1)
    slopes = jnp.asarray([_alibi_slope(h) for h in range(SWA_HEADS)], F32).reshape(SWA_HEADS, 1)
    os_s = _swa_step(us[:DB, COL_SQ:COL_SQ + SWA_Q].reshape(DB, SWA_HEADS, SWA_HD), new_k, new_v,
                     cache_meta_k[0].reshape(DB, N_META, SWA_KV), cache_meta_v[0].reshape(DB, N_META, SWA_KV),
                     sinks.reshape(SWA_HEADS, 1), slopes, batch=DB, bb=8, name="swa_step")
    hs2 = _merge(hs1, og_s.reshape(DB, GLA_V), us, os_s.reshape(DB, SWA_Q), onorm, gbias, wgo, wso, wout,
                 rows=DB, tm=DB, name="merge_small")
    y_sample = _ffn(hs2, g2, w2i, w2o, gf, tm=DB, tf=512, final_norm=True, name="ffn2_small")

    up3 = up.reshape(B, S, N_MAIN)
    kv_shape = (SWA_KV_HEADS, SWA_HD)
    p_meta_k = jnp.broadcast_to(us[DB:, COL_SK:COL_SK + SWA_KV].reshape(1, 1, N_META, *kv_shape),
                                (1, B, N_META, *kv_shape))
    p_meta_v = jnp.broadcast_to(us[DB:, COL_SV:COL_SV + SWA_KV].reshape(1, 1, N_META, *kv_shape),
                                (1, B, N_META, *kv_shape))
    p_win_k = up3[:, S - WINDOW:, COL_SK:COL_SK + SWA_KV].reshape(1, B, WINDOW, *kv_shape)
    p_win_v = up3[:, S - WINDOW:, COL_SV:COL_SV + SWA_KV].reshape(1, B, WINDOW, *kv_shape)
    return (y_prompt.reshape(B, S, D_MODEL), y_sample.reshape(DB, 1, D_MODEL),
            p_meta_k, p_meta_v, p_win_k, p_win_v, st_fin[None],
            new_k.reshape(1, DB, WB, *kv_shape), new_v.reshape(1, DB, WB, *kv_shape), st_s[None])
```

```python
import functools

import jax
import jax.numpy as jnp
from jax import lax
from jax.experimental import pallas as pl
from jax.experimental.pallas import tpu as pltpu

F32 = jnp.float32
BF16 = jnp.bfloat16

D_MODEL = 2048
N_META = 16
GLA_HEADS = 4
GLA_DK = 128
GLA_DV = 256
GLA_RANK = 16
GLA_TAU = 16.0
SWA_HEADS = 16
SWA_KV_HEADS = 4
SWA_GROUP = SWA_HEADS // SWA_KV_HEADS
SWA_HD = 64
WINDOW = 128
D_FF = 5632
EPS = 1e-6

GLA_QK = GLA_HEADS * GLA_DK
GLA_V = GLA_HEADS * GLA_DV
SWA_Q = SWA_HEADS * SWA_HD
SWA_KV = SWA_KV_HEADS * SWA_HD

COL_GA = 0
COL_GB = COL_GA + D_MODEL
COL_GQ = COL_GB + D_MODEL
COL_GK = COL_GQ + GLA_QK
COL_GV = COL_GK + GLA_QK
COL_GR = COL_GV + GLA_V
COL_SQ = COL_GR + GLA_V
COL_SK = COL_SQ + SWA_Q
COL_SV = COL_SK + SWA_KV
N_MAIN = COL_SV + SWA_KV
N_GRP_A = COL_GQ
N_GRP_B = COL_SQ - COL_GQ
N_GRP_C = N_MAIN - COL_SQ

GLA_SUB = 8
MASK_NEG = -1e30
MIB = 1024 * 1024
LANE = 128
ROW_CHUNK = 256
COL_CHUNK = 512

SWA_QB = 64
SWA_SPAN = WINDOW + SWA_QB
SWA_KT = 256
SWA_SINK_COL = SWA_SPAN + N_META


def _rms_scale(x):
    return x * lax.rsqrt(jnp.mean(x * x, axis=-1, keepdims=True) + EPS)


def _log_sigmoid(x):
    return jnp.minimum(x, 0.0) - jnp.log(1.0 + jnp.exp(-jnp.abs(x)))


def _dot(a, b):
    return jnp.dot(a, b, preferred_element_type=F32)


def _dot_nt(a, b):
    return lax.dot_general(a, b, (((1,), (1,)), ((), ())), preferred_element_type=F32)


def _row_to_col(row, n):
    r = lax.broadcasted_iota(jnp.int32, (n, n), 0)
    c = lax.broadcasted_iota(jnp.int32, (n, n), 1)
    return jnp.sum(jnp.where(r == c, jnp.broadcast_to(row, (n, n)), 0.0), axis=1, keepdims=True)


def _for_row_chunks(rows, fn):
    chunk = ROW_CHUNK if rows % ROW_CHUNK == 0 else rows

    def body(i, carry):
        fn(pl.ds(pl.multiple_of(i * chunk, chunk), chunk))
        return carry

    lax.fori_loop(0, rows // chunk, body, 0)


def _ffn_rows(j, n_ff, x_ref, o_ref, xn_ref, g_ref, fg_ref, wa, wb, wo, final_norm):
    rows = x_ref.shape[0]

    @pl.when(j == 0)
    def _():
        def norm_rows(sl):
            x = x_ref[sl, :]
            xn_ref[sl, :] = (_rms_scale(x) * g_ref[...]).astype(BF16)
            o_ref[sl, :] = x

        _for_row_chunks(rows, norm_rows)

    xn = xn_ref[...]
    a = _dot(xn, wa)
    b = _dot(xn, wb)
    h = ((0.5 * a) * jax.nn.sigmoid(a) * b).astype(BF16)
    for c in range(D_MODEL // COL_CHUNK):
        cs = slice(COL_CHUNK * c, COL_CHUNK * (c + 1))
        o_ref[:, cs] += _dot(h, wo[:, cs])

    if final_norm:
        @pl.when(j == n_ff - 1)
        def _():
            def final_rows(sl):
                o_ref[sl, :] = _rms_scale(o_ref[sl, :]) * fg_ref[...]

            _for_row_chunks(rows, final_rows)


def _ffn_first_kernel(x_ref, xs_ref, g_ref, wa_ref, wb_ref, wo_ref, fg_ref,
                      o_ref, os_ref, wa16_ref, wb16_ref, wo16_ref, xn_ref, xns_ref, *, n_ff, final_norm):
    j = pl.program_id(0)
    wa = wa_ref[...].astype(BF16)
    wb = wb_ref[...].astype(BF16)
    wo = wo_ref[...].astype(BF16)
    wa16_ref[...] = wa
    wb16_ref[...] = wb
    wo16_ref[...] = wo
    _ffn_rows(j, n_ff, x_ref, o_ref, xn_ref, g_ref, fg_ref, wa, wb, wo, final_norm)
    _ffn_rows(j, n_ff, xs_ref, os_ref, xns_ref, g_ref, fg_ref, wa, wb, wo, final_norm)


def _ffn_rest_kernel(x_ref, g_ref, wa_ref, wb_ref, wo_ref, fg_ref, o_first_ref, o_ref, xn_ref,
                     *, n_ff, final_norm):
    del o_first_ref
    _ffn_rows(pl.program_id(1), n_ff, x_ref, o_ref, xn_ref, g_ref, fg_ref,
              wa_ref[...], wb_ref[...], wo_ref[...], final_norm)


def _ffn(x, xs, gain, w_in, w_out, final_gain, *, tm, tf_first, tf, final_norm, name):
    rows, rows_s = x.shape[0], xs.shape[0]
    n1 = D_FF // tf_first
    o, os_, wa16, wb16, wo16 = pl.pallas_call(
        functools.partial(_ffn_first_kernel, n_ff=n1, final_norm=final_norm),
        grid=(n1,),
        in_specs=[
            pl.BlockSpec((tm, D_MODEL), lambda j: (0, 0), pipeline_mode=pl.Buffered(1)),
            pl.BlockSpec((rows_s, D_MODEL), lambda j: (0, 0), pipeline_mode=pl.Buffered(1)),
            pl.BlockSpec((1, D_MODEL), lambda j: (0, 0)),
            pl.BlockSpec((D_MODEL, tf_first), lambda j: (0, j)),
            pl.BlockSpec((D_MODEL, tf_first), lambda j: (0, j + n1)),
            pl.BlockSpec((tf_first, D_MODEL), lambda j: (j, 0)),
            pl.BlockSpec((1, D_MODEL), lambda j: (0, 0)),
        ],
        out_specs=[
            pl.BlockSpec((tm, D_MODEL), lambda j: (0, 0)),
            pl.BlockSpec((rows_s, D_MODEL), lambda j: (0, 0)),
            pl.BlockSpec((D_MODEL, tf_first), lambda j: (0, j)),
            pl.BlockSpec((D_MODEL, tf_first), lambda j: (0, j)),
            pl.BlockSpec((tf_first, D_MODEL), lambda j: (j, 0)),
        ],
        out_shape=[
            jax.ShapeDtypeStruct((rows, D_MODEL), F32),
            jax.ShapeDtypeStruct((rows_s, D_MODEL), F32),
            jax.ShapeDtypeStruct((D_MODEL, D_FF), BF16),
            jax.ShapeDtypeStruct((D_MODEL, D_FF), BF16),
            jax.ShapeDtypeStruct((D_FF, D_MODEL), BF16),
        ],
        scratch_shapes=[pltpu.VMEM((tm, D_MODEL), BF16), pltpu.VMEM((rows_s, D_MODEL), BF16)],
        compiler_params=pltpu.CompilerParams(
            dimension_semantics=("arbitrary",), vmem_limit_bytes=58 * MIB),
        name=name + "_first",
    )(x, xs, gain, w_in, w_in, w_out, final_gain)

    n2 = D_FF // tf
    o = pl.pallas_call(
        functools.partial(_ffn_rest_kernel, n_ff=n2, final_norm=final_norm),
        grid=(rows // tm - 1, n2),
        in_specs=[
            pl.BlockSpec((tm, D_MODEL), lambda i, j: (i + 1, 0), pipeline_mode=pl.Buffered(1)),
            pl.BlockSpec((1, D_MODEL), lambda i, j: (0, 0)),
            pl.BlockSpec((D_MODEL, tf), lambda i, j: (0, j)),
            pl.BlockSpec((D_MODEL, tf), lambda i, j: (0, j)),
            pl.BlockSpec((tf, D_MODEL), lambda i, j: (j, 0)),
            pl.BlockSpec((1, D_MODEL), lambda i, j: (0, 0)),
            pl.BlockSpec(memory_space=pl.ANY),
        ],
        out_specs=pl.BlockSpec((tm, D_MODEL), lambda i, j: (i + 1, 0)),
        out_shape=jax.ShapeDtypeStruct((rows, D_MODEL), F32),
        input_output_aliases={6: 0},
        scratch_shapes=[pltpu.VMEM((tm, D_MODEL), BF16)],
        compiler_params=pltpu.CompilerParams(
            dimension_semantics=("parallel", "arbitrary"), vmem_limit_bytes=58 * MIB),
        name=name + "_rest",
    )(x, gain, wa16, wb16, wo16, final_gain, o)
    return o, os_


def _inproj_rows(j, x_ref, xn_ref, g_ref, w, wg, u_ref, gad_ref):
    @pl.when(j == 0)
    def _():
        def norm_rows(sl):
            xn_ref[sl, :] = (_rms_scale(x_ref[sl, :]) * g_ref[...]).astype(BF16)

        _for_row_chunks(x_ref.shape[0], norm_rows)
        gad_ref[...] = _dot(xn_ref[...], wg)

    u_ref[...] = _dot(xn_ref[...], w)


def _inproj_first_kernel(x_ref, xs_ref, g_ref, wm_ref, wx_ref, wg_ref,
                         u_ref, gad_ref, us_ref, gads_ref, w16_ref, wg16_ref,
                         xn_ref, xns_ref, *, na, nb, tn):
    j = pl.program_id(0)
    shifted = (j < na) | (j >= na + nb)

    @pl.when(shifted)
    def _():
        wide = jnp.concatenate([wm_ref[0], wx_ref[0]], axis=1)
        w16_ref[...] = wide[:, GLA_RANK:GLA_RANK + tn].astype(BF16)

    @pl.when(jnp.logical_not(shifted))
    def _():
        w16_ref[...] = wm_ref[0].astype(BF16)

    w = w16_ref[...]
    wg = wg_ref[0][:, :GLA_RANK].astype(BF16)

    @pl.when(j == 0)
    def _():
        wg16_ref[...] = wg

    _inproj_rows(j, x_ref, xn_ref, g_ref, w, wg, u_ref, gad_ref)
    _inproj_rows(j, xs_ref, xns_ref, g_ref, w, wg, us_ref, gads_ref)


def _inproj_rest_kernel(x_ref, g_ref, w_ref, wg_ref, u_first_ref, gad_first_ref, u_ref, gad_ref, xn_ref):
    del u_first_ref, gad_first_ref
    _inproj_rows(pl.program_id(1), x_ref, xn_ref, g_ref, w_ref[...], wg_ref[...], u_ref, gad_ref)


def _inproj(x, xs, gain, w_in, *, tm, tn_first, tn, name):
    rows, rows_s = x.shape[0], xs.shape[0]
    na, nb, ncc = N_GRP_A // tn_first, N_GRP_B // tn_first, N_GRP_C // tn_first
    nj = na + nb + ncc
    c_gad = N_GRP_B
    c_sq = c_gad + GLA_RANK
    c_ga = c_sq + N_GRP_C
    assert c_gad % tn_first == 0 and (c_ga - GLA_RANK) % tn_first == 0
    assert tn_first % LANE == 0 and c_gad % LANE == 0

    def window(j):
        return jnp.where(j < na, (c_ga - GLA_RANK) // tn_first + j,
                         jnp.where(j < na + nb, j - na, c_gad // tn_first + j - na - nb))

    u, gad, us, gads, w16, wg16 = pl.pallas_call(
        functools.partial(_inproj_first_kernel, na=na, nb=nb, tn=tn_first),
        grid=(nj,),
        in_specs=[
            pl.BlockSpec((tm, D_MODEL), lambda j: (0, 0), pipeline_mode=pl.Buffered(1)),
            pl.BlockSpec((rows_s, D_MODEL), lambda j: (0, 0), pipeline_mode=pl.Buffered(1)),
            pl.BlockSpec((1, D_MODEL), lambda j: (0, 0)),
            pl.BlockSpec((1, D_MODEL, tn_first), lambda j: (0, 0, window(j))),
            pl.BlockSpec((1, D_MODEL, LANE), lambda j: (0, 0, (window(j) + 1) * (tn_first // LANE))),
            pl.BlockSpec((1, D_MODEL, LANE), lambda j: (0, 0, c_gad // LANE)),
        ],
        out_specs=[
            pl.BlockSpec((tm, tn_first), lambda j: (0, j)),
            pl.BlockSpec((tm, GLA_RANK), lambda j: (0, 0)),
            pl.BlockSpec((rows_s, tn_first), lambda j: (0, j)),
            pl.BlockSpec((rows_s, GLA_RANK), lambda j: (0, 0)),
            pl.BlockSpec((D_MODEL, tn_first), lambda j: (0, j)),
            pl.BlockSpec((D_MODEL, GLA_RANK), lambda j: (0, 0)),
        ],
        out_shape=[
            jax.ShapeDtypeStruct((rows, N_MAIN), F32),
            jax.ShapeDtypeStruct((rows, GLA_RANK), F32),
            jax.ShapeDtypeStruct((rows_s, N_MAIN), F32),
            jax.ShapeDtypeStruct((rows_s, GLA_RANK), F32),
            jax.ShapeDtypeStruct((D_MODEL, N_MAIN), BF16),
            jax.ShapeDtypeStruct((D_MODEL, GLA_RANK), BF16),
        ],
        scratch_shapes=[pltpu.VMEM((tm, D_MODEL), BF16), pltpu.VMEM((rows_s, D_MODEL), BF16)],
        compiler_params=pltpu.CompilerParams(
            dimension_semantics=("arbitrary",), vmem_limit_bytes=56 * MIB),
        name=name + "_first",
    )(x, xs, gain, w_in, w_in, w_in)

    u, gad = pl.pallas_call(
        _inproj_rest_kernel,
        grid=(rows // tm - 1, N_MAIN // tn),
        in_specs=[
            pl.BlockSpec((tm, D_MODEL), lambda i, j: (i + 1, 0), pipeline_mode=pl.Buffered(1)),
            pl.BlockSpec((1, D_MODEL), lambda i, j: (0, 0)),
            pl.BlockSpec((D_MODEL, tn), lambda i, j: (0, j)),
            pl.BlockSpec((D_MODEL, GLA_RANK), lambda i, j: (0, 0)),
            pl.BlockSpec(memory_space=pl.ANY),
            pl.BlockSpec(memory_space=pl.ANY),
        ],
        out_specs=[
            pl.BlockSpec((tm, tn), lambda i, j: (i + 1, j)),
            pl.BlockSpec((tm, GLA_RANK), lambda i, j: (i + 1, 0)),
        ],
        out_shape=[
            jax.ShapeDtypeStruct((rows, N_MAIN), F32),
            jax.ShapeDtypeStruct((rows, GLA_RANK), F32),
        ],
        input_output_aliases={4: 0, 5: 1},
        scratch_shapes=[pltpu.VMEM((tm, D_MODEL), BF16)],
        compiler_params=pltpu.CompilerParams(
            dimension_semantics=("parallel", "arbitrary"), vmem_limit_bytes=56 * MIB),
        name=name + "_rest",
    )(x, gain, w16, wg16, u, gad)
    return u, gad, us, gads


def _gla_head(q, k, v, b, state, *, C):
    o = _dot((q * jnp.exp(b)).astype(BF16), state.astype(BF16))

    ri = lax.broadcasted_iota(jnp.int32, (C, C), 0)
    ci = lax.broadcasted_iota(jnp.int32, (C, C), 1)
    rowd = lax.broadcasted_iota(jnp.int32, (C, GLA_DK), 0)
    att = None
    m = C // 2
    while m >= GLA_SUB:
        nblk = C // (2 * m)
        pieces = [jnp.broadcast_to(b[i * 2 * m + m - 1:i * 2 * m + m, :], (2 * m, GLA_DK))
                  for i in range(nblk)]
        bref = pieces[0] if nblk == 1 else jnp.concatenate(pieces, axis=0)
        e = jnp.exp(-jnp.abs(b - bref))
        second = (rowd & m) != 0
        ql = jnp.where(second, q * e, 0.0).astype(BF16)
        kl = jnp.where(second, 0.0, k * e).astype(BF16)
        a = _dot_nt(ql, kl)
        if nblk > 1:
            a = jnp.where((ri ^ ci) < 2 * m, a, 0.0)
        att = a if att is None else att + a
        m //= 2
    if att is not None:
        o = o + _dot(att.astype(BF16), v.astype(BF16))

    r16 = lax.broadcasted_iota(jnp.int32, (GLA_SUB, GLA_DK), 0)
    outs = []
    for i in range(C // GLA_SUB):
        sl = slice(GLA_SUB * i, GLA_SUB * (i + 1))
        bb, qq, kk, vv = b[sl], q[sl], k[sl], v[sl]
        od = o[sl]
        for s in range(GLA_SUB):
            d = jnp.where(r16 >= s, bb - bb[s:s + 1], MASK_NEG)
            w = jnp.sum(qq * jnp.exp(d) * kk[s:s + 1], axis=-1, keepdims=True)
            od = od + w * vv[s:s + 1]
        outs.append(od)
    o = outs[0] if len(outs) == 1 else jnp.concatenate(outs, axis=0)

    bl = b[C - 1:C, :]
    kd = k * jnp.exp(bl - b)
    vp = v
    if C < GLA_DK:
        kd = jnp.concatenate([kd, jnp.zeros((GLA_DK - C, GLA_DK), F32)], axis=0)
        vp = jnp.concatenate([v, jnp.zeros((GLA_DK - C, GLA_DV), F32)], axis=0)
    new_state = state * _row_to_col(jnp.exp(bl), GLA_DK) + _dot(kd.T.astype(BF16), vp.astype(BF16))
    return o, new_state


def _gla_kernel(q_ref, k_ref, v_ref, gad_ref, aup_ref, ab_ref, s0_ref, o_ref, sfin_ref, s_ref, *, C, nc):
    n = pl.program_id(1)

    @pl.when(n == 0)
    def _():
        s_ref[...] = s0_ref[...]

    logits = _dot(gad_ref[...].astype(BF16), aup_ref[...].astype(BF16)) + ab_ref[...]
    g = _log_sigmoid(logits) / GLA_TAU

    ri = lax.broadcasted_iota(jnp.int32, (C, C), 0)
    ci = lax.broadcasted_iota(jnp.int32, (C, C), 1)
    tri = jnp.where(ri >= ci, 1.0, 0.0).astype(BF16)
    g_hi = g.astype(BF16)
    r1 = g - g_hi.astype(F32)
    g_mid = r1.astype(BF16)
    g_lo = (r1 - g_mid.astype(F32)).astype(BF16)
    b_all = _dot(tri, g_hi) + _dot(tri, g_mid) + _dot(tri, g_lo)

    for h in range(GLA_HEADS):
        ks = slice(GLA_DK * h, GLA_DK * (h + 1))
        vs = slice(GLA_DV * h, GLA_DV * (h + 1))
        o, new_state = _gla_head(q_ref[:, ks] * (GLA_DK ** -0.5), k_ref[:, ks], v_ref[:, vs],
                                 b_all[:, ks], s_ref[h], C=C)
        o_ref[:, vs] = o
        s_ref[h] = new_state

    @pl.when(n == nc - 1)
    def _():
        sfin_ref[0] = s_ref[...]


def _gla(u, gad, a_up, a_bias, s0, *, batch, seq, C, row_off, name):
    nc = seq // C
    return pl.pallas_call(
        functools.partial(_gla_kernel, C=C, nc=nc),
        grid=(batch, nc),
        in_specs=[
            pl.BlockSpec((C, GLA_QK), lambda b, n: (row_off + b * nc + n, COL_GQ // GLA_QK)),
            pl.BlockSpec((C, GLA_QK), lambda b, n: (row_off + b * nc + n, COL_GK // GLA_QK)),
            pl.BlockSpec((C, GLA_V), lambda b, n: (row_off + b * nc + n, COL_GV // GLA_V)),
            pl.BlockSpec((C, GLA_RANK), lambda b, n: (row_off + b * nc + n, 0)),
            pl.BlockSpec((GLA_RANK, GLA_QK), lambda b, n: (0, 0)),
            pl.BlockSpec((1, GLA_QK), lambda b, n: (0, 0)),
            pl.BlockSpec((GLA_HEADS, GLA_DK, GLA_DV), lambda b, n: (0, 0, 0)),
        ],
        out_specs=[
            pl.BlockSpec((C, GLA_V), lambda b, n: (b * nc + n, 0)),
            pl.BlockSpec((1, GLA_HEADS, GLA_DK, GLA_DV), lambda b, n: (b, 0, 0, 0)),
        ],
        out_shape=[
            jax.ShapeDtypeStruct((batch * seq, GLA_V), F32),
            jax.ShapeDtypeStruct((batch, GLA_HEADS, GLA_DK, GLA_DV), F32),
        ],
        scratch_shapes=[pltpu.VMEM((GLA_HEADS, GLA_DK, GLA_DV), F32)],
        compiler_params=pltpu.CompilerParams(
            dimension_semantics=("parallel", "arbitrary"), vmem_limit_bytes=32 * MIB),
        name=name,
    )(u, u, u, gad, a_up, a_bias, s0)


def _gla_step_kernel(q_ref, k_ref, v_ref, gad_ref, aup_ref, ab_ref, s_ref, o_ref, sn_ref, *, bb):
    gl = jnp.concatenate([gad_ref[i] for i in range(bb)] + [jnp.zeros((8 - bb, GLA_RANK), F32)], axis=0)
    logits = _dot(gl.astype(BF16), aup_ref[...].astype(BF16)) + ab_ref[...]
    dec_all = jnp.exp(_log_sigmoid(logits) / GLA_TAU)
    for i in range(bb):
        q = q_ref[i] * (GLA_DK ** -0.5)
        k = k_ref[i]
        v = v_ref[i]
        dec = dec_all[i:i + 1]
        outs = []
        for h in range(GLA_HEADS):
            sl = slice(GLA_DK * h, GLA_DK * (h + 1))
            new_state = (s_ref[i, h] * _row_to_col(dec[:, sl], GLA_DK)
                         + _row_to_col(k[:, sl], GLA_DK) * v[:, GLA_DV * h:GLA_DV * (h + 1)])
            sn_ref[i, h] = new_state
            outs.append(jnp.sum(_row_to_col(q[:, sl], GLA_DK) * new_state, axis=0, keepdims=True))
        o_ref[i] = jnp.concatenate(outs, axis=1)


def _gla_step(u3, gad3, a_up, a_bias, state, *, batch, bb, name):
    return pl.pallas_call(
        functools.partial(_gla_step_kernel, bb=bb),
        grid=(batch // bb,),
        in_specs=[
            pl.BlockSpec((bb, 1, GLA_QK), lambda b: (b, 0, COL_GQ // GLA_QK)),
            pl.BlockSpec((bb, 1, GLA_QK), lambda b: (b, 0, COL_GK // GLA_QK)),
            pl.BlockSpec((bb, 1, GLA_V), lambda b: (b, 0, COL_GV // GLA_V)),
            pl.BlockSpec((bb, 1, GLA_RANK), lambda b: (b, 0, 0)),
            pl.BlockSpec((GLA_RANK, GLA_QK), lambda b: (0, 0)),
            pl.BlockSpec((1, GLA_QK), lambda b: (0, 0)),
            pl.BlockSpec((bb, GLA_HEADS, GLA_DK, GLA_DV), lambda b: (b, 0, 0, 0)),
        ],
        out_specs=[
            pl.BlockSpec((bb, 1, GLA_V), lambda b: (b, 0, 0)),
            pl.BlockSpec((bb, GLA_HEADS, GLA_DK, GLA_DV), lambda b: (b, 0, 0, 0)),
        ],
        out_shape=[
            jax.ShapeDtypeStruct((batch, 1, GLA_V), F32),
            jax.ShapeDtypeStruct((batch, GLA_HEADS, GLA_DK, GLA_DV), F32),
        ],
        compiler_params=pltpu.CompilerParams(dimension_semantics=("parallel",)),
        name=name,
    )(u3, u3, u3, gad3, a_up, a_bias, state)


def _alibi_slope(head):
    return 2.0 ** (-8.0 * (head + 1) / SWA_HEADS)


def _swa_kernel(sink_ref, q_ref, kc_ref, kp_ref, vc_ref, vp_ref, mk_ref, mv_ref, o_ref, bias_ref):
    n = pl.program_id(1)
    rows = SWA_GROUP * SWA_QB

    @pl.when(n <= 1)
    def _():
        r = lax.broadcasted_iota(jnp.int32, (SWA_QB, SWA_KT), 0)
        c = lax.broadcasted_iota(jnp.int32, (SWA_QB, SWA_KT), 1)
        dist = r + WINDOW - c
        distf = dist.astype(F32)
        band = (dist >= 0) & (dist < WINDOW) & (c < SWA_SPAN)
        for a in range(WINDOW // SWA_QB):
            ok = band & ((c >= WINDOW - SWA_QB * a) | (n > 0))
            for head in range(SWA_HEADS):
                val = jnp.where(ok, -_alibi_slope(head) * distf, MASK_NEG)
                val = jnp.where((c >= SWA_SPAN) & (c < SWA_SINK_COL), 0.0, val)
                val = jnp.where(c == SWA_SINK_COL, sink_ref[head], val)
                bias_ref[a, head // SWA_GROUP, pl.ds((head % SWA_GROUP) * SWA_QB, SWA_QB), :] = val

    q = q_ref[...] * (SWA_HD ** -0.5)
    pad = jnp.zeros((SWA_KT - SWA_SPAN - N_META, SWA_HD), BF16)
    for kvh in range(SWA_KV_HEADS):
        cs = slice(SWA_HD * kvh, SWA_HD * (kvh + 1))
        kwin = jnp.concatenate([kp_ref[:, cs], kc_ref[:, cs]], axis=0).astype(BF16)
        vwin = jnp.concatenate([vp_ref[:, cs], vc_ref[:, cs]], axis=0).astype(BF16)
        km = mk_ref[:, cs].astype(BF16)
        vm = mv_ref[:, cs].astype(BF16)
        for a in range(WINDOW // SWA_QB):
            qs = slice(SWA_QB * a, SWA_QB * (a + 1))
            kt = jnp.concatenate([kwin[SWA_QB * a:SWA_QB * a + SWA_SPAN], km, pad], axis=0)
            vt = jnp.concatenate([vwin[SWA_QB * a:SWA_QB * a + SWA_SPAN], vm, pad], axis=0)
            qg = jnp.concatenate(
                [q[qs, SWA_HD * (kvh * SWA_GROUP + grp):SWA_HD * (kvh * SWA_GROUP + grp + 1)]
                 for grp in range(SWA_GROUP)], axis=0).astype(BF16)
            s = _dot_nt(qg, kt) + bias_ref[a, kvh]
            p = jnp.exp(s - jnp.max(s, axis=-1, keepdims=True))
            o = _dot(p.astype(BF16), vt) / jnp.sum(p, axis=-1, keepdims=True)
            for grp in range(SWA_GROUP):
                head = kvh * SWA_GROUP + grp
                o_ref[qs, SWA_HD * head:SWA_HD * (head + 1)] = o[SWA_QB * grp:SWA_QB * (grp + 1)]


def _swa(sinks, u, u_small, *, batch, seq, meta_row_block, name):
    nb = seq // WINDOW
    kcol, vcol = COL_SK // SWA_KV, COL_SV // SWA_KV
    return pl.pallas_call(
        _swa_kernel,
        grid=(batch, nb),
        in_specs=[
            pl.BlockSpec(memory_space=pltpu.SMEM),
            pl.BlockSpec((WINDOW, SWA_Q), lambda b, n: (b * nb + n, COL_SQ // SWA_Q)),
            pl.BlockSpec((WINDOW, SWA_KV), lambda b, n: (b * nb + n, kcol)),
            pl.BlockSpec((WINDOW, SWA_KV), lambda b, n: (b * nb + jnp.maximum(n - 1, 0), kcol)),
            pl.BlockSpec((WINDOW, SWA_KV), lambda b, n: (b * nb + n, vcol)),
            pl.BlockSpec((WINDOW, SWA_KV), lambda b, n: (b * nb + jnp.maximum(n - 1, 0), vcol)),
            pl.BlockSpec((N_META, SWA_KV), lambda b, n: (meta_row_block, kcol)),
            pl.BlockSpec((N_META, SWA_KV), lambda b, n: (meta_row_block, vcol)),
        ],
        out_specs=pl.BlockSpec((WINDOW, SWA_Q), lambda b, n: (b * nb + n, 0)),
        out_shape=jax.ShapeDtypeStruct((batch * seq, SWA_Q), F32),
        scratch_shapes=[pltpu.VMEM((WINDOW // SWA_QB, SWA_KV_HEADS, SWA_GROUP * SWA_QB, SWA_KT), F32)],
        compiler_params=pltpu.CompilerParams(
            dimension_semantics=("arbitrary", "arbitrary"), vmem_limit_bytes=32 * MIB),
        name=name,
    )(sinks, u, u, u, u, u, u_small, u_small)


def _swa_step_kernel(q_ref, kw_ref, vw_ref, mk_ref, mv_ref, sink_ref, slope_ref, o_ref, *, bb):
    lane = lax.broadcasted_iota(jnp.int32, (1, WINDOW), 1)
    distf = (WINDOW - 1 - lane).astype(F32)
    sink = sink_ref[...]
    slope = slope_ref[...]
    head_row = lax.broadcasted_iota(jnp.int32, (SWA_HEADS, SWA_HD), 0)
    for i in range(bb):
        q = (q_ref[i] * (SWA_HD ** -0.5)).astype(BF16)
        acc = jnp.zeros((SWA_HEADS, SWA_HD), F32)
        for kvh in range(SWA_KV_HEADS):
            cs = slice(SWA_HD * kvh, SWA_HD * (kvh + 1))
            kw = kw_ref[i, :, cs].astype(BF16)
            vw = vw_ref[i, :, cs].astype(BF16)
            km = mk_ref[i, :, cs].astype(BF16)
            vm = mv_ref[i, :, cs].astype(BF16)
            s = _dot_nt(q, kw) - slope * distf
            sm = _dot_nt(q, km)
            mx = jnp.maximum(jnp.maximum(jnp.max(s, axis=-1, keepdims=True),
                                         jnp.max(sm, axis=-1, keepdims=True)), sink)
            p = jnp.exp(s - mx)
            pm = jnp.exp(sm - mx)
            den = (jnp.sum(p, axis=-1, keepdims=True) + jnp.sum(pm, axis=-1, keepdims=True)
                   + jnp.exp(sink - mx))
            oh = (_dot(p.astype(BF16), vw) + _dot(pm.astype(BF16), vm)) / den
            in_group = (head_row >= kvh * SWA_GROUP) & (head_row < (kvh + 1) * SWA_GROUP)
            acc = jnp.where(in_group, oh, acc)
        o_ref[i] = acc


def _swa_step(q3, kw, vw, mk, mv, sinks_col, slopes_col, *, batch, bb, name):
    return pl.pallas_call(
        functools.partial(_swa_step_kernel, bb=bb),
        grid=(batch // bb,),
        in_specs=[
            pl.BlockSpec((bb, SWA_HEADS, SWA_HD), lambda b: (b, 0, 0)),
            pl.BlockSpec((bb, WINDOW, SWA_KV), lambda b: (b, 0, 0)),
            pl.BlockSpec((bb, WINDOW, SWA_KV), lambda b: (b, 0, 0)),
            pl.BlockSpec((bb, N_META, SWA_KV), lambda b: (b, 0, 0)),
            pl.BlockSpec((bb, N_META, SWA_KV), lambda b: (b, 0, 0)),
            pl.BlockSpec((SWA_HEADS, 1), lambda b: (0, 0)),
            pl.BlockSpec((SWA_HEADS, 1), lambda b: (0, 0)),
        ],
        out_specs=pl.BlockSpec((bb, SWA_HEADS, SWA_HD), lambda b: (b, 0, 0)),
        out_shape=jax.ShapeDtypeStruct((batch, SWA_HEADS, SWA_HD), F32),
        compiler_params=pltpu.CompilerParams(dimension_semantics=("parallel",)),
        name=name,
    )(q3, kw, vw, mk, mv, sinks_col, slopes_col)


def _merge_kernel(h_ref, og_ref, gr_ref, os_ref, ga_ref, gb_ref,
                  hs_ref, ogs_ref, grs_ref, oss_ref, gas_ref, gbs_ref,
                  onorm_ref, gbias_ref, wg_ref, ws_ref, wo_ref, out_ref, outs_ref):
    _merge_rows(h_ref, og_ref, gr_ref, os_ref, ga_ref, gb_ref, onorm_ref, gbias_ref,
                wg_ref, ws_ref, wo_ref, out_ref)

    @pl.when(pl.program_id(0) == 0)
    def _():
        _merge_rows(hs_ref, ogs_ref, grs_ref, oss_ref, gas_ref, gbs_ref, onorm_ref, gbias_ref,
                    wg_ref, ws_ref, wo_ref, outs_ref)


def _merge_rows(h_ref, og_ref, gr_ref, os_ref, ga_ref, gb_ref, onorm_ref, gbias_ref,
                wg_ref, ws_ref, wo_ref, out_ref):
    og = og_ref[...]
    parts = []
    for h in range(GLA_HEADS):
        oh = og[:, GLA_DV * h:GLA_DV * (h + 1)]
        parts.append(oh * lax.rsqrt(jnp.mean(oh * oh, axis=-1, keepdims=True) + EPS))
    gr = gr_ref[...]
    of = jnp.concatenate(parts, axis=1) * onorm_ref[...] * (gr * jax.nn.sigmoid(gr))
    gla_b = _dot(of.astype(BF16), wg_ref[...])
    swa_b = _dot(os_ref[...].astype(BF16), ws_ref[...])
    gbias = gbias_ref[...]
    mix = (jax.nn.sigmoid(ga_ref[...] + gbias[:, :D_MODEL]) * gla_b
           + jax.nn.sigmoid(gb_ref[...] + gbias[:, D_MODEL:]) * swa_b)
    out_ref[...] = h_ref[...] + _dot(mix.astype(BF16), wo_ref[...])


def _merge(h, o_gla, u, o_swa, hs, o_gla_s, us, o_swa_s, onorm, gbias, w_gla_o, w_swa_o, w_out,
           *, tm, name):
    rows, rs = h.shape[0], o_gla_s.shape[0]
    const = dict(pipeline_mode=pl.Buffered(1))

    def row_specs(r, idx):
        return [
            pl.BlockSpec((r, D_MODEL), lambda i: (idx(i), 0)),
            pl.BlockSpec((r, GLA_V), lambda i: (idx(i), 0)),
            pl.BlockSpec((r, GLA_V), lambda i: (idx(i), COL_GR // GLA_V)),
            pl.BlockSpec((r, SWA_Q), lambda i: (idx(i), 0)),
            pl.BlockSpec((r, D_MODEL), lambda i: (idx(i), COL_GA // D_MODEL)),
            pl.BlockSpec((r, D_MODEL), lambda i: (idx(i), COL_GB // D_MODEL)),
        ]

    return pl.pallas_call(
        _merge_kernel,
        grid=(rows // tm,),
        in_specs=row_specs(tm, lambda i: i) + row_specs(rs, lambda i: 0) + [
            pl.BlockSpec((1, GLA_V), lambda i: (0, 0)),
            pl.BlockSpec((1, 2 * D_MODEL), lambda i: (0, 0)),
            pl.BlockSpec((GLA_V, D_MODEL), lambda i: (0, 0), **const),
            pl.BlockSpec((SWA_Q, D_MODEL), lambda i: (0, 0), **const),
            pl.BlockSpec((D_MODEL, D_MODEL), lambda i: (0, 0), **const),
        ],
        out_specs=[
            pl.BlockSpec((tm, D_MODEL), lambda i: (i, 0)),
            pl.BlockSpec((rs, D_MODEL), lambda i: (0, 0)),
        ],
        out_shape=[
            jax.ShapeDtypeStruct((rows, D_MODEL), F32),
            jax.ShapeDtypeStruct((rs, D_MODEL), F32),
        ],
        compiler_params=pltpu.CompilerParams(
            dimension_semantics=("arbitrary",), vmem_limit_bytes=56 * MIB),
        name=name,
    )(h, o_gla, u, o_swa, u, u, hs, o_gla_s, us, o_swa_s, us, us,
      onorm, gbias, w_gla_o, w_swa_o, w_out)


def kernel(x_prompt, x_sample, cache_meta_k, cache_meta_v, cache_win_k, cache_win_v, state_gla,
           meta_tokens, ffn1_norm, ffn1_w_in, ffn1_w_out, mix_norm, w_in, gla_a_up, gla_a_bias,
           gla_out_norm, w_gla_o, swa_sinks, w_swa_o, gate_bias, w_out, ffn2_norm, ffn2_w_in,
           ffn2_w_out, final_norm):
    B, S, _ = x_prompt.shape
    DB = x_sample.shape[0]
    assert x_sample.shape[1] == 1 and ffn1_norm.shape[0] == 1
    n_small = DB + N_META

    wgo, wso, wout = w_gla_o[0].astype(BF16), w_swa_o[0].astype(BF16), w_out[0].astype(BF16)
    g1, gm, g2 = ffn1_norm, mix_norm, ffn2_norm
    gf = final_norm.reshape(1, D_MODEL)
    a_up, a_bias = gla_a_up[0], gla_a_bias
    onorm, gbias = gla_out_norm, gate_bias
    sinks = swa_sinks[0]

    xs = jnp.concatenate([x_sample.reshape(DB, D_MODEL), meta_tokens.astype(F32)], axis=0)
    xp = x_prompt.reshape(B * S, D_MODEL)
    hp1, hs1 = _ffn(xp, xs, g1, ffn1_w_in[0], ffn1_w_out[0], gf, tm=1024, tf_first=256, tf=512,
                    final_norm=False, name="ffn1")
    up, gadp, us, gads = _inproj(hp1, hs1, gm, w_in, tm=2048, tn_first=256, tn=512, name="inproj")

    meta_blk = DB // N_META
    _, st_meta = _gla(us, gads, a_up, a_bias, jnp.zeros((GLA_HEADS, GLA_DK, GLA_DV), F32),
                      batch=1, seq=N_META, C=N_META, row_off=meta_blk, name="gla_meta")
    st_meta = st_meta[0]
    o_gla, st_fin = _gla(up, gadp, a_up, a_bias, st_meta, batch=B, seq=S, C=128, row_off=0, name="gla")
    o_swa = _swa(sinks, up, us, batch=B, seq=S, meta_row_block=meta_blk, name="swa")

    og_s, st_s = _gla_step(us.reshape(n_small, 1, N_MAIN), gads.reshape(n_small, 1, GLA_RANK),
                           a_up, a_bias, state_gla[0], batch=DB, bb=4, name="gla_step")
    sk_s = us[:DB, COL_SK:COL_SK + SWA_KV]
    sv_s = us[:DB, COL_SV:COL_SV + SWA_KV]
    WB = cache_win_k.shape[2]
    new_k = jnp.concatenate([cache_win_k[0].reshape(DB, WB, SWA_KV)[:, 1:], sk_s[:, None, :]], axis=1)
    new_v = jnp.concatenate([cache_win_v[0].reshape(DB, WB, SWA_KV)[:, 1:], sv_s[:, None, :]], axis=1)
    slopes = jnp.asarray([_alibi_slope(h) for h in range(SWA_HEADS)], F32).reshape(SWA_HEADS, 1)
    os_s = _swa_step(us[:DB, COL_SQ:COL_SQ + SWA_Q].reshape(DB, SWA_HEADS, SWA_HD), new_k, new_v,
                     cache_meta_k[0].reshape(DB, N_META, SWA_KV), cache_meta_v[0].reshape(DB, N_META, SWA_KV),
                     sinks.reshape(SWA_HEADS, 1), slopes, batch=DB, bb=8, name="swa_step")
    hp2, hs2 = _merge(hp1, o_gla, up, o_swa, hs1, og_s.reshape(DB, GLA_V), us, os_s.reshape(DB, SWA_Q),
                      onorm, gbias, wgo, wso, wout, tm=256, name="merge")
    y_prompt, y_sample = _ffn(hp2, hs2, g2, ffn2_w_in[0], ffn2_w_out[0], gf, tm=1024, tf_first=256, tf=512,
                              final_norm=True, name="ffn2")

    up3 = up.reshape(B, S, N_MAIN)
    kv_shape = (SWA_KV_HEADS, SWA_HD)
    p_meta_k = jnp.broadcast_to(us[DB:, COL_SK:COL_SK + SWA_KV].reshape(1, 1, N_META, *kv_shape),
                                (1, B, N_META, *kv_shape))
    p_meta_v = jnp.broadcast_to(us[DB:, COL_SV:COL_SV + SWA_KV].reshape(1, 1, N_META, *kv_shape),
                                (1, B, N_META, *kv_shape))
    p_win_k = up3[:, S - WINDOW:, COL_SK:COL_SK + SWA_KV].reshape(1, B, WINDOW, *kv_shape)
    p_win_v = up3[:, S - WINDOW:, COL_SV:COL_SV + SWA_KV].reshape(1, B, WINDOW, *kv_shape)
    return (y_prompt.reshape(B, S, D_MODEL), y_sample.reshape(DB, 1, D_MODEL),
            p_meta_k, p_meta_v, p_win_k, p_win_v, st_fin[None],
            new_k.reshape(1, DB, WB, *kv_shape), new_v.reshape(1, DB, WB, *kv_shape), st_s[None])
```

```python
import functools

import jax
import jax.numpy as jnp
from jax import lax
from jax.experimental import pallas as pl
from jax.experimental.pallas import tpu as pltpu

F32 = jnp.float32
BF16 = jnp.bfloat16

D_MODEL = 2048
N_META = 16
GLA_HEADS = 4
GLA_DK = 128
GLA_DV = 256
GLA_RANK = 16
GLA_TAU = 16.0
SWA_HEADS = 16
SWA_KV_HEADS = 4
SWA_GROUP = SWA_HEADS // SWA_KV_HEADS
SWA_HD = 64
WINDOW = 128
D_FF = 5632
EPS = 1e-6

GLA_QK = GLA_HEADS * GLA_DK
GLA_V = GLA_HEADS * GLA_DV
SWA_Q = SWA_HEADS * SWA_HD
SWA_KV = SWA_KV_HEADS * SWA_HD

COL_GA = 0
COL_GB = COL_GA + D_MODEL
COL_GQ = COL_GB + D_MODEL
COL_GK = COL_GQ + GLA_QK
COL_GV = COL_GK + GLA_QK
COL_GR = COL_GV + GLA_V
COL_SQ = COL_GR + GLA_V
COL_SK = COL_SQ + SWA_Q
COL_SV = COL_SK + SWA_KV
N_MAIN = COL_SV + SWA_KV
N_GRP_A = COL_GQ
N_GRP_B = COL_SQ - COL_GQ
N_GRP_C = N_MAIN - COL_SQ

GLA_SUB = 8
MASK_NEG = -1e30
MIB = 1024 * 1024
LANE = 128
ROW_CHUNK = 256
COL_CHUNK = 512

SWA_QB = 64
SWA_SPAN = WINDOW + SWA_QB
SWA_KT = 256
SWA_SINK_COL = SWA_SPAN + N_META


def _rms_scale(x):
    return x * lax.rsqrt(jnp.mean(x * x, axis=-1, keepdims=True) + EPS)


def _log_sigmoid(x):
    return jnp.minimum(x, 0.0) - jnp.log(1.0 + jnp.exp(-jnp.abs(x)))


def _dot(a, b):
    return jnp.dot(a, b, preferred_element_type=F32)


def _dot_nt(a, b):
    return lax.dot_general(a, b, (((1,), (1,)), ((), ())), preferred_element_type=F32)


def _row_to_col(row, n):
    r = lax.broadcasted_iota(jnp.int32, (n, n), 0)
    c = lax.broadcasted_iota(jnp.int32, (n, n), 1)
    return jnp.sum(jnp.where(r == c, jnp.broadcast_to(row, (n, n)), 0.0), axis=1, keepdims=True)


def _for_row_chunks(rows, fn):
    chunk = ROW_CHUNK if rows % ROW_CHUNK == 0 else rows

    def body(i, carry):
        fn(pl.ds(pl.multiple_of(i * chunk, chunk), chunk))
        return carry

    lax.fori_loop(0, rows // chunk, body, 0)


def _ffn_rows(j, n_ff, x_ref, o_ref, xn_ref, g_ref, fg_ref, wa, wb, wo, final_norm):
    rows = x_ref.shape[0]

    @pl.when(j == 0)
    def _():
        def norm_rows(sl):
            x = x_ref[sl, :]
            xn_ref[sl, :] = (_rms_scale(x) * g_ref[...]).astype(BF16)
            o_ref[sl, :] = x

        _for_row_chunks(rows, norm_rows)

    xn = xn_ref[...]
    a = _dot(xn, wa)
    b = _dot(xn, wb)
    h = ((0.5 * a) * jax.nn.sigmoid(a) * b).astype(BF16)
    for c in range(D_MODEL // COL_CHUNK):
        cs = slice(COL_CHUNK * c, COL_CHUNK * (c + 1))
        o_ref[:, cs] += _dot(h, wo[:, cs])

    if final_norm:
        @pl.when(j == n_ff - 1)
        def _():
            def final_rows(sl):
                o_ref[sl, :] = _rms_scale(o_ref[sl, :]) * fg_ref[...]

            _for_row_chunks(rows, final_rows)


def _ffn_first_kernel(x_ref, xs_ref, g_ref, wa_ref, wb_ref, wo_ref, fg_ref,
                      o_ref, os_ref, wa16_ref, wb16_ref, wo16_ref, xn_ref, xns_ref, *, n_ff, final_norm):
    j = pl.program_id(0)
    wa = wa_ref[...].astype(BF16)
    wb = wb_ref[...].astype(BF16)
    wo = wo_ref[...].astype(BF16)
    wa16_ref[...] = wa
    wb16_ref[...] = wb
    wo16_ref[...] = wo
    _ffn_rows(j, n_ff, x_ref, o_ref, xn_ref, g_ref, fg_ref, wa, wb, wo, final_norm)
    _ffn_rows(j, n_ff, xs_ref, os_ref, xns_ref, g_ref, fg_ref, wa, wb, wo, final_norm)


def _ffn_rest_kernel(x_ref, g_ref, wa_ref, wb_ref, wo_ref, fg_ref, o_first_ref, o_ref, xn_ref,
                     *, n_ff, final_norm):
    del o_first_ref
    _ffn_rows(pl.program_id(1), n_ff, x_ref, o_ref, xn_ref, g_ref, fg_ref,
              wa_ref[...], wb_ref[...], wo_ref[...], final_norm)


def _ffn(x, xs, gain, w_in, w_out, final_gain, *, tm, tf_first, tf, final_norm, name):
    rows, rows_s = x.shape[0], xs.shape[0]
    n1 = D_FF // tf_first
    o, os_, wa16, wb16, wo16 = pl.pallas_call(
        functools.partial(_ffn_first_kernel, n_ff=n1, final_norm=final_norm),
        grid=(n1,),
        in_specs=[
            pl.BlockSpec((tm, D_MODEL), lambda j: (0, 0), pipeline_mode=pl.Buffered(1)),
            pl.BlockSpec((rows_s, D_MODEL), lambda j: (0, 0), pipeline_mode=pl.Buffered(1)),
            pl.BlockSpec((1, D_MODEL), lambda j: (0, 0)),
            pl.BlockSpec((D_MODEL, tf_first), lambda j: (0, j)),
            pl.BlockSpec((D_MODEL, tf_first), lambda j: (0, j + n1)),
            pl.BlockSpec((tf_first, D_MODEL), lambda j: (j, 0)),
            pl.BlockSpec((1, D_MODEL), lambda j: (0, 0)),
        ],
        out_specs=[
            pl.BlockSpec((tm, D_MODEL), lambda j: (0, 0)),
            pl.BlockSpec((rows_s, D_MODEL), lambda j: (0, 0)),
            pl.BlockSpec((D_MODEL, tf_first), lambda j: (0, j)),
            pl.BlockSpec((D_MODEL, tf_first), lambda j: (0, j)),
            pl.BlockSpec((tf_first, D_MODEL), lambda j: (j, 0)),
        ],
        out_shape=[
            jax.ShapeDtypeStruct((rows, D_MODEL), F32),
            jax.ShapeDtypeStruct((rows_s, D_MODEL), F32),
            jax.ShapeDtypeStruct((D_MODEL, D_FF), BF16),
            jax.ShapeDtypeStruct((D_MODEL, D_FF), BF16),
            jax.ShapeDtypeStruct((D_FF, D_MODEL), BF16),
        ],
        scratch_shapes=[pltpu.VMEM((tm, D_MODEL), BF16), pltpu.VMEM((rows_s, D_MODEL), BF16)],
        compiler_params=pltpu.CompilerParams(
            dimension_semantics=("arbitrary",), vmem_limit_bytes=58 * MIB),
        name=name + "_first",
    )(x, xs, gain, w_in, w_in, w_out, final_gain)

    n2 = D_FF // tf
    o = pl.pallas_call(
        functools.partial(_ffn_rest_kernel, n_ff=n2, final_norm=final_norm),
        grid=(rows // tm - 1, n2),
        in_specs=[
            pl.BlockSpec((tm, D_MODEL), lambda i, j: (i + 1, 0)),
            pl.BlockSpec((1, D_MODEL), lambda i, j: (0, 0)),
            pl.BlockSpec((D_MODEL, tf), lambda i, j: (0, j)),
            pl.BlockSpec((D_MODEL, tf), lambda i, j: (0, j)),
            pl.BlockSpec((tf, D_MODEL), lambda i, j: (j, 0)),
            pl.BlockSpec((1, D_MODEL), lambda i, j: (0, 0)),
            pl.BlockSpec(memory_space=pl.ANY),
        ],
        out_specs=pl.BlockSpec((tm, D_MODEL), lambda i, j: (i + 1, 0)),
        out_shape=jax.ShapeDtypeStruct((rows, D_MODEL), F32),
        input_output_aliases={6: 0},
        scratch_shapes=[pltpu.VMEM((tm, D_MODEL), BF16)],
        compiler_params=pltpu.CompilerParams(
            dimension_semantics=("parallel", "arbitrary"), vmem_limit_bytes=58 * MIB),
        name=name + "_rest",
    )(x, gain, wa16, wb16, wo16, final_gain, o)
    return o, os_


def _inproj_rows(j, x_ref, xn_ref, g_ref, w, wg, u_ref, gad_ref):
    @pl.when(j == 0)
    def _():
        def norm_rows(sl):
            xn_ref[sl, :] = (_rms_scale(x_ref[sl, :]) * g_ref[...]).astype(BF16)

        _for_row_chunks(x_ref.shape[0], norm_rows)
        gad_ref[...] = _dot(xn_ref[...], wg)

    u_ref[...] = _dot(xn_ref[...], w)


def _inproj_first_kernel(x_ref, xs_ref, g_ref, wm_ref, wx_ref, wg_ref,
                         u_ref, gad_ref, us_ref, gads_ref, w16_ref, wg16_ref,
                         xn_ref, xns_ref, *, na, nb, tn):
    j = pl.program_id(0)
    shifted = (j < na) | (j >= na + nb)

    @pl.when(shifted)
    def _():
        wide = jnp.concatenate([wm_ref[...], wx_ref[...]], axis=1)
        w16_ref[...] = wide[:, GLA_RANK:GLA_RANK + tn].astype(BF16)

    @pl.when(jnp.logical_not(shifted))
    def _():
        w16_ref[...] = wm_ref[...].astype(BF16)

    w = w16_ref[...]
    wg = wg_ref[:, :GLA_RANK].astype(BF16)

    @pl.when(j == 0)
    def _():
        wg16_ref[...] = wg

    _inproj_rows(j, x_ref, xn_ref, g_ref, w, wg, u_ref, gad_ref)
    _inproj_rows(j, xs_ref, xns_ref, g_ref, w, wg, us_ref, gads_ref)


def _inproj_rest_kernel(x_ref, g_ref, w_ref, wg_ref, u_first_ref, gad_first_ref, u_ref, gad_ref, xn_ref):
    del u_first_ref, gad_first_ref
    _inproj_rows(pl.program_id(1), x_ref, xn_ref, g_ref, w_ref[...], wg_ref[...], u_ref, gad_ref)


def _inproj(x, xs, gain, w_in, *, tm, tn_first, tn, name):
    rows, rows_s = x.shape[0], xs.shape[0]
    na, nb, ncc = N_GRP_A // tn_first, N_GRP_B // tn_first, N_GRP_C // tn_first
    nj = na + nb + ncc
    c_gad = N_GRP_B
    c_sq = c_gad + GLA_RANK
    c_ga = c_sq + N_GRP_C
    assert c_gad % tn_first == 0 and (c_ga - GLA_RANK) % tn_first == 0
    assert tn_first % LANE == 0 and c_gad % LANE == 0

    def window(j):
        return jnp.where(j < na, (c_ga - GLA_RANK) // tn_first + j,
                         jnp.where(j < na + nb, j - na, c_gad // tn_first + j - na - nb))

    u, gad, us, gads, w16, wg16 = pl.pallas_call(
        functools.partial(_inproj_first_kernel, na=na, nb=nb, tn=tn_first),
        grid=(nj,),
        in_specs=[
            pl.BlockSpec((tm, D_MODEL), lambda j: (0, 0), pipeline_mode=pl.Buffered(1)),
            pl.BlockSpec((rows_s, D_MODEL), lambda j: (0, 0), pipeline_mode=pl.Buffered(1)),
            pl.BlockSpec((1, D_MODEL), lambda j: (0, 0)),
            pl.BlockSpec((D_MODEL, tn_first), lambda j: (0, window(j))),
            pl.BlockSpec((D_MODEL, LANE), lambda j: (0, (window(j) + 1) * (tn_first // LANE))),
            pl.BlockSpec((D_MODEL, LANE), lambda j: (0, c_gad // LANE)),
        ],
        out_specs=[
            pl.BlockSpec((tm, tn_first), lambda j: (0, j)),
            pl.BlockSpec((tm, GLA_RANK), lambda j: (0, 0)),
            pl.BlockSpec((rows_s, tn_first), lambda j: (0, j)),
            pl.BlockSpec((rows_s, GLA_RANK), lambda j: (0, 0)),
            pl.BlockSpec((D_MODEL, tn_first), lambda j: (0, j)),
            pl.BlockSpec((D_MODEL, GLA_RANK), lambda j: (0, 0)),
        ],
        out_shape=[
            jax.ShapeDtypeStruct((rows, N_MAIN), F32),
            jax.ShapeDtypeStruct((rows, GLA_RANK), F32),
            jax.ShapeDtypeStruct((rows_s, N_MAIN), F32),
            jax.ShapeDtypeStruct((rows_s, GLA_RANK), F32),
            jax.ShapeDtypeStruct((D_MODEL, N_MAIN), BF16),
            jax.ShapeDtypeStruct((D_MODEL, GLA_RANK), BF16),
        ],
        scratch_shapes=[pltpu.VMEM((tm, D_MODEL), BF16), pltpu.VMEM((rows_s, D_MODEL), BF16)],
        compiler_params=pltpu.CompilerParams(
            dimension_semantics=("arbitrary",), vmem_limit_bytes=56 * MIB),
        name=name + "_first",
    )(x, xs, gain, w_in, w_in, w_in)

    u, gad = pl.pallas_call(
        _inproj_rest_kernel,
        grid=(rows // tm - 1, N_MAIN // tn),
        in_specs=[
            pl.BlockSpec((tm, D_MODEL), lambda i, j: (i + 1, 0), pipeline_mode=pl.Buffered(1)),
            pl.BlockSpec((1, D_MODEL), lambda i, j: (0, 0)),
            pl.BlockSpec((D_MODEL, tn), lambda i, j: (0, j)),
            pl.BlockSpec((D_MODEL, GLA_RANK), lambda i, j: (0, 0)),
            pl.BlockSpec(memory_space=pl.ANY),
            pl.BlockSpec(memory_space=pl.ANY),
        ],
        out_specs=[
            pl.BlockSpec((tm, tn), lambda i, j: (i + 1, j)),
            pl.BlockSpec((tm, GLA_RANK), lambda i, j: (i + 1, 0)),
        ],
        out_shape=[
            jax.ShapeDtypeStruct((rows, N_MAIN), F32),
            jax.ShapeDtypeStruct((rows, GLA_RANK), F32),
        ],
        input_output_aliases={4: 0, 5: 1},
        scratch_shapes=[pltpu.VMEM((tm, D_MODEL), BF16)],
        compiler_params=pltpu.CompilerParams(
            dimension_semantics=("parallel", "arbitrary"), vmem_limit_bytes=56 * MIB),
        name=name + "_rest",
    )(x, gain, w16, wg16, u, gad)
    return u, gad, us, gads


def _gla_head(q, k, v, b, state, *, C):
    o = _dot((q * jnp.exp(b)).astype(BF16), state.astype(BF16))

    ri = lax.broadcasted_iota(jnp.int32, (C, C), 0)
    ci = lax.broadcasted_iota(jnp.int32, (C, C), 1)
    rowd = lax.broadcasted_iota(jnp.int32, (C, GLA_DK), 0)
    att = None
    m = C // 2
    while m >= GLA_SUB:
        nblk = C // (2 * m)
        pieces = [jnp.broadcast_to(b[i * 2 * m + m - 1:i * 2 * m + m, :], (2 * m, GLA_DK))
                  for i in range(nblk)]
        bref = pieces[0] if nblk == 1 else jnp.concatenate(pieces, axis=0)
        e = jnp.exp(-jnp.abs(b - bref))
        second = (rowd & m) != 0
        ql = jnp.where(second, q * e, 0.0).astype(BF16)
        kl = jnp.where(second, 0.0, k * e).astype(BF16)
        a = _dot_nt(ql, kl)
        if nblk > 1:
            a = jnp.where((ri ^ ci) < 2 * m, a, 0.0)
        att = a if att is None else att + a
        m //= 2
    if att is not None:
        o = o + _dot(att.astype(BF16), v.astype(BF16))

    r16 = lax.broadcasted_iota(jnp.int32, (GLA_SUB, GLA_DK), 0)
    outs = []
    for i in range(C // GLA_SUB):
        sl = slice(GLA_SUB * i, GLA_SUB * (i + 1))
        bb, qq, kk, vv = b[sl], q[sl], k[sl], v[sl]
        od = o[sl]
        for s in range(GLA_SUB):
            d = jnp.where(r16 >= s, bb - bb[s:s + 1], MASK_NEG)
            w = jnp.sum(qq * jnp.exp(d) * kk[s:s + 1], axis=-1, keepdims=True)
            od = od + w * vv[s:s + 1]
        outs.append(od)
    o = outs[0] if len(outs) == 1 else jnp.concatenate(outs, axis=0)

    bl = b[C - 1:C, :]
    kd = k * jnp.exp(bl - b)
    vp = v
    if C < GLA_DK:
        kd = jnp.concatenate([kd, jnp.zeros((GLA_DK - C, GLA_DK), F32)], axis=0)
        vp = jnp.concatenate([v, jnp.zeros((GLA_DK - C, GLA_DV), F32)], axis=0)
    new_state = state * _row_to_col(jnp.exp(bl), GLA_DK) + _dot(kd.T.astype(BF16), vp.astype(BF16))
    return o, new_state


def _gla_kernel(q_ref, k_ref, v_ref, gad_ref, aup_ref, ab_ref, s0_ref, o_ref, sfin_ref, s_ref, *, C, nc):
    n = pl.program_id(1)

    @pl.when(n == 0)
    def _():
        s_ref[...] = s0_ref[...]

    logits = _dot(gad_ref[...].astype(BF16), aup_ref[...].astype(BF16)) + ab_ref[...]
    g = _log_sigmoid(logits) / GLA_TAU

    ri = lax.broadcasted_iota(jnp.int32, (C, C), 0)
    ci = lax.broadcasted_iota(jnp.int32, (C, C), 1)
    tri = jnp.where(ri >= ci, 1.0, 0.0).astype(BF16)
    g_hi = g.astype(BF16)
    r1 = g - g_hi.astype(F32)
    g_mid = r1.astype(BF16)
    g_lo = (r1 - g_mid.astype(F32)).astype(BF16)
    b_all = _dot(tri, g_hi) + _dot(tri, g_mid) + _dot(tri, g_lo)

    for h in range(GLA_HEADS):
        ks = slice(GLA_DK * h, GLA_DK * (h + 1))
        vs = slice(GLA_DV * h, GLA_DV * (h + 1))
        o, new_state = _gla_head(q_ref[:, ks] * (GLA_DK ** -0.5), k_ref[:, ks], v_ref[:, vs],
                                 b_all[:, ks], s_ref[h], C=C)
        o_ref[:, vs] = o
        s_ref[h] = new_state

    @pl.when(n == nc - 1)
    def _():
        sfin_ref[0] = s_ref[...]


def _gla(u, gad, a_up, a_bias, s0, *, batch, seq, C, row_off, name):
    nc = seq // C
    return pl.pallas_call(
        functools.partial(_gla_kernel, C=C, nc=nc),
        grid=(batch, nc),
        in_specs=[
            pl.BlockSpec((C, GLA_QK), lambda b, n: (row_off + b * nc + n, COL_GQ // GLA_QK)),
            pl.BlockSpec((C, GLA_QK), lambda b, n: (row_off + b * nc + n, COL_GK // GLA_QK)),
            pl.BlockSpec((C, GLA_V), lambda b, n: (row_off + b * nc + n, COL_GV // GLA_V)),
            pl.BlockSpec((C, GLA_RANK), lambda b, n: (row_off + b * nc + n, 0)),
            pl.BlockSpec((GLA_RANK, GLA_QK), lambda b, n: (0, 0)),
            pl.BlockSpec((1, GLA_QK), lambda b, n: (0, 0)),
            pl.BlockSpec((GLA_HEADS, GLA_DK, GLA_DV), lambda b, n: (0, 0, 0)),
        ],
        out_specs=[
            pl.BlockSpec((C, GLA_V), lambda b, n: (b * nc + n, 0)),
            pl.BlockSpec((1, GLA_HEADS, GLA_DK, GLA_DV), lambda b, n: (b, 0, 0, 0)),
        ],
        out_shape=[
            jax.ShapeDtypeStruct((batch * seq, GLA_V), F32),
            jax.ShapeDtypeStruct((batch, GLA_HEADS, GLA_DK, GLA_DV), F32),
        ],
        scratch_shapes=[pltpu.VMEM((GLA_HEADS, GLA_DK, GLA_DV), F32)],
        compiler_params=pltpu.CompilerParams(
            dimension_semantics=("parallel", "arbitrary"), vmem_limit_bytes=32 * MIB),
        name=name,
    )(u, u, u, gad, a_up, a_bias, s0)


def _gla_step_kernel(q_ref, k_ref, v_ref, gad_ref, aup_ref, ab_ref, s_ref, o_ref, sn_ref, *, bb):
    gl = jnp.concatenate([gad_ref[i] for i in range(bb)] + [jnp.zeros((8 - bb, GLA_RANK), F32)], axis=0)
    logits = _dot(gl.astype(BF16), aup_ref[...].astype(BF16)) + ab_ref[...]
    dec_all = jnp.exp(_log_sigmoid(logits) / GLA_TAU)
    for i in range(bb):
        q = q_ref[i] * (GLA_DK ** -0.5)
        k = k_ref[i]
        v = v_ref[i]
        dec = dec_all[i:i + 1]
        outs = []
        for h in range(GLA_HEADS):
            sl = slice(GLA_DK * h, GLA_DK * (h + 1))
            new_state = (s_ref[i, h] * _row_to_col(dec[:, sl], GLA_DK)
                         + _row_to_col(k[:, sl], GLA_DK) * v[:, GLA_DV * h:GLA_DV * (h + 1)])
            sn_ref[i, h] = new_state
            outs.append(jnp.sum(_row_to_col(q[:, sl], GLA_DK) * new_state, axis=0, keepdims=True))
        o_ref[i] = jnp.concatenate(outs, axis=1)


def _gla_step(u3, gad3, a_up, a_bias, state, *, batch, bb, name):
    return pl.pallas_call(
        functools.partial(_gla_step_kernel, bb=bb),
        grid=(batch // bb,),
        in_specs=[
            pl.BlockSpec((bb, 1, GLA_QK), lambda b: (b, 0, COL_GQ // GLA_QK)),
            pl.BlockSpec((bb, 1, GLA_QK), lambda b: (b, 0, COL_GK // GLA_QK)),
            pl.BlockSpec((bb, 1, GLA_V), lambda b: (b, 0, COL_GV // GLA_V)),
            pl.BlockSpec((bb, 1, GLA_RANK), lambda b: (b, 0, 0)),
            pl.BlockSpec((GLA_RANK, GLA_QK), lambda b: (0, 0)),
            pl.BlockSpec((1, GLA_QK), lambda b: (0, 0)),
            pl.BlockSpec((bb, GLA_HEADS, GLA_DK, GLA_DV), lambda b: (b, 0, 0, 0)),
        ],
        out_specs=[
            pl.BlockSpec((bb, 1, GLA_V), lambda b: (b, 0, 0)),
            pl.BlockSpec((bb, GLA_HEADS, GLA_DK, GLA_DV), lambda b: (b, 0, 0, 0)),
        ],
        out_shape=[
            jax.ShapeDtypeStruct((batch, 1, GLA_V), F32),
            jax.ShapeDtypeStruct((batch, GLA_HEADS, GLA_DK, GLA_DV), F32),
        ],
        compiler_params=pltpu.CompilerParams(dimension_semantics=("parallel",)),
        name=name,
    )(u3, u3, u3, gad3, a_up, a_bias, state)


def _alibi_slope(head):
    return 2.0 ** (-8.0 * (head + 1) / SWA_HEADS)


def _swa_kernel(sink_ref, q_ref, kc_ref, kp_ref, vc_ref, vp_ref, mk_ref, mv_ref, o_ref, bias_ref):
    n = pl.program_id(1)
    rows = SWA_GROUP * SWA_QB

    @pl.when(n <= 1)
    def _():
        r = lax.broadcasted_iota(jnp.int32, (SWA_QB, SWA_KT), 0)
        c = lax.broadcasted_iota(jnp.int32, (SWA_QB, SWA_KT), 1)
        dist = r + WINDOW - c
        distf = dist.astype(F32)
        band = (dist >= 0) & (dist < WINDOW) & (c < SWA_SPAN)
        for a in range(WINDOW // SWA_QB):
            ok = band & ((c >= WINDOW - SWA_QB * a) | (n > 0))
            for head in range(SWA_HEADS):
                val = jnp.where(ok, -_alibi_slope(head) * distf, MASK_NEG)
                val = jnp.where((c >= SWA_SPAN) & (c < SWA_SINK_COL), 0.0, val)
                val = jnp.where(c == SWA_SINK_COL, sink_ref[head], val)
                bias_ref[a, head // SWA_GROUP, pl.ds((head % SWA_GROUP) * SWA_QB, SWA_QB), :] = val

    q = q_ref[...] * (SWA_HD ** -0.5)
    pad = jnp.zeros((SWA_KT - SWA_SPAN - N_META, SWA_HD), BF16)
    for kvh in range(SWA_KV_HEADS):
        cs = slice(SWA_HD * kvh, SWA_HD * (kvh + 1))
        kwin = jnp.concatenate([kp_ref[:, cs], kc_ref[:, cs]], axis=0).astype(BF16)
        vwin = jnp.concatenate([vp_ref[:, cs], vc_ref[:, cs]], axis=0).astype(BF16)
        km = mk_ref[:, cs].astype(BF16)
        vm = mv_ref[:, cs].astype(BF16)
        for a in range(WINDOW // SWA_QB):
            qs = slice(SWA_QB * a, SWA_QB * (a + 1))
            kt = jnp.concatenate([kwin[SWA_QB * a:SWA_QB * a + SWA_SPAN], km, pad], axis=0)
            vt = jnp.concatenate([vwin[SWA_QB * a:SWA_QB * a + SWA_SPAN], vm, pad], axis=0)
            qg = jnp.concatenate(
                [q[qs, SWA_HD * (kvh * SWA_GROUP + grp):SWA_HD * (kvh * SWA_GROUP + grp + 1)]
                 for grp in range(SWA_GROUP)], axis=0).astype(BF16)
            s = _dot_nt(qg, kt) + bias_ref[a, kvh]
            p = jnp.exp(s - jnp.max(s, axis=-1, keepdims=True))
            o = _dot(p.astype(BF16), vt) / jnp.sum(p, axis=-1, keepdims=True)
            for grp in range(SWA_GROUP):
                head = kvh * SWA_GROUP + grp
                o_ref[qs, SWA_HD * head:SWA_HD * (head + 1)] = o[SWA_QB * grp:SWA_QB * (grp + 1)]


def _swa(sinks, u, u_small, *, batch, seq, meta_row_block, name):
    nb = seq // WINDOW
    kcol, vcol = COL_SK // SWA_KV, COL_SV // SWA_KV
    return pl.pallas_call(
        _swa_kernel,
        grid=(batch, nb),
        in_specs=[
            pl.BlockSpec(memory_space=pltpu.SMEM),
            pl.BlockSpec((WINDOW, SWA_Q), lambda b, n: (b * nb + n, COL_SQ // SWA_Q)),
            pl.BlockSpec((WINDOW, SWA_KV), lambda b, n: (b * nb + n, kcol)),
            pl.BlockSpec((WINDOW, SWA_KV), lambda b, n: (b * nb + jnp.maximum(n - 1, 0), kcol)),
            pl.BlockSpec((WINDOW, SWA_KV), lambda b, n: (b * nb + n, vcol)),
            pl.BlockSpec((WINDOW, SWA_KV), lambda b, n: (b * nb + jnp.maximum(n - 1, 0), vcol)),
            pl.BlockSpec((N_META, SWA_KV), lambda b, n: (meta_row_block, kcol)),
            pl.BlockSpec((N_META, SWA_KV), lambda b, n: (meta_row_block, vcol)),
        ],
        out_specs=pl.BlockSpec((WINDOW, SWA_Q), lambda b, n: (b * nb + n, 0)),
        out_shape=jax.ShapeDtypeStruct((batch * seq, SWA_Q), F32),
        scratch_shapes=[pltpu.VMEM((WINDOW // SWA_QB, SWA_KV_HEADS, SWA_GROUP * SWA_QB, SWA_KT), F32)],
        compiler_params=pltpu.CompilerParams(
            dimension_semantics=("arbitrary", "arbitrary"), vmem_limit_bytes=32 * MIB),
        name=name,
    )(sinks, u, u, u, u, u, u_small, u_small)


def _swa_step_kernel(q_ref, kold_ref, vold_ref, knew_ref, vnew_ref, mk_ref, mv_ref, sink_ref, slope_ref,
                     o_ref, nk_ref, nv_ref, *, bb):
    rows = bb * SWA_GROUP
    wcols = bb * WINDOW
    ncol = wcols + bb * N_META
    win_shift, meta_shift, grp_shift = WINDOW.bit_length() - 1, N_META.bit_length() - 1, SWA_GROUP.bit_length() - 1
    for i in range(bb):
        nk_ref[i] = jnp.concatenate([kold_ref[i, 1:, :], knew_ref[i]], axis=0)
        nv_ref[i] = jnp.concatenate([vold_ref[i, 1:, :], vnew_ref[i]], axis=0)

    r = lax.broadcasted_iota(jnp.int32, (rows, ncol), 0)
    c = lax.broadcasted_iota(jnp.int32, (rows, ncol), 1)
    in_win = c < wcols
    col_batch = jnp.where(in_win, c >> win_shift, (c - wcols) >> meta_shift)
    own = col_batch == (r >> grp_shift)
    distf = jnp.where(in_win, WINDOW - 1 - (c & (WINDOW - 1)), 0).astype(F32)
    for kvh in range(SWA_KV_HEADS):
        cs = slice(SWA_HD * kvh, SWA_HD * (kvh + 1))
        hs = slice(SWA_GROUP * kvh, SWA_GROUP * (kvh + 1))
        kcat = jnp.concatenate([nk_ref[i, :, cs] for i in range(bb)]
                               + [mk_ref[i, :, cs] for i in range(bb)], axis=0).astype(BF16)
        vcat = jnp.concatenate([nv_ref[i, :, cs] for i in range(bb)]
                               + [mv_ref[i, :, cs] for i in range(bb)], axis=0).astype(BF16)
        qh = jnp.concatenate([q_ref[i, hs, :] for i in range(bb)], axis=0)
        qh = (qh * (SWA_HD ** -0.5)).astype(BF16)
        s = jnp.where(own, _dot_nt(qh, kcat) - slope_ref[kvh] * distf, MASK_NEG)
        sink = sink_ref[kvh]
        mx = jnp.maximum(jnp.max(s, axis=-1, keepdims=True), sink)
        p = jnp.exp(s - mx)
        den = jnp.sum(p, axis=-1, keepdims=True) + jnp.exp(sink - mx)
        oh = _dot(p.astype(BF16), vcat) / den
        for i in range(bb):
            o_ref[i, hs, :] = oh[SWA_GROUP * i:SWA_GROUP * (i + 1)]


def _swa_step(q3, k_old, v_old, u3, mk, mv, sinks, *, batch, bb, name):
    def per_row(x16):
        return jnp.tile(x16.reshape(SWA_KV_HEADS, 1, SWA_GROUP), (1, bb, 1)).reshape(
            SWA_KV_HEADS, bb * SWA_GROUP, 1)

    slopes = jnp.asarray([_alibi_slope(h) for h in range(SWA_HEADS)], F32)
    head_spec = pl.BlockSpec((SWA_KV_HEADS, bb * SWA_GROUP, 1), lambda b: (0, 0, 0))
    win_spec = pl.BlockSpec((bb, WINDOW, SWA_KV), lambda b: (b, 0, 0))
    meta_spec = pl.BlockSpec((bb, N_META, SWA_KV), lambda b: (b, 0, 0))
    return pl.pallas_call(
        functools.partial(_swa_step_kernel, bb=bb),
        grid=(batch // bb,),
        in_specs=[
            pl.BlockSpec((bb, SWA_HEADS, SWA_HD), lambda b: (b, 0, 0)),
            win_spec, win_spec,
            pl.BlockSpec((bb, 1, SWA_KV), lambda b: (b, 0, COL_SK // SWA_KV)),
            pl.BlockSpec((bb, 1, SWA_KV), lambda b: (b, 0, COL_SV // SWA_KV)),
            meta_spec, meta_spec, head_spec, head_spec,
        ],
        out_specs=[pl.BlockSpec((bb, SWA_HEADS, SWA_HD), lambda b: (b, 0, 0)), win_spec, win_spec],
        out_shape=[
            jax.ShapeDtypeStruct((batch, SWA_HEADS, SWA_HD), F32),
            jax.ShapeDtypeStruct((batch, WINDOW, SWA_KV), F32),
            jax.ShapeDtypeStruct((batch, WINDOW, SWA_KV), F32),
        ],
        compiler_params=pltpu.CompilerParams(dimension_semantics=("parallel",)),
        name=name,
    )(q3, k_old, v_old, u3, u3, mk, mv, per_row(sinks), per_row(slopes))


def _merge_kernel(h_ref, og_ref, gr_ref, os_ref, ga_ref, gb_ref,
                  hs_ref, ogs_ref, grs_ref, oss_ref, gas_ref, gbs_ref,
                  onorm_ref, gbias_ref, wg_ref, ws_ref, wo_ref, out_ref, outs_ref):
    _merge_rows(h_ref, og_ref, gr_ref, os_ref, ga_ref, gb_ref, onorm_ref, gbias_ref,
                wg_ref, ws_ref, wo_ref, out_ref)

    @pl.when(pl.program_id(0) == 0)
    def _():
        _merge_rows(hs_ref, ogs_ref, grs_ref, oss_ref, gas_ref, gbs_ref, onorm_ref, gbias_ref,
                    wg_ref, ws_ref, wo_ref, outs_ref)


def _merge_rows(h_ref, og_ref, gr_ref, os_ref, ga_ref, gb_ref, onorm_ref, gbias_ref,
                wg_ref, ws_ref, wo_ref, out_ref):
    og = og_ref[...]
    parts = []
    for h in range(GLA_HEADS):
        oh = og[:, GLA_DV * h:GLA_DV * (h + 1)]
        parts.append(oh * lax.rsqrt(jnp.mean(oh * oh, axis=-1, keepdims=True) + EPS))
    gr = gr_ref[...]
    of = jnp.concatenate(parts, axis=1) * onorm_ref[...] * (gr * jax.nn.sigmoid(gr))
    gla_b = _dot(of.astype(BF16), wg_ref[...])
    swa_b = _dot(os_ref[...].astype(BF16), ws_ref[...])
    gbias = gbias_ref[...]
    mix = (jax.nn.sigmoid(ga_ref[...] + gbias[:, :D_MODEL]) * gla_b
           + jax.nn.sigmoid(gb_ref[...] + gbias[:, D_MODEL:]) * swa_b)
    out_ref[...] = h_ref[...] + _dot(mix.astype(BF16), wo_ref[...])


def _merge(h, o_gla, u, o_swa, hs, o_gla_s, us, o_swa_s, onorm, gbias, w_gla_o, w_swa_o, w_out,
           *, tm, name):
    rows, rs = h.shape[0], o_gla_s.shape[0]
    const = dict(pipeline_mode=pl.Buffered(1))

    def row_specs(r, idx):
        return [
            pl.BlockSpec((r, D_MODEL), lambda i: (idx(i), 0)),
            pl.BlockSpec((r, GLA_V), lambda i: (idx(i), 0)),
            pl.BlockSpec((r, GLA_V), lambda i: (idx(i), COL_GR // GLA_V)),
            pl.BlockSpec((r, SWA_Q), lambda i: (idx(i), 0)),
            pl.BlockSpec((r, D_MODEL), lambda i: (idx(i), COL_GA // D_MODEL)),
            pl.BlockSpec((r, D_MODEL), lambda i: (idx(i), COL_GB // D_MODEL)),
        ]

    return pl.pallas_call(
        _merge_kernel,
        grid=(rows // tm,),
        in_specs=row_specs(tm, lambda i: i) + row_specs(rs, lambda i: 0) + [
            pl.BlockSpec((1, GLA_V), lambda i: (0, 0)),
            pl.BlockSpec((1, 2 * D_MODEL), lambda i: (0, 0)),
            pl.BlockSpec((GLA_V, D_MODEL), lambda i: (0, 0), **const),
            pl.BlockSpec((SWA_Q, D_MODEL), lambda i: (0, 0), **const),
            pl.BlockSpec((D_MODEL, D_MODEL), lambda i: (0, 0), **const),
        ],
        out_specs=[
            pl.BlockSpec((tm, D_MODEL), lambda i: (i, 0)),
            pl.BlockSpec((rs, D_MODEL), lambda i: (0, 0)),
        ],
        out_shape=[
            jax.ShapeDtypeStruct((rows, D_MODEL), F32),
            jax.ShapeDtypeStruct((rs, D_MODEL), F32),
        ],
        compiler_params=pltpu.CompilerParams(
            dimension_semantics=("arbitrary",), vmem_limit_bytes=56 * MIB),
        name=name,
    )(h, o_gla, u, o_swa, u, u, hs, o_gla_s, us, o_swa_s, us, us,
      onorm, gbias, w_gla_o, w_swa_o, w_out)


def kernel(x_prompt, x_sample, cache_meta_k, cache_meta_v, cache_win_k, cache_win_v, state_gla,
           meta_tokens, ffn1_norm, ffn1_w_in, ffn1_w_out, mix_norm, w_in, gla_a_up, gla_a_bias,
           gla_out_norm, w_gla_o, swa_sinks, w_swa_o, gate_bias, w_out, ffn2_norm, ffn2_w_in,
           ffn2_w_out, final_norm):
    B, S, _ = x_prompt.shape
    DB = x_sample.shape[0]
    assert x_sample.shape[1] == 1 and ffn1_norm.shape[0] == 1
    n_small = DB + N_META

    wgo, wso, wout = w_gla_o[0].astype(BF16), w_swa_o[0].astype(BF16), w_out[0].astype(BF16)
    g1, gm, g2 = ffn1_norm, mix_norm, ffn2_norm
    gf = final_norm.reshape(1, D_MODEL)
    a_up, a_bias = gla_a_up[0], gla_a_bias
    onorm, gbias = gla_out_norm, gate_bias
    sinks = swa_sinks[0]

    xs = jnp.concatenate([x_sample.reshape(DB, D_MODEL), meta_tokens.astype(F32)], axis=0)
    xp = x_prompt.reshape(B * S, D_MODEL)
    hp1, hs1 = _ffn(xp, xs, g1, ffn1_w_in[0], ffn1_w_out[0], gf, tm=512, tf_first=256, tf=512,
                    final_norm=False, name="ffn1")
    up, gadp, us, gads = _inproj(hp1, hs1, gm, w_in[0], tm=2048, tn_first=256, tn=512, name="inproj")

    meta_blk = DB // N_META
    _, st_meta = _gla(us, gads, a_up, a_bias, jnp.zeros((GLA_HEADS, GLA_DK, GLA_DV), F32),
                      batch=1, seq=N_META, C=N_META, row_off=meta_blk, name="gla_meta")
    st_meta = st_meta[0]
    o_gla, st_fin = _gla(up, gadp, a_up, a_bias, st_meta, batch=B, seq=S, C=128, row_off=0, name="gla")
    o_swa = _swa(sinks, up, us, batch=B, seq=S, meta_row_block=meta_blk, name="swa")

    us3 = us.reshape(n_small, 1, N_MAIN)
    og_s, st_s = _gla_step(us3, gads.reshape(n_small, 1, GLA_RANK),
                           a_up, a_bias, state_gla[0], batch=DB, bb=4, name="gla_step")
    WB = cache_win_k.shape[2]
    assert WB == WINDOW
    os_s, new_k, new_v = _swa_step(
        us[:DB, COL_SQ:COL_SQ + SWA_Q].reshape(DB, SWA_HEADS, SWA_HD),
        cache_win_k[0].reshape(DB, WB, SWA_KV), cache_win_v[0].reshape(DB, WB, SWA_KV), us3,
        cache_meta_k[0].reshape(DB, N_META, SWA_KV), cache_meta_v[0].reshape(DB, N_META, SWA_KV),
        sinks, batch=DB, bb=8, name="swa_step")
    hp2, hs2 = _merge(hp1, o_gla, up, o_swa, hs1, og_s.reshape(DB, GLA_V), us, os_s.reshape(DB, SWA_Q),
                      onorm, gbias, wgo, wso, wout, tm=256, name="merge")
    y_prompt, y_sample = _ffn(hp2, hs2, g2, ffn2_w_in[0], ffn2_w_out[0], gf, tm=512, tf_first=256, tf=512,
                              final_norm=True, name="ffn2")

    up3 = up.reshape(B, S, N_MAIN)
    kv_shape = (SWA_KV_HEADS, SWA_HD)
    p_meta_k = jnp.broadcast_to(us[DB:, COL_SK:COL_SK + SWA_KV].reshape(1, 1, N_META, *kv_shape),
                                (1, B, N_META, *kv_shape))
    p_meta_v = jnp.broadcast_to(us[DB:, COL_SV:COL_SV + SWA_KV].reshape(1, 1, N_META, *kv_shape),
                                (1, B, N_META, *kv_shape))
    p_win_k = up3[:, S - WINDOW:, COL_SK:COL_SK + SWA_KV].reshape(1, B, WINDOW, *kv_shape)
    p_win_v = up3[:, S - WINDOW:, COL_SV:COL_SV + SWA_KV].reshape(1, B, WINDOW, *kv_shape)
    return (y_prompt.reshape(B, S, D_MODEL), y_sample.reshape(DB, 1, D_MODEL),
            p_meta_k, p_meta_v, p_win_k, p_win_v, st_fin[None],
            new_k.reshape(1, DB, WB, *kv_shape), new_v.reshape(1, DB, WB, *kv_shape), st_s[None])
```

```python
import functools

import jax
import jax.numpy as jnp
from jax import lax
from jax.experimental import pallas as pl
from jax.experimental.pallas import tpu as pltpu

F32 = jnp.float32
BF16 = jnp.bfloat16

D_MODEL = 2048
N_META = 16
GLA_HEADS = 4
GLA_DK = 128
GLA_DV = 256
GLA_RANK = 16
GLA_TAU = 16.0
SWA_HEADS = 16
SWA_KV_HEADS = 4
SWA_GROUP = SWA_HEADS // SWA_KV_HEADS
SWA_HD = 64
WINDOW = 128
D_FF = 5632
EPS = 1e-6

GLA_QK = GLA_HEADS * GLA_DK
GLA_V = GLA_HEADS * GLA_DV
SWA_Q = SWA_HEADS * SWA_HD
SWA_KV = SWA_KV_HEADS * SWA_HD

COL_GA = 0
COL_GB = COL_GA + D_MODEL
COL_GQ = COL_GB + D_MODEL
COL_GK = COL_GQ + GLA_QK
COL_GV = COL_GK + GLA_QK
COL_GR = COL_GV + GLA_V
COL_SQ = COL_GR + GLA_V
COL_SK = COL_SQ + SWA_Q
COL_SV = COL_SK + SWA_KV
N_MAIN = COL_SV + SWA_KV
N_GRP_A = COL_GQ
N_GRP_B = COL_SQ - COL_GQ
N_GRP_C = N_MAIN - COL_SQ

GLA_SUB = 8
MASK_NEG = -1e30
MIB = 1024 * 1024
LANE = 128
ROW_CHUNK = 256
COL_CHUNK = 512

SWA_QB = 64
SWA_SPAN = WINDOW + SWA_QB
SWA_KT = 256
SWA_SINK_COL = SWA_SPAN + N_META


def _rms_scale(x):
    return x * lax.rsqrt(jnp.mean(x * x, axis=-1, keepdims=True) + EPS)


def _log_sigmoid(x):
    return jnp.minimum(x, 0.0) - jnp.log(1.0 + jnp.exp(-jnp.abs(x)))


def _dot(a, b):
    return jnp.dot(a, b, preferred_element_type=F32)


def _dot_nt(a, b):
    return lax.dot_general(a, b, (((1,), (1,)), ((), ())), preferred_element_type=F32)


def _row_to_col(row, n):
    r = lax.broadcasted_iota(jnp.int32, (n, n), 0)
    c = lax.broadcasted_iota(jnp.int32, (n, n), 1)
    return jnp.sum(jnp.where(r == c, jnp.broadcast_to(row, (n, n)), 0.0), axis=1, keepdims=True)


def _for_row_chunks(rows, fn):
    chunk = ROW_CHUNK if rows % ROW_CHUNK == 0 else rows

    def body(i, carry):
        fn(pl.ds(pl.multiple_of(i * chunk, chunk), chunk))
        return carry

    lax.fori_loop(0, rows // chunk, body, 0)


def _ffn_rows(j, n_ff, x_ref, o_ref, xn_ref, g_ref, fg_ref, wa, wb, wo, final_norm):
    rows = x_ref.shape[0]

    @pl.when(j == 0)
    def _():
        def norm_rows(sl):
            x = x_ref[sl, :]
            xn_ref[sl, :] = (_rms_scale(x) * g_ref[...]).astype(BF16)
            o_ref[sl, :] = x

        _for_row_chunks(rows, norm_rows)

    xn = xn_ref[...]
    a = _dot(xn, wa)
    b = _dot(xn, wb)
    h = ((0.5 * a) * jax.nn.sigmoid(a) * b).astype(BF16)
    for c in range(D_MODEL // COL_CHUNK):
        cs = slice(COL_CHUNK * c, COL_CHUNK * (c + 1))
        o_ref[:, cs] += _dot(h, wo[:, cs])

    if final_norm:
        @pl.when(j == n_ff - 1)
        def _():
            def final_rows(sl):
                o_ref[sl, :] = _rms_scale(o_ref[sl, :]) * fg_ref[...]

            _for_row_chunks(rows, final_rows)


def _ffn_first_kernel(x_ref, xs_ref, g_ref, wa_ref, wb_ref, wo_ref, fg_ref,
                      o_ref, os_ref, wa16_ref, wb16_ref, wo16_ref, xn_ref, xns_ref, *, n_ff, final_norm):
    j = pl.program_id(0)
    wa = wa_ref[...].astype(BF16)
    wb = wb_ref[...].astype(BF16)
    wo = wo_ref[...].astype(BF16)
    wa16_ref[...] = wa
    wb16_ref[...] = wb
    wo16_ref[...] = wo
    _ffn_rows(j, n_ff, x_ref, o_ref, xn_ref, g_ref, fg_ref, wa, wb, wo, final_norm)
    _ffn_rows(j, n_ff, xs_ref, os_ref, xns_ref, g_ref, fg_ref, wa, wb, wo, final_norm)


def _ffn_rest_kernel(x_ref, g_ref, wa_ref, wb_ref, wo_ref, fg_ref, o_first_ref, o_ref, xn_ref,
                     *, n_ff, final_norm):
    del o_first_ref
    _ffn_rows(pl.program_id(1), n_ff, x_ref, o_ref, xn_ref, g_ref, fg_ref,
              wa_ref[...], wb_ref[...], wo_ref[...], final_norm)


def _ffn(x, xs, gain, w_in, w_out, final_gain, *, tm, tf_first, tf, final_norm, name):
    rows, rows_s = x.shape[0], xs.shape[0]
    n1 = D_FF // tf_first
    o, os_, wa16, wb16, wo16 = pl.pallas_call(
        functools.partial(_ffn_first_kernel, n_ff=n1, final_norm=final_norm),
        grid=(n1,),
        in_specs=[
            pl.BlockSpec((tm, D_MODEL), lambda j: (0, 0), pipeline_mode=pl.Buffered(1)),
            pl.BlockSpec((rows_s, D_MODEL), lambda j: (0, 0), pipeline_mode=pl.Buffered(1)),
            pl.BlockSpec((1, D_MODEL), lambda j: (0, 0)),
            pl.BlockSpec((D_MODEL, tf_first), lambda j: (0, j)),
            pl.BlockSpec((D_MODEL, tf_first), lambda j: (0, j + n1)),
            pl.BlockSpec((tf_first, D_MODEL), lambda j: (j, 0)),
            pl.BlockSpec((1, D_MODEL), lambda j: (0, 0)),
        ],
        out_specs=[
            pl.BlockSpec((tm, D_MODEL), lambda j: (0, 0)),
            pl.BlockSpec((rows_s, D_MODEL), lambda j: (0, 0)),
            pl.BlockSpec((D_MODEL, tf_first), lambda j: (0, j)),
            pl.BlockSpec((D_MODEL, tf_first), lambda j: (0, j)),
            pl.BlockSpec((tf_first, D_MODEL), lambda j: (j, 0)),
        ],
        out_shape=[
            jax.ShapeDtypeStruct((rows, D_MODEL), F32),
            jax.ShapeDtypeStruct((rows_s, D_MODEL), F32),
            jax.ShapeDtypeStruct((D_MODEL, D_FF), BF16),
            jax.ShapeDtypeStruct((D_MODEL, D_FF), BF16),
            jax.ShapeDtypeStruct((D_FF, D_MODEL), BF16),
        ],
        scratch_shapes=[pltpu.VMEM((tm, D_MODEL), BF16), pltpu.VMEM((rows_s, D_MODEL), BF16)],
        compiler_params=pltpu.CompilerParams(
            dimension_semantics=("arbitrary",), vmem_limit_bytes=58 * MIB),
        name=name + "_first",
    )(x, xs, gain, w_in, w_in, w_out, final_gain)

    n2 = D_FF // tf
    o = pl.pallas_call(
        functools.partial(_ffn_rest_kernel, n_ff=n2, final_norm=final_norm),
        grid=(rows // tm - 1, n2),
        in_specs=[
            pl.BlockSpec((tm, D_MODEL), lambda i, j: (i + 1, 0), pipeline_mode=pl.Buffered(1)),
            pl.BlockSpec((1, D_MODEL), lambda i, j: (0, 0)),
            pl.BlockSpec((D_MODEL, tf), lambda i, j: (0, j)),
            pl.BlockSpec((D_MODEL, tf), lambda i, j: (0, j)),
            pl.BlockSpec((tf, D_MODEL), lambda i, j: (j, 0)),
            pl.BlockSpec((1, D_MODEL), lambda i, j: (0, 0)),
            pl.BlockSpec(memory_space=pl.ANY),
        ],
        out_specs=pl.BlockSpec((tm, D_MODEL), lambda i, j: (i + 1, 0)),
        out_shape=jax.ShapeDtypeStruct((rows, D_MODEL), F32),
        input_output_aliases={6: 0},
        scratch_shapes=[pltpu.VMEM((tm, D_MODEL), BF16)],
        compiler_params=pltpu.CompilerParams(
            dimension_semantics=("parallel", "arbitrary"), vmem_limit_bytes=58 * MIB),
        name=name + "_rest",
    )(x, gain, wa16, wb16, wo16, final_gain, o)
    return o, os_


def _inproj_rows(j, x_ref, xn_ref, g_ref, w, wg, u_ref, gad_ref):
    @pl.when(j == 0)
    def _():
        def norm_rows(sl):
            xn_ref[sl, :] = (_rms_scale(x_ref[sl, :]) * g_ref[...]).astype(BF16)

        _for_row_chunks(x_ref.shape[0], norm_rows)
        gad_ref[...] = _dot_nt(xn_ref[...], wg)

    u_ref[...] = _dot(xn_ref[...], w)


def _inproj_first_kernel(x_ref, xs_ref, g_ref, wm_ref, wx_ref, wg_ref,
                         u_ref, gad_ref, us_ref, gads_ref, w16_ref, wg16_ref,
                         xn_ref, xns_ref, *, na, nb, tn):
    j = pl.program_id(0)
    shifted = (j < na) | (j >= na + nb)

    @pl.when(shifted)
    def _():
        tall = jnp.concatenate([wm_ref[...], wx_ref[...]], axis=0)
        w16_ref[...] = tall[GLA_RANK:GLA_RANK + tn].T.astype(BF16)

    @pl.when(jnp.logical_not(shifted))
    def _():
        w16_ref[...] = wm_ref[...].T.astype(BF16)

    w = w16_ref[...]
    wg = wg_ref[...].astype(BF16)

    @pl.when(j == 0)
    def _():
        wg16_ref[...] = wg

    _inproj_rows(j, x_ref, xn_ref, g_ref, w, wg, u_ref, gad_ref)
    _inproj_rows(j, xs_ref, xns_ref, g_ref, w, wg, us_ref, gads_ref)


def _inproj_rest_kernel(x_ref, g_ref, w_ref, wg_ref, u_first_ref, gad_first_ref, u_ref, gad_ref, xn_ref):
    del u_first_ref, gad_first_ref
    _inproj_rows(pl.program_id(1), x_ref, xn_ref, g_ref, w_ref[...], wg_ref[...], u_ref, gad_ref)


def _inproj(x, xs, gain, w_in, *, tm, tn_first, tn, name):
    rows, rows_s = x.shape[0], xs.shape[0]
    na, nb, ncc = N_GRP_A // tn_first, N_GRP_B // tn_first, N_GRP_C // tn_first
    nj = na + nb + ncc
    c_gad = N_GRP_B
    c_sq = c_gad + GLA_RANK
    c_ga = c_sq + N_GRP_C
    assert c_gad % tn_first == 0 and (c_ga - GLA_RANK) % tn_first == 0
    assert tn_first % LANE == 0 and c_gad % GLA_RANK == 0

    def window(j):
        return jnp.where(j < na, (c_ga - GLA_RANK) // tn_first + j,
                         jnp.where(j < na + nb, j - na, c_gad // tn_first + j - na - nb))

    u, gad, us, gads, w16, wg16 = pl.pallas_call(
        functools.partial(_inproj_first_kernel, na=na, nb=nb, tn=tn_first),
        grid=(nj,),
        in_specs=[
            pl.BlockSpec((tm, D_MODEL), lambda j: (0, 0), pipeline_mode=pl.Buffered(1)),
            pl.BlockSpec((rows_s, D_MODEL), lambda j: (0, 0), pipeline_mode=pl.Buffered(1)),
            pl.BlockSpec((1, D_MODEL), lambda j: (0, 0)),
            pl.BlockSpec((tn_first, D_MODEL), lambda j: (window(j), 0)),
            pl.BlockSpec((GLA_RANK, D_MODEL), lambda j: ((window(j) + 1) * (tn_first // GLA_RANK), 0)),
            pl.BlockSpec((GLA_RANK, D_MODEL), lambda j: (c_gad // GLA_RANK, 0)),
        ],
        out_specs=[
            pl.BlockSpec((tm, tn_first), lambda j: (0, j)),
            pl.BlockSpec((tm, GLA_RANK), lambda j: (0, 0)),
            pl.BlockSpec((rows_s, tn_first), lambda j: (0, j)),
            pl.BlockSpec((rows_s, GLA_RANK), lambda j: (0, 0)),
            pl.BlockSpec((D_MODEL, tn_first), lambda j: (0, j)),
            pl.BlockSpec((GLA_RANK, D_MODEL), lambda j: (0, 0)),
        ],
        out_shape=[
            jax.ShapeDtypeStruct((rows, N_MAIN), F32),
            jax.ShapeDtypeStruct((rows, GLA_RANK), F32),
            jax.ShapeDtypeStruct((rows_s, N_MAIN), F32),
            jax.ShapeDtypeStruct((rows_s, GLA_RANK), F32),
            jax.ShapeDtypeStruct((D_MODEL, N_MAIN), BF16),
            jax.ShapeDtypeStruct((GLA_RANK, D_MODEL), BF16),
        ],
        scratch_shapes=[pltpu.VMEM((tm, D_MODEL), BF16), pltpu.VMEM((rows_s, D_MODEL), BF16)],
        compiler_params=pltpu.CompilerParams(
            dimension_semantics=("arbitrary",), vmem_limit_bytes=56 * MIB),
        name=name + "_first",
    )(x, xs, gain, w_in, w_in, w_in)

    u, gad = pl.pallas_call(
        _inproj_rest_kernel,
        grid=(rows // tm - 1, N_MAIN // tn),
        in_specs=[
            pl.BlockSpec((tm, D_MODEL), lambda i, j: (i + 1, 0), pipeline_mode=pl.Buffered(1)),
            pl.BlockSpec((1, D_MODEL), lambda i, j: (0, 0)),
            pl.BlockSpec((D_MODEL, tn), lambda i, j: (0, j)),
            pl.BlockSpec((GLA_RANK, D_MODEL), lambda i, j: (0, 0)),
            pl.BlockSpec(memory_space=pl.ANY),
            pl.BlockSpec(memory_space=pl.ANY),
        ],
        out_specs=[
            pl.BlockSpec((tm, tn), lambda i, j: (i + 1, j)),
            pl.BlockSpec((tm, GLA_RANK), lambda i, j: (i + 1, 0)),
        ],
        out_shape=[
            jax.ShapeDtypeStruct((rows, N_MAIN), F32),
            jax.ShapeDtypeStruct((rows, GLA_RANK), F32),
        ],
        input_output_aliases={4: 0, 5: 1},
        scratch_shapes=[pltpu.VMEM((tm, D_MODEL), BF16)],
        compiler_params=pltpu.CompilerParams(
            dimension_semantics=("parallel", "arbitrary"), vmem_limit_bytes=56 * MIB),
        name=name + "_rest",
    )(x, gain, w16, wg16, u, gad)
    return u, gad, us, gads


def _gla_head(q, k, v, b, state, *, C):
    o = _dot((q * jnp.exp(b)).astype(BF16), state.astype(BF16))

    ri = lax.broadcasted_iota(jnp.int32, (C, C), 0)
    ci = lax.broadcasted_iota(jnp.int32, (C, C), 1)
    rowd = lax.broadcasted_iota(jnp.int32, (C, GLA_DK), 0)
    att = None
    m = C // 2
    while m >= GLA_SUB:
        nblk = C // (2 * m)
        pieces = [jnp.broadcast_to(b[i * 2 * m + m - 1:i * 2 * m + m, :], (2 * m, GLA_DK))
                  for i in range(nblk)]
        bref = pieces[0] if nblk == 1 else jnp.concatenate(pieces, axis=0)
        e = jnp.exp(-jnp.abs(b - bref))
        second = (rowd & m) != 0
        ql = jnp.where(second, q * e, 0.0).astype(BF16)
        kl = jnp.where(second, 0.0, k * e).astype(BF16)
        a = _dot_nt(ql, kl)
        if nblk > 1:
            a = jnp.where((ri ^ ci) < 2 * m, a, 0.0)
        att = a if att is None else att + a
        m //= 2

    sub_row = lax.broadcasted_iota(jnp.int32, (GLA_SUB, GLA_DK), 0)
    sub_col = lax.broadcasted_iota(jnp.int32, (GLA_SUB, C), 1)
    diag = []
    for i in range(C // GLA_SUB):
        sl = slice(GLA_SUB * i, GLA_SUB * (i + 1))
        bb, qq, kk = b[sl], q[sl], k[sl]
        blk = jnp.zeros((GLA_SUB, C), F32)
        for s in range(GLA_SUB):
            d = jnp.where(sub_row >= s, bb - bb[s:s + 1], MASK_NEG)
            w = jnp.sum(qq * jnp.exp(d) * kk[s:s + 1], axis=-1, keepdims=True)
            blk = jnp.where(sub_col == GLA_SUB * i + s, w, blk)
        diag.append(blk)
    diag = diag[0] if len(diag) == 1 else jnp.concatenate(diag, axis=0)
    att = diag if att is None else att + diag
    o = o + _dot(att.astype(BF16), v.astype(BF16))

    bl = b[C - 1:C, :]
    kd = k * jnp.exp(bl - b)
    vp = v
    if C < GLA_DK:
        kd = jnp.concatenate([kd, jnp.zeros((GLA_DK - C, GLA_DK), F32)], axis=0)
        vp = jnp.concatenate([v, jnp.zeros((GLA_DK - C, GLA_DV), F32)], axis=0)
    new_state = state * _row_to_col(jnp.exp(bl), GLA_DK) + _dot(kd.T.astype(BF16), vp.astype(BF16))
    return o, new_state


def _gla_kernel(q_ref, k_ref, v_ref, gad_ref, aup_ref, ab_ref, s0_ref, o_ref, sfin_ref, s_ref, *, C, nc):
    n = pl.program_id(1)

    @pl.when(n == 0)
    def _():
        s_ref[...] = s0_ref[...]

    logits = _dot(gad_ref[...].astype(BF16), aup_ref[...].astype(BF16)) + ab_ref[...]
    g = _log_sigmoid(logits) / GLA_TAU

    ri = lax.broadcasted_iota(jnp.int32, (C, C), 0)
    ci = lax.broadcasted_iota(jnp.int32, (C, C), 1)
    tri = jnp.where(ri >= ci, 1.0, 0.0).astype(BF16)
    g_hi = g.astype(BF16)
    r1 = g - g_hi.astype(F32)
    g_mid = r1.astype(BF16)
    g_lo = (r1 - g_mid.astype(F32)).astype(BF16)
    b_all = _dot(tri, g_hi) + _dot(tri, g_mid) + _dot(tri, g_lo)

    for h in range(GLA_HEADS):
        ks = slice(GLA_DK * h, GLA_DK * (h + 1))
        vs = slice(GLA_DV * h, GLA_DV * (h + 1))
        o, new_state = _gla_head(q_ref[:, ks] * (GLA_DK ** -0.5), k_ref[:, ks], v_ref[:, vs],
                                 b_all[:, ks], s_ref[h], C=C)
        o_ref[:, vs] = o
        s_ref[h] = new_state

    @pl.when(n == nc - 1)
    def _():
        sfin_ref[0] = s_ref[...]


def _gla(u, gad, a_up, a_bias, s0, *, batch, seq, C, row_off, name):
    nc = seq // C
    return pl.pallas_call(
        functools.partial(_gla_kernel, C=C, nc=nc),
        grid=(batch, nc),
        in_specs=[
            pl.BlockSpec((C, GLA_QK), lambda b, n: (row_off + b * nc + n, COL_GQ // GLA_QK)),
            pl.BlockSpec((C, GLA_QK), lambda b, n: (row_off + b * nc + n, COL_GK // GLA_QK)),
            pl.BlockSpec((C, GLA_V), lambda b, n: (row_off + b * nc + n, COL_GV // GLA_V)),
            pl.BlockSpec((C, GLA_RANK), lambda b, n: (row_off + b * nc + n, 0)),
            pl.BlockSpec((GLA_RANK, GLA_QK), lambda b, n: (0, 0)),
            pl.BlockSpec((1, GLA_QK), lambda b, n: (0, 0)),
            pl.BlockSpec((GLA_HEADS, GLA_DK, GLA_DV), lambda b, n: (0, 0, 0)),
        ],
        out_specs=[
            pl.BlockSpec((C, GLA_V), lambda b, n: (b * nc + n, 0)),
            pl.BlockSpec((1, GLA_HEADS, GLA_DK, GLA_DV), lambda b, n: (b, 0, 0, 0)),
        ],
        out_shape=[
            jax.ShapeDtypeStruct((batch * seq, GLA_V), F32),
            jax.ShapeDtypeStruct((batch, GLA_HEADS, GLA_DK, GLA_DV), F32),
        ],
        scratch_shapes=[pltpu.VMEM((GLA_HEADS, GLA_DK, GLA_DV), F32)],
        compiler_params=pltpu.CompilerParams(
            dimension_semantics=("parallel", "arbitrary"), vmem_limit_bytes=32 * MIB),
        name=name,
    )(u, u, u, gad, a_up, a_bias, s0)


def _gla_step_kernel(q_ref, k_ref, v_ref, gad_ref, aup_ref, ab_ref, s_ref, o_ref, sn_ref, *, bb):
    gl = jnp.concatenate([gad_ref[i] for i in range(bb)] + [jnp.zeros((8 - bb, GLA_RANK), F32)], axis=0)
    logits = _dot(gl.astype(BF16), aup_ref[...].astype(BF16)) + ab_ref[...]
    dec_all = jnp.exp(_log_sigmoid(logits) / GLA_TAU)
    for i in range(bb):
        q = q_ref[i] * (GLA_DK ** -0.5)
        k = k_ref[i]
        v = v_ref[i]
        dec = dec_all[i:i + 1]
        outs = []
        for h in range(GLA_HEADS):
            sl = slice(GLA_DK * h, GLA_DK * (h + 1))
            new_state = (s_ref[i, h] * _row_to_col(dec[:, sl], GLA_DK)
                         + _row_to_col(k[:, sl], GLA_DK) * v[:, GLA_DV * h:GLA_DV * (h + 1)])
            sn_ref[i, h] = new_state
            outs.append(jnp.sum(_row_to_col(q[:, sl], GLA_DK) * new_state, axis=0, keepdims=True))
        o_ref[i] = jnp.concatenate(outs, axis=1)


def _gla_step(u3, gad3, a_up, a_bias, state, *, batch, bb, name):
    return pl.pallas_call(
        functools.partial(_gla_step_kernel, bb=bb),
        grid=(batch // bb,),
        in_specs=[
            pl.BlockSpec((bb, 1, GLA_QK), lambda b: (b, 0, COL_GQ // GLA_QK)),
            pl.BlockSpec((bb, 1, GLA_QK), lambda b: (b, 0, COL_GK // GLA_QK)),
            pl.BlockSpec((bb, 1, GLA_V), lambda b: (b, 0, COL_GV // GLA_V)),
            pl.BlockSpec((bb, 1, GLA_RANK), lambda b: (b, 0, 0)),
            pl.BlockSpec((GLA_RANK, GLA_QK), lambda b: (0, 0)),
            pl.BlockSpec((1, GLA_QK), lambda b: (0, 0)),
            pl.BlockSpec((bb, GLA_HEADS, GLA_DK, GLA_DV), lambda b: (b, 0, 0, 0)),
        ],
        out_specs=[
            pl.BlockSpec((bb, 1, GLA_V), lambda b: (b, 0, 0)),
            pl.BlockSpec((bb, GLA_HEADS, GLA_DK, GLA_DV), lambda b: (b, 0, 0, 0)),
        ],
        out_shape=[
            jax.ShapeDtypeStruct((batch, 1, GLA_V), F32),
            jax.ShapeDtypeStruct((batch, GLA_HEADS, GLA_DK, GLA_DV), F32),
        ],
        compiler_params=pltpu.CompilerParams(dimension_semantics=("parallel",)),
        name=name,
    )(u3, u3, u3, gad3, a_up, a_bias, state)


def _alibi_slope(head):
    return 2.0 ** (-8.0 * (head + 1) / SWA_HEADS)


def _swa_kernel(sink_ref, q_ref, kc_ref, kp_ref, vc_ref, vp_ref, mk_ref, mv_ref, o_ref, bias_ref):
    n = pl.program_id(1)
    rows = SWA_GROUP * SWA_QB
    nsub = WINDOW // SWA_QB

    @pl.when(n <= 1)
    def _():
        r = lax.broadcasted_iota(jnp.int32, (SWA_QB, SWA_KT), 0)
        c = lax.broadcasted_iota(jnp.int32, (SWA_QB, SWA_KT), 1)
        dist = r + WINDOW - c
        distf = dist.astype(F32)
        band = (dist >= 0) & (dist < WINDOW) & (c < SWA_SPAN)
        for a in range(nsub):
            ok = band & ((c >= WINDOW - SWA_QB * a) | (n > 0))
            for head in range(SWA_HEADS):
                val = jnp.where(ok, -_alibi_slope(head) * distf, MASK_NEG)
                val = jnp.where((c >= SWA_SPAN) & (c < SWA_SINK_COL), 0.0, val)
                val = jnp.where(c == SWA_SINK_COL, sink_ref[head], val)
                tile = (head // SWA_GROUP) * nsub + a
                bias_ref[pl.ds(tile * rows + (head % SWA_GROUP) * SWA_QB, SWA_QB), :] = val

    q = q_ref[...] * (SWA_HD ** -0.5)
    pad = jnp.zeros((SWA_KT - SWA_SPAN - N_META, SWA_HD), BF16)
    scores, values = [], []
    for kvh in range(SWA_KV_HEADS):
        cs = slice(SWA_HD * kvh, SWA_HD * (kvh + 1))
        kwin = jnp.concatenate([kp_ref[:, cs], kc_ref[:, cs]], axis=0).astype(BF16)
        vwin = jnp.concatenate([vp_ref[:, cs], vc_ref[:, cs]], axis=0).astype(BF16)
        km = mk_ref[:, cs].astype(BF16)
        vm = mv_ref[:, cs].astype(BF16)
        for a in range(nsub):
            qs = slice(SWA_QB * a, SWA_QB * (a + 1))
            kt = jnp.concatenate([kwin[SWA_QB * a:SWA_QB * a + SWA_SPAN], km, pad], axis=0)
            values.append(jnp.concatenate([vwin[SWA_QB * a:SWA_QB * a + SWA_SPAN], vm, pad], axis=0))
            qg = jnp.concatenate(
                [q[qs, SWA_HD * (kvh * SWA_GROUP + grp):SWA_HD * (kvh * SWA_GROUP + grp + 1)]
                 for grp in range(SWA_GROUP)], axis=0).astype(BF16)
            scores.append(_dot_nt(qg, kt))
    s = jnp.concatenate(scores, axis=0) + bias_ref[...]
    p = jnp.exp(s - jnp.max(s, axis=-1, keepdims=True))
    inv = 1.0 / jnp.sum(p, axis=-1, keepdims=True)
    p = p.astype(BF16)
    for kvh in range(SWA_KV_HEADS):
        for a in range(nsub):
            tile = kvh * nsub + a
            ts = slice(rows * tile, rows * (tile + 1))
            o = _dot(p[ts], values[tile]) * inv[ts]
            for grp in range(SWA_GROUP):
                head = kvh * SWA_GROUP + grp
                o_ref[SWA_QB * a:SWA_QB * (a + 1), SWA_HD * head:SWA_HD * (head + 1)] = (
                    o[SWA_QB * grp:SWA_QB * (grp + 1)])


def _swa(sinks, u, u_small, *, batch, seq, meta_row_block, name):
    nb = seq // WINDOW
    kcol, vcol = COL_SK // SWA_KV, COL_SV // SWA_KV
    return pl.pallas_call(
        _swa_kernel,
        grid=(batch, nb),
        in_specs=[
            pl.BlockSpec(memory_space=pltpu.SMEM),
            pl.BlockSpec((WINDOW, SWA_Q), lambda b, n: (b * nb + n, COL_SQ // SWA_Q)),
            pl.BlockSpec((WINDOW, SWA_KV), lambda b, n: (b * nb + n, kcol)),
            pl.BlockSpec((WINDOW, SWA_KV), lambda b, n: (b * nb + jnp.maximum(n - 1, 0), kcol)),
            pl.BlockSpec((WINDOW, SWA_KV), lambda b, n: (b * nb + n, vcol)),
            pl.BlockSpec((WINDOW, SWA_KV), lambda b, n: (b * nb + jnp.maximum(n - 1, 0), vcol)),
            pl.BlockSpec((N_META, SWA_KV), lambda b, n: (meta_row_block, kcol)),
            pl.BlockSpec((N_META, SWA_KV), lambda b, n: (meta_row_block, vcol)),
        ],
        out_specs=pl.BlockSpec((WINDOW, SWA_Q), lambda b, n: (b * nb + n, 0)),
        out_shape=jax.ShapeDtypeStruct((batch * seq, SWA_Q), F32),
        scratch_shapes=[pltpu.VMEM((SWA_HEADS * WINDOW, SWA_KT), F32)],
        compiler_params=pltpu.CompilerParams(
            dimension_semantics=("arbitrary", "arbitrary"), vmem_limit_bytes=32 * MIB),
        name=name,
    )(sinks, u, u, u, u, u, u_small, u_small)


def _swa_step_kernel(q_ref, kold_ref, vold_ref, knew_ref, vnew_ref, mk_ref, mv_ref, sink_ref, slope_ref,
                     o_ref, nk_ref, nv_ref, *, bb):
    rows = bb * SWA_GROUP
    wcols = bb * WINDOW
    ncol = wcols + bb * N_META
    win_shift, meta_shift, grp_shift = WINDOW.bit_length() - 1, N_META.bit_length() - 1, SWA_GROUP.bit_length() - 1
    for i in range(bb):
        nk_ref[i] = jnp.concatenate([kold_ref[i, 1:, :], knew_ref[i]], axis=0)
        nv_ref[i] = jnp.concatenate([vold_ref[i, 1:, :], vnew_ref[i]], axis=0)

    r = lax.broadcasted_iota(jnp.int32, (rows, ncol), 0)
    c = lax.broadcasted_iota(jnp.int32, (rows, ncol), 1)
    in_win = c < wcols
    col_batch = jnp.where(in_win, c >> win_shift, (c - wcols) >> meta_shift)
    own = col_batch == (r >> grp_shift)
    distf = jnp.where(in_win, WINDOW - 1 - (c & (WINDOW - 1)), 0).astype(F32)
    for kvh in range(SWA_KV_HEADS):
        cs = slice(SWA_HD * kvh, SWA_HD * (kvh + 1))
        hs = slice(SWA_GROUP * kvh, SWA_GROUP * (kvh + 1))
        kcat = jnp.concatenate([nk_ref[i, :, cs] for i in range(bb)]
                               + [mk_ref[i, :, cs] for i in range(bb)], axis=0).astype(BF16)
        vcat = jnp.concatenate([nv_ref[i, :, cs] for i in range(bb)]
                               + [mv_ref[i, :, cs] for i in range(bb)], axis=0).astype(BF16)
        qh = jnp.concatenate([q_ref[i, hs, :] for i in range(bb)], axis=0)
        qh = (qh * (SWA_HD ** -0.5)).astype(BF16)
        s = jnp.where(own, _dot_nt(qh, kcat) - slope_ref[kvh] * distf, MASK_NEG)
        sink = sink_ref[kvh]
        mx = jnp.maximum(jnp.max(s, axis=-1, keepdims=True), sink)
        p = jnp.exp(s - mx)
        den = jnp.sum(p, axis=-1, keepdims=True) + jnp.exp(sink - mx)
        oh = _dot(p.astype(BF16), vcat) / den
        for i in range(bb):
            o_ref[i, hs, :] = oh[SWA_GROUP * i:SWA_GROUP * (i + 1)]


def _swa_step(q3, k_old, v_old, u3, mk, mv, sinks, *, batch, bb, name):
    def per_row(x16):
        return jnp.tile(x16.reshape(SWA_KV_HEADS, 1, SWA_GROUP), (1, bb, 1)).reshape(
            SWA_KV_HEADS, bb * SWA_GROUP, 1)

    slopes = jnp.asarray([_alibi_slope(h) for h in range(SWA_HEADS)], F32)
    head_spec = pl.BlockSpec((SWA_KV_HEADS, bb * SWA_GROUP, 1), lambda b: (0, 0, 0))
    win_spec = pl.BlockSpec((bb, WINDOW, SWA_KV), lambda b: (b, 0, 0))
    meta_spec = pl.BlockSpec((bb, N_META, SWA_KV), lambda b: (b, 0, 0))
    return pl.pallas_call(
        functools.partial(_swa_step_kernel, bb=bb),
        grid=(batch // bb,),
        in_specs=[
            pl.BlockSpec((bb, SWA_HEADS, SWA_HD), lambda b: (b, 0, 0)),
            win_spec, win_spec,
            pl.BlockSpec((bb, 1, SWA_KV), lambda b: (b, 0, COL_SK // SWA_KV)),
            pl.BlockSpec((bb, 1, SWA_KV), lambda b: (b, 0, COL_SV // SWA_KV)),
            meta_spec, meta_spec, head_spec, head_spec,
        ],
        out_specs=[pl.BlockSpec((bb, SWA_HEADS, SWA_HD), lambda b: (b, 0, 0)), win_spec, win_spec],
        out_shape=[
            jax.ShapeDtypeStruct((batch, SWA_HEADS, SWA_HD), F32),
            jax.ShapeDtypeStruct((batch, WINDOW, SWA_KV), F32),
            jax.ShapeDtypeStruct((batch, WINDOW, SWA_KV), F32),
        ],
        compiler_params=pltpu.CompilerParams(dimension_semantics=("parallel",)),
        name=name,
    )(q3, k_old, v_old, u3, u3, mk, mv, per_row(sinks), per_row(slopes))


def _merge_kernel(h_ref, og_ref, gr_ref, os_ref, ga_ref, gb_ref,
                  hs_ref, ogs_ref, grs_ref, oss_ref, gas_ref, gbs_ref,
                  onorm_ref, gbias_ref, wg_ref, ws_ref, wo_ref, out_ref, outs_ref):
    _merge_rows(h_ref, og_ref, gr_ref, os_ref, ga_ref, gb_ref, onorm_ref, gbias_ref,
                wg_ref, ws_ref, wo_ref, out_ref)

    @pl.when(pl.program_id(0) == 0)
    def _():
        _merge_rows(hs_ref, ogs_ref, grs_ref, oss_ref, gas_ref, gbs_ref, onorm_ref, gbias_ref,
                    wg_ref, ws_ref, wo_ref, outs_ref)


def _merge_rows(h_ref, og_ref, gr_ref, os_ref, ga_ref, gb_ref, onorm_ref, gbias_ref,
                wg_ref, ws_ref, wo_ref, out_ref):
    og = og_ref[...]
    parts = []
    for h in range(GLA_HEADS):
        oh = og[:, GLA_DV * h:GLA_DV * (h + 1)]
        parts.append(oh * lax.rsqrt(jnp.mean(oh * oh, axis=-1, keepdims=True) + EPS))
    gr = gr_ref[...]
    of = jnp.concatenate(parts, axis=1) * onorm_ref[...] * (gr * jax.nn.sigmoid(gr))
    gla_b = _dot(of.astype(BF16), wg_ref[...])
    swa_b = _dot(os_ref[...].astype(BF16), ws_ref[...])
    gbias = gbias_ref[...]
    mix = (jax.nn.sigmoid(ga_ref[...] + gbias[:, :D_MODEL]) * gla_b
           + jax.nn.sigmoid(gb_ref[...] + gbias[:, D_MODEL:]) * swa_b)
    out_ref[...] = h_ref[...] + _dot(mix.astype(BF16), wo_ref[...])


def _merge(h, o_gla, u, o_swa, hs, o_gla_s, us, o_swa_s, onorm, gbias, w_gla_o, w_swa_o, w_out,
           *, tm, name):
    rows, rs = h.shape[0], o_gla_s.shape[0]
    const = dict(pipeline_mode=pl.Buffered(1))

    def row_specs(r, idx):
        return [
            pl.BlockSpec((r, D_MODEL), lambda i: (idx(i), 0)),
            pl.BlockSpec((r, GLA_V), lambda i: (idx(i), 0)),
            pl.BlockSpec((r, GLA_V), lambda i: (idx(i), COL_GR // GLA_V)),
            pl.BlockSpec((r, SWA_Q), lambda i: (idx(i), 0)),
            pl.BlockSpec((r, D_MODEL), lambda i: (idx(i), COL_GA // D_MODEL)),
            pl.BlockSpec((r, D_MODEL), lambda i: (idx(i), COL_GB // D_MODEL)),
        ]

    return pl.pallas_call(
        _merge_kernel,
        grid=(rows // tm,),
        in_specs=row_specs(tm, lambda i: i) + row_specs(rs, lambda i: 0) + [
            pl.BlockSpec((1, GLA_V), lambda i: (0, 0)),
            pl.BlockSpec((1, 2 * D_MODEL), lambda i: (0, 0)),
            pl.BlockSpec((GLA_V, D_MODEL), lambda i: (0, 0), **const),
            pl.BlockSpec((SWA_Q, D_MODEL), lambda i: (0, 0), **const),
            pl.BlockSpec((D_MODEL, D_MODEL), lambda i: (0, 0), **const),
        ],
        out_specs=[
            pl.BlockSpec((tm, D_MODEL), lambda i: (i, 0)),
            pl.BlockSpec((rs, D_MODEL), lambda i: (0, 0)),
        ],
        out_shape=[
            jax.ShapeDtypeStruct((rows, D_MODEL), F32),
            jax.ShapeDtypeStruct((rs, D_MODEL), F32),
        ],
        compiler_params=pltpu.CompilerParams(
            dimension_semantics=("arbitrary",), vmem_limit_bytes=56 * MIB),
        name=name,
    )(h, o_gla, u, o_swa, u, u, hs, o_gla_s, us, o_swa_s, us, us,
      onorm, gbias, w_gla_o, w_swa_o, w_out)


def kernel(x_prompt, x_sample, cache_meta_k, cache_meta_v, cache_win_k, cache_win_v, state_gla,
           meta_tokens, ffn1_norm, ffn1_w_in, ffn1_w_out, mix_norm, w_in, gla_a_up, gla_a_bias,
           gla_out_norm, w_gla_o, swa_sinks, w_swa_o, gate_bias, w_out, ffn2_norm, ffn2_w_in,
           ffn2_w_out, final_norm):
    B, S, _ = x_prompt.shape
    DB = x_sample.shape[0]
    assert x_sample.shape[1] == 1 and ffn1_norm.shape[0] == 1
    n_small = DB + N_META

    wgo, wso, wout = w_gla_o[0].astype(BF16), w_swa_o[0].astype(BF16), w_out[0].astype(BF16)
    g1, gm, g2 = ffn1_norm, mix_norm, ffn2_norm
    gf = final_norm.reshape(1, D_MODEL)
    a_up, a_bias = gla_a_up[0], gla_a_bias
    onorm, gbias = gla_out_norm, gate_bias
    sinks = swa_sinks[0]

    xs = jnp.concatenate([x_sample.reshape(DB, D_MODEL), meta_tokens.astype(F32)], axis=0)
    xp = x_prompt.reshape(B * S, D_MODEL)
    hp1, hs1 = _ffn(xp, xs, g1, ffn1_w_in[0], ffn1_w_out[0], gf, tm=1024, tf_first=256, tf=512,
                    final_norm=False, name="ffn1")
    up, gadp, us, gads = _inproj(hp1, hs1, gm, w_in[0].T, tm=2048, tn_first=256, tn=512, name="inproj")

    meta_blk = DB // N_META
    _, st_meta = _gla(us, gads, a_up, a_bias, jnp.zeros((GLA_HEADS, GLA_DK, GLA_DV), F32),
                      batch=1, seq=N_META, C=N_META, row_off=meta_blk, name="gla_meta")
    st_meta = st_meta[0]
    o_gla, st_fin = _gla(up, gadp, a_up, a_bias, st_meta, batch=B, seq=S, C=128, row_off=0, name="gla")
    o_swa = _swa(sinks, up, us, batch=B, seq=S, meta_row_block=meta_blk, name="swa")

    us3 = us.reshape(n_small, 1, N_MAIN)
    og_s, st_s = _gla_step(us3, gads.reshape(n_small, 1, GLA_RANK),
                           a_up, a_bias, state_gla[0], batch=DB, bb=4, name="gla_step")
    WB = cache_win_k.shape[2]
    assert WB == WINDOW
    os_s, new_k, new_v = _swa_step(
        us[:DB, COL_SQ:COL_SQ + SWA_Q].reshape(DB, SWA_HEADS, SWA_HD),
        cache_win_k[0].reshape(DB, WB, SWA_KV), cache_win_v[0].reshape(DB, WB, SWA_KV), us3,
        cache_meta_k[0].reshape(DB, N_META, SWA_KV), cache_meta_v[0].reshape(DB, N_META, SWA_KV),
        sinks, batch=DB, bb=8, name="swa_step")
    hp2, hs2 = _merge(hp1, o_gla, up, o_swa, hs1, og_s.reshape(DB, GLA_V), us, os_s.reshape(DB, SWA_Q),
                      onorm, gbias, wgo, wso, wout, tm=256, name="merge")
    y_prompt, y_sample = _ffn(hp2, hs2, g2, ffn2_w_in[0], ffn2_w_out[0], gf, tm=1024, tf_first=256, tf=512,
                              final_norm=True, name="ffn2")

    up3 = up.reshape(B, S, N_MAIN)
    kv_shape = (SWA_KV_HEADS, SWA_HD)
    p_meta_k = jnp.broadcast_to(us[DB:, COL_SK:COL_SK + SWA_KV].reshape(1, 1, N_META, *kv_shape),
                                (1, B, N_META, *kv_shape))
    p_meta_v = jnp.broadcast_to(us[DB:, COL_SV:COL_SV + SWA_KV].reshape(1, 1, N_META, *kv_shape),
                                (1, B, N_META, *kv_shape))
    p_win_k = up3[:, S - WINDOW:, COL_SK:COL_SK + SWA_KV].reshape(1, B, WINDOW, *kv_shape)
    p_win_v = up3[:, S - WINDOW:, COL_SV:COL_SV + SWA_KV].reshape(1, B, WINDOW, *kv_shape)
    return (y_prompt.reshape(B, S, D_MODEL), y_sample.reshape(DB, 1, D_MODEL),
            p_meta_k, p_meta_v, p_win_k, p_win_v, st_fin[None],
            new_k.reshape(1, DB, WB, *kv_shape), new_v.reshape(1, DB, WB, *kv_shape), st_s[None])
```

```python
import functools

import jax
import jax.numpy as jnp
from jax import lax
from jax.experimental import pallas as pl
from jax.experimental.pallas import tpu as pltpu

F32 = jnp.float32
BF16 = jnp.bfloat16

D_MODEL = 2048
N_META = 16
GLA_HEADS = 4
GLA_DK = 128
GLA_DV = 256
GLA_RANK = 16
GLA_TAU = 16.0
SWA_HEADS = 16
SWA_KV_HEADS = 4
SWA_GROUP = SWA_HEADS // SWA_KV_HEADS
SWA_HD = 64
WINDOW = 128
D_FF = 5632
EPS = 1e-6

GLA_QK = GLA_HEADS * GLA_DK
GLA_V = GLA_HEADS * GLA_DV
SWA_Q = SWA_HEADS * SWA_HD
SWA_KV = SWA_KV_HEADS * SWA_HD

COL_GA = 0
COL_GB = COL_GA + D_MODEL
COL_GQ = COL_GB + D_MODEL
COL_GK = COL_GQ + GLA_QK
COL_GV = COL_GK + GLA_QK
COL_GR = COL_GV + GLA_V
COL_SQ = COL_GR + GLA_V
COL_SK = COL_SQ + SWA_Q
COL_SV = COL_SK + SWA_KV
N_MAIN = COL_SV + SWA_KV
N_GRP_A = COL_GQ
N_GRP_B = COL_SQ - COL_GQ
N_GRP_C = N_MAIN - COL_SQ

GLA_SUB = 8
MASK_NEG = -1e30
MIB = 1024 * 1024
LANE = 128
BF16_ROWS = 16
ROW_CHUNK = 256
COL_CHUNK = 512

SWA_QB = 64
SWA_SPAN = WINDOW + SWA_QB
SWA_KT = 256
SWA_SINK_COL = SWA_SPAN + N_META


def _rms_scale(x):
    return x * lax.rsqrt(jnp.mean(x * x, axis=-1, keepdims=True) + EPS)


def _log_sigmoid(x):
    return jnp.minimum(x, 0.0) - jnp.log(1.0 + jnp.exp(-jnp.abs(x)))


def _dot(a, b):
    return jnp.dot(a, b, preferred_element_type=F32)


def _dot_nt(a, b):
    return lax.dot_general(a, b, (((1,), (1,)), ((), ())), preferred_element_type=F32)


def _row_to_col(row, n):
    r = lax.broadcasted_iota(jnp.int32, (n, n), 0)
    c = lax.broadcasted_iota(jnp.int32, (n, n), 1)
    return jnp.sum(jnp.where(r == c, jnp.broadcast_to(row, (n, n)), 0.0), axis=1, keepdims=True)


def _for_row_chunks(rows, fn):
    chunk = ROW_CHUNK if rows % ROW_CHUNK == 0 else rows

    def body(i, carry):
        fn(pl.ds(pl.multiple_of(i * chunk, chunk), chunk))
        return carry

    lax.fori_loop(0, rows // chunk, body, 0)


def _row_offsets(refs):
    offs, total = [], 0
    for r in refs:
        offs.append(total)
        total += r.shape[0]
    return offs


def _ffn_rows(j, n_ff, x_refs, o_refs, xn_ref, g_ref, fg_ref, wa, wb, wo, final_norm):
    offs = _row_offsets(x_refs)

    @pl.when(j == 0)
    def _():
        for x_ref, o_ref, off in zip(x_refs, o_refs, offs):
            def norm_rows(sl, x_ref=x_ref, o_ref=o_ref, off=off):
                x = x_ref[sl, :]
                dst = pl.ds(pl.multiple_of(off + sl.start, BF16_ROWS), sl.size)
                xn_ref[dst, :] = (_rms_scale(x) * g_ref[...]).astype(BF16)
                o_ref[sl, :] = x

            _for_row_chunks(x_ref.shape[0], norm_rows)

    xn = xn_ref[...]
    a = _dot(xn, wa)
    b = _dot(xn, wb)
    h = ((0.5 * a) * jax.nn.sigmoid(a) * b).astype(BF16)
    for c in range(D_MODEL // COL_CHUNK):
        cs = slice(COL_CHUNK * c, COL_CHUNK * (c + 1))
        r = _dot(h, wo[:, cs])
        for o_ref, off in zip(o_refs, offs):
            o_ref[:, cs] += r[off:off + o_ref.shape[0]]

    if final_norm:
        @pl.when(j == n_ff - 1)
        def _():
            for o_ref in o_refs:
                def final_rows(sl, o_ref=o_ref):
                    o_ref[sl, :] = _rms_scale(o_ref[sl, :]) * fg_ref[...]

                _for_row_chunks(o_ref.shape[0], final_rows)


def _ffn_first_kernel(x_ref, xs_ref, g_ref, wa_ref, wb_ref, wo_ref, fg_ref,
                      o_ref, os_ref, wa16_ref, wb16_ref, wo16_ref, xn_ref, *, n_ff, final_norm):
    wa = wa_ref[...].astype(BF16)
    wb = wb_ref[...].astype(BF16)
    wo = wo_ref[...].astype(BF16)
    wa16_ref[...] = wa
    wb16_ref[...] = wb
    wo16_ref[...] = wo
    _ffn_rows(pl.program_id(0), n_ff, [x_ref, xs_ref], [o_ref, os_ref], xn_ref, g_ref, fg_ref,
              wa, wb, wo, final_norm)


def _ffn_rest_kernel(x_ref, g_ref, wa_ref, wb_ref, wo_ref, fg_ref, o_first_ref, o_ref, xn_ref,
                     *, n_ff, final_norm):
    del o_first_ref
    _ffn_rows(pl.program_id(1), n_ff, [x_ref], [o_ref], xn_ref, g_ref, fg_ref,
              wa_ref[...], wb_ref[...], wo_ref[...], final_norm)


def _ffn(x, xs, gain, w_in, w_out, final_gain, *, tm, tf_first, tf, final_norm, name):
    rows, rows_s = x.shape[0], xs.shape[0]
    n1 = D_FF // tf_first
    o, os_, wa16, wb16, wo16 = pl.pallas_call(
        functools.partial(_ffn_first_kernel, n_ff=n1, final_norm=final_norm),
        grid=(n1,),
        in_specs=[
            pl.BlockSpec((tm, D_MODEL), lambda j: (0, 0), pipeline_mode=pl.Buffered(1)),
            pl.BlockSpec((rows_s, D_MODEL), lambda j: (0, 0), pipeline_mode=pl.Buffered(1)),
            pl.BlockSpec((1, D_MODEL), lambda j: (0, 0)),
            pl.BlockSpec((D_MODEL, tf_first), lambda j: (0, j)),
            pl.BlockSpec((D_MODEL, tf_first), lambda j: (0, j + n1)),
            pl.BlockSpec((tf_first, D_MODEL), lambda j: (j, 0)),
            pl.BlockSpec((1, D_MODEL), lambda j: (0, 0)),
        ],
        out_specs=[
            pl.BlockSpec((tm, D_MODEL), lambda j: (0, 0)),
            pl.BlockSpec((rows_s, D_MODEL), lambda j: (0, 0)),
            pl.BlockSpec((D_MODEL, tf_first), lambda j: (0, j)),
            pl.BlockSpec((D_MODEL, tf_first), lambda j: (0, j)),
            pl.BlockSpec((tf_first, D_MODEL), lambda j: (j, 0)),
        ],
        out_shape=[
            jax.ShapeDtypeStruct((rows, D_MODEL), F32),
            jax.ShapeDtypeStruct((rows_s, D_MODEL), F32),
            jax.ShapeDtypeStruct((D_MODEL, D_FF), BF16),
            jax.ShapeDtypeStruct((D_MODEL, D_FF), BF16),
            jax.ShapeDtypeStruct((D_FF, D_MODEL), BF16),
        ],
        scratch_shapes=[pltpu.VMEM((tm + rows_s, D_MODEL), BF16)],
        compiler_params=pltpu.CompilerParams(
            dimension_semantics=("arbitrary",), vmem_limit_bytes=58 * MIB),
        name=name + "_first",
    )(x, xs, gain, w_in, w_in, w_out, final_gain)

    n2 = D_FF // tf
    o = pl.pallas_call(
        functools.partial(_ffn_rest_kernel, n_ff=n2, final_norm=final_norm),
        grid=(rows // tm - 1, n2),
        in_specs=[
            pl.BlockSpec((tm, D_MODEL), lambda i, j: (i + 1, 0)),
            pl.BlockSpec((1, D_MODEL), lambda i, j: (0, 0)),
            pl.BlockSpec((D_MODEL, tf), lambda i, j: (0, j)),
            pl.BlockSpec((D_MODEL, tf), lambda i, j: (0, j)),
            pl.BlockSpec((tf, D_MODEL), lambda i, j: (j, 0)),
            pl.BlockSpec((1, D_MODEL), lambda i, j: (0, 0)),
            pl.BlockSpec(memory_space=pl.ANY),
        ],
        out_specs=pl.BlockSpec((tm, D_MODEL), lambda i, j: (i + 1, 0)),
        out_shape=jax.ShapeDtypeStruct((rows, D_MODEL), F32),
        input_output_aliases={6: 0},
        scratch_shapes=[pltpu.VMEM((tm, D_MODEL), BF16)],
        compiler_params=pltpu.CompilerParams(
            dimension_semantics=("parallel", "arbitrary"), vmem_limit_bytes=58 * MIB),
        name=name + "_rest",
    )(x, gain, wa16, wb16, wo16, final_gain, o)
    return o, os_


def _inproj_rows(j, x_refs, xn_ref, g_ref, w, wg, u_refs, gad_refs):
    offs = _row_offsets(x_refs)

    @pl.when(j == 0)
    def _():
        for x_ref, off in zip(x_refs, offs):
            def norm_rows(sl, x_ref=x_ref, off=off):
                dst = pl.ds(pl.multiple_of(off + sl.start, BF16_ROWS), sl.size)
                xn_ref[dst, :] = (_rms_scale(x_ref[sl, :]) * g_ref[...]).astype(BF16)

            _for_row_chunks(x_ref.shape[0], norm_rows)
        gad = _dot_nt(xn_ref[...], wg)
        for gad_ref, off in zip(gad_refs, offs):
            gad_ref[...] = gad[off:off + gad_ref.shape[0]]

    u = _dot(xn_ref[...], w)
    for u_ref, off in zip(u_refs, offs):
        u_ref[...] = u[off:off + u_ref.shape[0]]


def _inproj_first_kernel(x_ref, xs_ref, g_ref, wm_ref, wx_ref, wg_ref,
                         u_ref, gad_ref, us_ref, gads_ref, w16_ref, wg16_ref,
                         xn_ref, *, na, nb, tn):
    j = pl.program_id(0)
    shifted = (j < na) | (j >= na + nb)

    @pl.when(shifted)
    def _():
        tall = jnp.concatenate([wm_ref[...], wx_ref[...]], axis=0)
        w16_ref[...] = tall[GLA_RANK:GLA_RANK + tn].T.astype(BF16)

    @pl.when(jnp.logical_not(shifted))
    def _():
        w16_ref[...] = wm_ref[...].T.astype(BF16)

    w = w16_ref[...]
    wg = wg_ref[...].astype(BF16)

    @pl.when(j == 0)
    def _():
        wg16_ref[...] = wg

    _inproj_rows(j, [x_ref, xs_ref], xn_ref, g_ref, w, wg, [u_ref, us_ref], [gad_ref, gads_ref])


def _inproj_rest_kernel(x_ref, g_ref, w_ref, wg_ref, u_first_ref, gad_first_ref, u_ref, gad_ref, xn_ref):
    del u_first_ref, gad_first_ref
    _inproj_rows(pl.program_id(1), [x_ref], xn_ref, g_ref, w_ref[...], wg_ref[...], [u_ref], [gad_ref])


def _inproj(x, xs, gain, w_in, *, tm, tn_first, tn, name):
    rows, rows_s = x.shape[0], xs.shape[0]
    na, nb, ncc = N_GRP_A // tn_first, N_GRP_B // tn_first, N_GRP_C // tn_first
    nj = na + nb + ncc
    c_gad = N_GRP_B
    c_sq = c_gad + GLA_RANK
    c_ga = c_sq + N_GRP_C
    assert c_gad % tn_first == 0 and (c_ga - GLA_RANK) % tn_first == 0
    assert tn_first % LANE == 0 and c_gad % GLA_RANK == 0

    def window(j):
        return jnp.where(j < na, (c_ga - GLA_RANK) // tn_first + j,
                         jnp.where(j < na + nb, j - na, c_gad // tn_first + j - na - nb))

    u, gad, us, gads, w16, wg16 = pl.pallas_call(
        functools.partial(_inproj_first_kernel, na=na, nb=nb, tn=tn_first),
        grid=(nj,),
        in_specs=[
            pl.BlockSpec((tm, D_MODEL), lambda j: (0, 0), pipeline_mode=pl.Buffered(1)),
            pl.BlockSpec((rows_s, D_MODEL), lambda j: (0, 0), pipeline_mode=pl.Buffered(1)),
            pl.BlockSpec((1, D_MODEL), lambda j: (0, 0)),
            pl.BlockSpec((tn_first, D_MODEL), lambda j: (window(j), 0)),
            pl.BlockSpec((GLA_RANK, D_MODEL), lambda j: ((window(j) + 1) * (tn_first // GLA_RANK), 0)),
            pl.BlockSpec((GLA_RANK, D_MODEL), lambda j: (c_gad // GLA_RANK, 0)),
        ],
        out_specs=[
            pl.BlockSpec((tm, tn_first), lambda j: (0, j)),
            pl.BlockSpec((tm, GLA_RANK), lambda j: (0, 0)),
            pl.BlockSpec((rows_s, tn_first), lambda j: (0, j)),
            pl.BlockSpec((rows_s, GLA_RANK), lambda j: (0, 0)),
            pl.BlockSpec((D_MODEL, tn_first), lambda j: (0, j)),
            pl.BlockSpec((GLA_RANK, D_MODEL), lambda j: (0, 0)),
        ],
        out_shape=[
            jax.ShapeDtypeStruct((rows, N_MAIN), F32),
            jax.ShapeDtypeStruct((rows, GLA_RANK), F32),
            jax.ShapeDtypeStruct((rows_s, N_MAIN), F32),
            jax.ShapeDtypeStruct((rows_s, GLA_RANK), F32),
            jax.ShapeDtypeStruct((D_MODEL, N_MAIN), BF16),
            jax.ShapeDtypeStruct((GLA_RANK, D_MODEL), BF16),
        ],
        scratch_shapes=[pltpu.VMEM((tm + rows_s, D_MODEL), BF16)],
        compiler_params=pltpu.CompilerParams(
            dimension_semantics=("arbitrary",), vmem_limit_bytes=56 * MIB),
        name=name + "_first",
    )(x, xs, gain, w_in, w_in, w_in)

    u, gad = pl.pallas_call(
        _inproj_rest_kernel,
        grid=(rows // tm - 1, N_MAIN // tn),
        in_specs=[
            pl.BlockSpec((tm, D_MODEL), lambda i, j: (i + 1, 0), pipeline_mode=pl.Buffered(1)),
            pl.BlockSpec((1, D_MODEL), lambda i, j: (0, 0)),
            pl.BlockSpec((D_MODEL, tn), lambda i, j: (0, j)),
            pl.BlockSpec((GLA_RANK, D_MODEL), lambda i, j: (0, 0)),
            pl.BlockSpec(memory_space=pl.ANY),
            pl.BlockSpec(memory_space=pl.ANY),
        ],
        out_specs=[
            pl.BlockSpec((tm, tn), lambda i, j: (i + 1, j)),
            pl.BlockSpec((tm, GLA_RANK), lambda i, j: (i + 1, 0)),
        ],
        out_shape=[
            jax.ShapeDtypeStruct((rows, N_MAIN), F32),
            jax.ShapeDtypeStruct((rows, GLA_RANK), F32),
        ],
        input_output_aliases={4: 0, 5: 1},
        scratch_shapes=[pltpu.VMEM((tm, D_MODEL), BF16)],
        compiler_params=pltpu.CompilerParams(
            dimension_semantics=("parallel", "arbitrary"), vmem_limit_bytes=56 * MIB),
        name=name + "_rest",
    )(x, gain, w16, wg16, u, gad)
    return u, gad, us, gads


def _gla_head(q, k, v, b, state, *, C):
    o = _dot((q * jnp.exp(b)).astype(BF16), state.astype(BF16))

    ri = lax.broadcasted_iota(jnp.int32, (C, C), 0)
    ci = lax.broadcasted_iota(jnp.int32, (C, C), 1)
    rowd = lax.broadcasted_iota(jnp.int32, (C, GLA_DK), 0)
    att = None
    m = C // 2
    while m >= GLA_SUB:
        nblk = C // (2 * m)
        pieces = [jnp.broadcast_to(b[i * 2 * m + m - 1:i * 2 * m + m, :], (2 * m, GLA_DK))
                  for i in range(nblk)]
        bref = pieces[0] if nblk == 1 else jnp.concatenate(pieces, axis=0)
        e = jnp.exp(-jnp.abs(b - bref))
        second = (rowd & m) != 0
        ql = jnp.where(second, q * e, 0.0).astype(BF16)
        kl = jnp.where(second, 0.0, k * e).astype(BF16)
        a = _dot_nt(ql, kl)
        if nblk > 1:
            a = jnp.where((ri ^ ci) < 2 * m, a, 0.0)
        att = a if att is None else att + a
        m //= 2

    sub_row = lax.broadcasted_iota(jnp.int32, (GLA_SUB, GLA_DK), 0)
    sub_col = lax.broadcasted_iota(jnp.int32, (GLA_SUB, C), 1)
    diag = []
    for i in range(C // GLA_SUB):
        sl = slice(GLA_SUB * i, GLA_SUB * (i + 1))
        bb, qq, kk = b[sl], q[sl], k[sl]
        blk = jnp.zeros((GLA_SUB, C), F32)
        for s in range(GLA_SUB):
            d = jnp.where(sub_row >= s, bb - bb[s:s + 1], MASK_NEG)
            w = jnp.sum(qq * jnp.exp(d) * kk[s:s + 1], axis=-1, keepdims=True)
            blk = jnp.where(sub_col == GLA_SUB * i + s, w, blk)
        diag.append(blk)
    diag = diag[0] if len(diag) == 1 else jnp.concatenate(diag, axis=0)
    att = diag if att is None else att + diag
    o = o + _dot(att.astype(BF16), v.astype(BF16))

    bl = b[C - 1:C, :]
    kd = k * jnp.exp(bl - b)
    vp = v
    if C < GLA_DK:
        kd = jnp.concatenate([kd, jnp.zeros((GLA_DK - C, GLA_DK), F32)], axis=0)
        vp = jnp.concatenate([v, jnp.zeros((GLA_DK - C, GLA_DV), F32)], axis=0)
    new_state = state * _row_to_col(jnp.exp(bl), GLA_DK) + _dot(kd.T.astype(BF16), vp.astype(BF16))
    return o, new_state


def _gla_kernel(q_ref, k_ref, v_ref, gad_ref, aup_ref, ab_ref, s0_ref, o_ref, sfin_ref, s_ref, *, C, nc):
    n = pl.program_id(1)

    @pl.when(n == 0)
    def _():
        s_ref[...] = s0_ref[...]

    logits = _dot(gad_ref[...].astype(BF16), aup_ref[...].astype(BF16)) + ab_ref[...]
    g = _log_sigmoid(logits) / GLA_TAU

    ri = lax.broadcasted_iota(jnp.int32, (C, C), 0)
    ci = lax.broadcasted_iota(jnp.int32, (C, C), 1)
    tri = jnp.where(ri >= ci, 1.0, 0.0).astype(BF16)
    g_hi = g.astype(BF16)
    r1 = g - g_hi.astype(F32)
    g_mid = r1.astype(BF16)
    g_lo = (r1 - g_mid.astype(F32)).astype(BF16)
    b_all = _dot(tri, g_hi) + _dot(tri, g_mid) + _dot(tri, g_lo)

    for h in range(GLA_HEADS):
        ks = slice(GLA_DK * h, GLA_DK * (h + 1))
        vs = slice(GLA_DV * h, GLA_DV * (h + 1))
        o, new_state = _gla_head(q_ref[:, ks] * (GLA_DK ** -0.5), k_ref[:, ks], v_ref[:, vs],
                                 b_all[:, ks], s_ref[h], C=C)
        o_ref[:, vs] = o
        s_ref[h] = new_state

    @pl.when(n == nc - 1)
    def _():
        sfin_ref[0] = s_ref[...]


def _gla(u, gad, a_up, a_bias, s0, *, batch, seq, C, row_off, name):
    nc = seq // C
    return pl.pallas_call(
        functools.partial(_gla_kernel, C=C, nc=nc),
        grid=(batch, nc),
        in_specs=[
            pl.BlockSpec((C, GLA_QK), lambda b, n: (row_off + b * nc + n, COL_GQ // GLA_QK)),
            pl.BlockSpec((C, GLA_QK), lambda b, n: (row_off + b * nc + n, COL_GK // GLA_QK)),
            pl.BlockSpec((C, GLA_V), lambda b, n: (row_off + b * nc + n, COL_GV // GLA_V)),
            pl.BlockSpec((C, GLA_RANK), lambda b, n: (row_off + b * nc + n, 0)),
            pl.BlockSpec((GLA_RANK, GLA_QK), lambda b, n: (0, 0)),
            pl.BlockSpec((1, GLA_QK), lambda b, n: (0, 0)),
            pl.BlockSpec((GLA_HEADS, GLA_DK, GLA_DV), lambda b, n: (0, 0, 0)),
        ],
        out_specs=[
            pl.BlockSpec((C, GLA_V), lambda b, n: (b * nc + n, 0)),
            pl.BlockSpec((1, GLA_HEADS, GLA_DK, GLA_DV), lambda b, n: (b, 0, 0, 0)),
        ],
        out_shape=[
            jax.ShapeDtypeStruct((batch * seq, GLA_V), F32),
            jax.ShapeDtypeStruct((batch, GLA_HEADS, GLA_DK, GLA_DV), F32),
        ],
        scratch_shapes=[pltpu.VMEM((GLA_HEADS, GLA_DK, GLA_DV), F32)],
        compiler_params=pltpu.CompilerParams(
            dimension_semantics=("parallel", "arbitrary"), vmem_limit_bytes=32 * MIB),
        name=name,
    )(u, u, u, gad, a_up, a_bias, s0)


def _gla_step_kernel(q_ref, k_ref, v_ref, gad_ref, aup_ref, ab_ref, s_ref, o_ref, sn_ref, *, bb):
    gl = jnp.concatenate([gad_ref[i] for i in range(bb)] + [jnp.zeros((8 - bb, GLA_RANK), F32)], axis=0)
    logits = _dot(gl.astype(BF16), aup_ref[...].astype(BF16)) + ab_ref[...]
    dec_all = jnp.exp(_log_sigmoid(logits) / GLA_TAU)
    for i in range(bb):
        q = q_ref[i] * (GLA_DK ** -0.5)
        k = k_ref[i]
        v = v_ref[i]
        dec = dec_all[i:i + 1]
        outs = []
        for h in range(GLA_HEADS):
            sl = slice(GLA_DK * h, GLA_DK * (h + 1))
            new_state = (s_ref[i, h] * _row_to_col(dec[:, sl], GLA_DK)
                         + _row_to_col(k[:, sl], GLA_DK) * v[:, GLA_DV * h:GLA_DV * (h + 1)])
            sn_ref[i, h] = new_state
            outs.append(jnp.sum(_row_to_col(q[:, sl], GLA_DK) * new_state, axis=0, keepdims=True))
        o_ref[i] = jnp.concatenate(outs, axis=1)


def _gla_step(u3, gad3, a_up, a_bias, state, *, batch, bb, name):
    return pl.pallas_call(
        functools.partial(_gla_step_kernel, bb=bb),
        grid=(batch // bb,),
        in_specs=[
            pl.BlockSpec((bb, 1, GLA_QK), lambda b: (b, 0, COL_GQ // GLA_QK)),
            pl.BlockSpec((bb, 1, GLA_QK), lambda b: (b, 0, COL_GK // GLA_QK)),
            pl.BlockSpec((bb, 1, GLA_V), lambda b: (b, 0, COL_GV // GLA_V)),
            pl.BlockSpec((bb, 1, GLA_RANK), lambda b: (b, 0, 0)),
            pl.BlockSpec((GLA_RANK, GLA_QK), lambda b: (0, 0)),
            pl.BlockSpec((1, GLA_QK), lambda b: (0, 0)),
            pl.BlockSpec((bb, GLA_HEADS, GLA_DK, GLA_DV), lambda b: (b, 0, 0, 0)),
        ],
        out_specs=[
            pl.BlockSpec((bb, 1, GLA_V), lambda b: (b, 0, 0)),
            pl.BlockSpec((bb, GLA_HEADS, GLA_DK, GLA_DV), lambda b: (b, 0, 0, 0)),
        ],
        out_shape=[
            jax.ShapeDtypeStruct((batch, 1, GLA_V), F32),
            jax.ShapeDtypeStruct((batch, GLA_HEADS, GLA_DK, GLA_DV), F32),
        ],
        compiler_params=pltpu.CompilerParams(dimension_semantics=("parallel",)),
        name=name,
    )(u3, u3, u3, gad3, a_up, a_bias, state)


def _alibi_slope(head):
    return 2.0 ** (-8.0 * (head + 1) / SWA_HEADS)


def _swa_kernel(sink_ref, q_ref, kc_ref, kp_ref, vc_ref, vp_ref, mk_ref, mv_ref, o_ref, bias_ref):
    n = pl.program_id(1)
    rows = SWA_GROUP * SWA_QB
    nsub = WINDOW // SWA_QB

    @pl.when(n <= 1)
    def _():
        r = lax.broadcasted_iota(jnp.int32, (SWA_QB, SWA_KT), 0)
        c = lax.broadcasted_iota(jnp.int32, (SWA_QB, SWA_KT), 1)
        dist = r + WINDOW - c
        distf = dist.astype(F32)
        band = (dist >= 0) & (dist < WINDOW) & (c < SWA_SPAN)
        for a in range(nsub):
            ok = band & ((c >= WINDOW - SWA_QB * a) | (n > 0))
            for head in range(SWA_HEADS):
                val = jnp.where(ok, -_alibi_slope(head) * distf, MASK_NEG)
                val = jnp.where((c >= SWA_SPAN) & (c < SWA_SINK_COL), 0.0, val)
                val = jnp.where(c == SWA_SINK_COL, sink_ref[head], val)
                tile = (head // SWA_GROUP) * nsub + a
                bias_ref[pl.ds(tile * rows + (head % SWA_GROUP) * SWA_QB, SWA_QB), :] = val

    q = q_ref[...] * (SWA_HD ** -0.5)
    pad = jnp.zeros((SWA_KT - SWA_SPAN - N_META, SWA_HD), BF16)
    scores, values = [], []
    for kvh in range(SWA_KV_HEADS):
        cs = slice(SWA_HD * kvh, SWA_HD * (kvh + 1))
        kwin = jnp.concatenate([kp_ref[:, cs], kc_ref[:, cs]], axis=0).astype(BF16)
        vwin = jnp.concatenate([vp_ref[:, cs], vc_ref[:, cs]], axis=0).astype(BF16)
        km = mk_ref[:, cs].astype(BF16)
        vm = mv_ref[:, cs].astype(BF16)
        for a in range(nsub):
            qs = slice(SWA_QB * a, SWA_QB * (a + 1))
            kt = jnp.concatenate([kwin[SWA_QB * a:SWA_QB * a + SWA_SPAN], km, pad], axis=0)
            values.append(jnp.concatenate([vwin[SWA_QB * a:SWA_QB * a + SWA_SPAN], vm, pad], axis=0))
            qg = jnp.concatenate(
                [q[qs, SWA_HD * (kvh * SWA_GROUP + grp):SWA_HD * (kvh * SWA_GROUP + grp + 1)]
                 for grp in range(SWA_GROUP)], axis=0).astype(BF16)
            scores.append(_dot_nt(qg, kt))
    s = jnp.concatenate(scores, axis=0) + bias_ref[...]
    p = jnp.exp(s - jnp.max(s, axis=-1, keepdims=True))
    inv = 1.0 / jnp.sum(p, axis=-1, keepdims=True)
    p = p.astype(BF16)
    for kvh in range(SWA_KV_HEADS):
        for a in range(nsub):
            tile = kvh * nsub + a
            ts = slice(rows * tile, rows * (tile + 1))
            o = _dot(p[ts], values[tile]) * inv[ts]
            for grp in range(SWA_GROUP):
                head = kvh * SWA_GROUP + grp
                o_ref[SWA_QB * a:SWA_QB * (a + 1), SWA_HD * head:SWA_HD * (head + 1)] = (
                    o[SWA_QB * grp:SWA_QB * (grp + 1)])


def _swa(sinks, u, u_small, *, batch, seq, meta_row_block, name):
    nb = seq // WINDOW
    kcol, vcol = COL_SK // SWA_KV, COL_SV // SWA_KV
    return pl.pallas_call(
        _swa_kernel,
        grid=(batch, nb),
        in_specs=[
            pl.BlockSpec(memory_space=pltpu.SMEM),
            pl.BlockSpec((WINDOW, SWA_Q), lambda b, n: (b * nb + n, COL_SQ // SWA_Q)),
            pl.BlockSpec((WINDOW, SWA_KV), lambda b, n: (b * nb + n, kcol)),
            pl.BlockSpec((WINDOW, SWA_KV), lambda b, n: (b * nb + jnp.maximum(n - 1, 0), kcol)),
            pl.BlockSpec((WINDOW, SWA_KV), lambda b, n: (b * nb + n, vcol)),
            pl.BlockSpec((WINDOW, SWA_KV), lambda b, n: (b * nb + jnp.maximum(n - 1, 0), vcol)),
            pl.BlockSpec((N_META, SWA_KV), lambda b, n: (meta_row_block, kcol)),
            pl.BlockSpec((N_META, SWA_KV), lambda b, n: (meta_row_block, vcol)),
        ],
        out_specs=pl.BlockSpec((WINDOW, SWA_Q), lambda b, n: (b * nb + n, 0)),
        out_shape=jax.ShapeDtypeStruct((batch * seq, SWA_Q), F32),
        scratch_shapes=[pltpu.VMEM((SWA_HEADS * WINDOW, SWA_KT), F32)],
        compiler_params=pltpu.CompilerParams(
            dimension_semantics=("arbitrary", "arbitrary"), vmem_limit_bytes=32 * MIB),
        name=name,
    )(sinks, u, u, u, u, u, u_small, u_small)


def _swa_step_kernel(q_ref, kold_ref, vold_ref, knew_ref, vnew_ref, mk_ref, mv_ref, sink_ref, slope_ref,
                     o_ref, nk_ref, nv_ref, *, bb):
    rows = bb * SWA_GROUP
    wcols = bb * WINDOW
    ncol = wcols + bb * N_META
    win_shift, meta_shift, grp_shift = WINDOW.bit_length() - 1, N_META.bit_length() - 1, SWA_GROUP.bit_length() - 1
    for i in range(bb):
        nk_ref[i] = jnp.concatenate([kold_ref[i, 1:, :], knew_ref[i]], axis=0)
        nv_ref[i] = jnp.concatenate([vold_ref[i, 1:, :], vnew_ref[i]], axis=0)

    r = lax.broadcasted_iota(jnp.int32, (rows, ncol), 0)
    c = lax.broadcasted_iota(jnp.int32, (rows, ncol), 1)
    in_win = c < wcols
    col_batch = jnp.where(in_win, c >> win_shift, (c - wcols) >> meta_shift)
    own = col_batch == (r >> grp_shift)
    distf = jnp.where(in_win, WINDOW - 1 - (c & (WINDOW - 1)), 0).astype(F32)
    for kvh in range(SWA_KV_HEADS):
        cs = slice(SWA_HD * kvh, SWA_HD * (kvh + 1))
        hs = slice(SWA_GROUP * kvh, SWA_GROUP * (kvh + 1))
        kcat = jnp.concatenate([nk_ref[i, :, cs] for i in range(bb)]
                               + [mk_ref[i, :, cs] for i in range(bb)], axis=0).astype(BF16)
        vcat = jnp.concatenate([nv_ref[i, :, cs] for i in range(bb)]
                               + [mv_ref[i, :, cs] for i in range(bb)], axis=0).astype(BF16)
        qh = jnp.concatenate([q_ref[i, hs, :] for i in range(bb)], axis=0)
        qh = (qh * (SWA_HD ** -0.5)).astype(BF16)
        s = jnp.where(own, _dot_nt(qh, kcat) - slope_ref[kvh] * distf, MASK_NEG)
        sink = sink_ref[kvh]
        mx = jnp.maximum(jnp.max(s, axis=-1, keepdims=True), sink)
        p = jnp.exp(s - mx)
        den = jnp.sum(p, axis=-1, keepdims=True) + jnp.exp(sink - mx)
        oh = _dot(p.astype(BF16), vcat) / den
        for i in range(bb):
            o_ref[i, hs, :] = oh[SWA_GROUP * i:SWA_GROUP * (i + 1)]


def _swa_step(q3, k_old, v_old, u3, mk, mv, sinks, *, batch, bb, name):
    def per_row(x16):
        return jnp.tile(x16.reshape(SWA_KV_HEADS, 1, SWA_GROUP), (1, bb, 1)).reshape(
            SWA_KV_HEADS, bb * SWA_GROUP, 1)

    slopes = jnp.asarray([_alibi_slope(h) for h in range(SWA_HEADS)], F32)
    head_spec = pl.BlockSpec((SWA_KV_HEADS, bb * SWA_GROUP, 1), lambda b: (0, 0, 0))
    win_spec = pl.BlockSpec((bb, WINDOW, SWA_KV), lambda b: (b, 0, 0))
    meta_spec = pl.BlockSpec((bb, N_META, SWA_KV), lambda b: (b, 0, 0))
    return pl.pallas_call(
        functools.partial(_swa_step_kernel, bb=bb),
        grid=(batch // bb,),
        in_specs=[
            pl.BlockSpec((bb, SWA_HEADS, SWA_HD), lambda b: (b, 0, 0)),
            win_spec, win_spec,
            pl.BlockSpec((bb, 1, SWA_KV), lambda b: (b, 0, COL_SK // SWA_KV)),
            pl.BlockSpec((bb, 1, SWA_KV), lambda b: (b, 0, COL_SV // SWA_KV)),
            meta_spec, meta_spec, head_spec, head_spec,
        ],
        out_specs=[pl.BlockSpec((bb, SWA_HEADS, SWA_HD), lambda b: (b, 0, 0)), win_spec, win_spec],
        out_shape=[
            jax.ShapeDtypeStruct((batch, SWA_HEADS, SWA_HD), F32),
            jax.ShapeDtypeStruct((batch, WINDOW, SWA_KV), F32),
            jax.ShapeDtypeStruct((batch, WINDOW, SWA_KV), F32),
        ],
        compiler_params=pltpu.CompilerParams(dimension_semantics=("parallel",)),
        name=name,
    )(q3, k_old, v_old, u3, u3, mk, mv, per_row(sinks), per_row(slopes))


def _merge_kernel(h_ref, og_ref, gr_ref, os_ref, ga_ref, gb_ref,
                  hs_ref, ogs_ref, grs_ref, oss_ref, gas_ref, gbs_ref,
                  onorm_ref, gbias_ref, wg_ref, ws_ref, wo_ref, out_ref, outs_ref):
    _merge_rows(h_ref, og_ref, gr_ref, os_ref, ga_ref, gb_ref, onorm_ref, gbias_ref,
                wg_ref, ws_ref, wo_ref, out_ref)

    @pl.when(pl.program_id(0) == 0)
    def _():
        _merge_rows(hs_ref, ogs_ref, grs_ref, oss_ref, gas_ref, gbs_ref, onorm_ref, gbias_ref,
                    wg_ref, ws_ref, wo_ref, outs_ref)


def _merge_rows(h_ref, og_ref, gr_ref, os_ref, ga_ref, gb_ref, onorm_ref, gbias_ref,
                wg_ref, ws_ref, wo_ref, out_ref):
    og = og_ref[...]
    parts = []
    for h in range(GLA_HEADS):
        oh = og[:, GLA_DV * h:GLA_DV * (h + 1)]
        parts.append(oh * lax.rsqrt(jnp.mean(oh * oh, axis=-1, keepdims=True) + EPS))
    gr = gr_ref[...]
    of = jnp.concatenate(parts, axis=1) * onorm_ref[...] * (gr * jax.nn.sigmoid(gr))
    gla_b = _dot(of.astype(BF16), wg_ref[...])
    swa_b = _dot(os_ref[...].astype(BF16), ws_ref[...])
    gbias = gbias_ref[...]
    mix = (jax.nn.sigmoid(ga_ref[...] + gbias[:, :D_MODEL]) * gla_b
           + jax.nn.sigmoid(gb_ref[...] + gbias[:, D_MODEL:]) * swa_b)
    out_ref[...] = h_ref[...] + _dot(mix.astype(BF16), wo_ref[...])


def _merge(h, o_gla, u, o_swa, hs, o_gla_s, us, o_swa_s, onorm, gbias, w_gla_o, w_swa_o, w_out,
           *, tm, name):
    rows, rs = h.shape[0], o_gla_s.shape[0]
    const = dict(pipeline_mode=pl.Buffered(1))

    def row_specs(r, idx):
        return [
            pl.BlockSpec((r, D_MODEL), lambda i: (idx(i), 0)),
            pl.BlockSpec((r, GLA_V), lambda i: (idx(i), 0)),
            pl.BlockSpec((r, GLA_V), lambda i: (idx(i), COL_GR // GLA_V)),
            pl.BlockSpec((r, SWA_Q), lambda i: (idx(i), 0)),
            pl.BlockSpec((r, D_MODEL), lambda i: (idx(i), COL_GA // D_MODEL)),
            pl.BlockSpec((r, D_MODEL), lambda i: (idx(i), COL_GB // D_MODEL)),
        ]

    return pl.pallas_call(
        _merge_kernel,
        grid=(rows // tm,),
        in_specs=row_specs(tm, lambda i: i) + row_specs(rs, lambda i: 0) + [
            pl.BlockSpec((1, GLA_V), lambda i: (0, 0)),
            pl.BlockSpec((1, 2 * D_MODEL), lambda i: (0, 0)),
            pl.BlockSpec((GLA_V, D_MODEL), lambda i: (0, 0), **const),
            pl.BlockSpec((SWA_Q, D_MODEL), lambda i: (0, 0), **const),
            pl.BlockSpec((D_MODEL, D_MODEL), lambda i: (0, 0), **const),
        ],
        out_specs=[
            pl.BlockSpec((tm, D_MODEL), lambda i: (i, 0)),
            pl.BlockSpec((rs, D_MODEL), lambda i: (0, 0)),
        ],
        out_shape=[
            jax.ShapeDtypeStruct((rows, D_MODEL), F32),
            jax.ShapeDtypeStruct((rs, D_MODEL), F32),
        ],
        compiler_params=pltpu.CompilerParams(
            dimension_semantics=("arbitrary",), vmem_limit_bytes=56 * MIB),
        name=name,
    )(h, o_gla, u, o_swa, u, u, hs, o_gla_s, us, o_swa_s, us, us,
      onorm, gbias, w_gla_o, w_swa_o, w_out)


def kernel(x_prompt, x_sample, cache_meta_k, cache_meta_v, cache_win_k, cache_win_v, state_gla,
           meta_tokens, ffn1_norm, ffn1_w_in, ffn1_w_out, mix_norm, w_in, gla_a_up, gla_a_bias,
           gla_out_norm, w_gla_o, swa_sinks, w_swa_o, gate_bias, w_out, ffn2_norm, ffn2_w_in,
           ffn2_w_out, final_norm):
    B, S, _ = x_prompt.shape
    DB = x_sample.shape[0]
    assert x_sample.shape[1] == 1 and ffn1_norm.shape[0] == 1
    n_small = DB + N_META

    wgo, wso, wout = w_gla_o[0].astype(BF16), w_swa_o[0].astype(BF16), w_out[0].astype(BF16)
    g1, gm, g2 = ffn1_norm, mix_norm, ffn2_norm
    gf = final_norm.reshape(1, D_MODEL)
    a_up, a_bias = gla_a_up[0], gla_a_bias
    onorm, gbias = gla_out_norm, gate_bias
    sinks = swa_sinks[0]

    xs = jnp.concatenate([x_sample.reshape(DB, D_MODEL), meta_tokens.astype(F32)], axis=0)
    xp = x_prompt.reshape(B * S, D_MODEL)
    hp1, hs1 = _ffn(xp, xs, g1, ffn1_w_in[0], ffn1_w_out[0], gf, tm=1024, tf_first=256, tf=512,
                    final_norm=False, name="ffn1")
    up, gadp, us, gads = _inproj(hp1, hs1, gm, w_in[0].T, tm=2048, tn_first=256, tn=512, name="inproj")

    meta_blk = DB // N_META
    _, st_meta = _gla(us, gads, a_up, a_bias, jnp.zeros((GLA_HEADS, GLA_DK, GLA_DV), F32),
                      batch=1, seq=N_META, C=N_META, row_off=meta_blk, name="gla_meta")
    st_meta = st_meta[0]
    o_gla, st_fin = _gla(up, gadp, a_up, a_bias, st_meta, batch=B, seq=S, C=128, row_off=0, name="gla")
    o_swa = _swa(sinks, up, us, batch=B, seq=S, meta_row_block=meta_blk, name="swa")

    us3 = us.reshape(n_small, 1, N_MAIN)
    og_s, st_s = _gla_step(us3, gads.reshape(n_small, 1, GLA_RANK),
                           a_up, a_bias, state_gla[0], batch=DB, bb=4, name="gla_step")
    WB = cache_win_k.shape[2]
    assert WB == WINDOW
    os_s, new_k, new_v = _swa_step(
        us[:DB, COL_SQ:COL_SQ + SWA_Q].reshape(DB, SWA_HEADS, SWA_HD),
        cache_win_k[0].reshape(DB, WB, SWA_KV), cache_win_v[0].reshape(DB, WB, SWA_KV), us3,
        cache_meta_k[0].reshape(DB, N_META, SWA_KV), cache_meta_v[0].reshape(DB, N_META, SWA_KV),
        sinks, batch=DB, bb=8, name="swa_step")
    hp2, hs2 = _merge(hp1, o_gla, up, o_swa, hs1, og_s.reshape(DB, GLA_V), us, os_s.reshape(DB, SWA_Q),
                      onorm, gbias, wgo, wso, wout, tm=256, name="merge")
    y_prompt, y_sample = _ffn(hp2, hs2, g2, ffn2_w_in[0], ffn2_w_out[0], gf, tm=1024, tf_first=256, tf=512,
                              final_norm=True, name="ffn2")

    up3 = up.reshape(B, S, N_MAIN)
    kv_shape = (SWA_KV_HEADS, SWA_HD)
    p_meta_k = jnp.broadcast_to(us[DB:, COL_SK:COL_SK + SWA_KV].reshape(1, 1, N_META, *kv_shape),
                                (1, B, N_META, *kv_shape))
    p_meta_v = jnp.broadcast_to(us[DB:, COL_SV:COL_SV + SWA_KV].reshape(1, 1, N_META, *kv_shape),
                                (1, B, N_META, *kv_shape))
    p_win_k = up3[:, S - WINDOW:, COL_SK:COL_SK + SWA_KV].reshape(1, B, WINDOW, *kv_shape)
    p_win_v = up3[:, S - WINDOW:, COL_SV:COL_SV + SWA_KV].reshape(1, B, WINDOW, *kv_shape)
    return (y_prompt.reshape(B, S, D_MODEL), y_sample.reshape(DB, 1, D_MODEL),
            p_meta_k, p_meta_v, p_win_k, p_win_v, st_fin[None],
            new_k.reshape(1, DB, WB, *kv_shape), new_v.reshape(1, DB, WB, *kv_shape), st_s[None])
```

```python
import functools

import jax
import jax.numpy as jnp
from jax import lax
from jax.experimental import pallas as pl
from jax.experimental.pallas import tpu as pltpu

F32 = jnp.float32
BF16 = jnp.bfloat16

D_MODEL = 2048
N_META = 16
GLA_HEADS = 4
GLA_DK = 128
GLA_DV = 256
GLA_RANK = 16
GLA_TAU = 16.0
SWA_HEADS = 16
SWA_KV_HEADS = 4
SWA_GROUP = SWA_HEADS // SWA_KV_HEADS
SWA_HD = 64
WINDOW = 128
D_FF = 5632
EPS = 1e-6

GLA_QK = GLA_HEADS * GLA_DK
GLA_V = GLA_HEADS * GLA_DV
SWA_Q = SWA_HEADS * SWA_HD
SWA_KV = SWA_KV_HEADS * SWA_HD

COL_GA = 0
COL_GB = COL_GA + D_MODEL
COL_GQ = COL_GB + D_MODEL
COL_GK = COL_GQ + GLA_QK
COL_GV = COL_GK + GLA_QK
COL_GR = COL_GV + GLA_V
COL_SQ = COL_GR + GLA_V
COL_SK = COL_SQ + SWA_Q
COL_SV = COL_SK + SWA_KV
N_MAIN = COL_SV + SWA_KV
N_GRP_A = COL_GQ
N_GRP_B = COL_SQ - COL_GQ
N_GRP_C = N_MAIN - COL_SQ

GLA_SUB = 8
MASK_NEG = -1e30
MIB = 1024 * 1024
LANE = 128
BF16_ROWS = 16
ROW_CHUNK = 256
COL_CHUNK = 512

SWA_QB = 64
SWA_SPAN = WINDOW + SWA_QB
SWA_KT = 256
SWA_SINK_COL = SWA_SPAN + N_META


def _rms_scale(x):
    return x * lax.rsqrt(jnp.mean(x * x, axis=-1, keepdims=True) + EPS)


def _log_sigmoid(x):
    return jnp.minimum(x, 0.0) - jnp.log(1.0 + jnp.exp(-jnp.abs(x)))


def _dot(a, b):
    return jnp.dot(a, b, preferred_element_type=F32)


def _dot_nt(a, b):
    return lax.dot_general(a, b, (((1,), (1,)), ((), ())), preferred_element_type=F32)


def _row_to_col(row, n):
    r = lax.broadcasted_iota(jnp.int32, (n, n), 0)
    c = lax.broadcasted_iota(jnp.int32, (n, n), 1)
    return jnp.sum(jnp.where(r == c, jnp.broadcast_to(row, (n, n)), 0.0), axis=1, keepdims=True)


def _for_row_chunks(rows, fn):
    chunk = ROW_CHUNK if rows % ROW_CHUNK == 0 else rows

    def body(i, carry):
        fn(pl.ds(pl.multiple_of(i * chunk, chunk), chunk))
        return carry

    lax.fori_loop(0, rows // chunk, body, 0)


def _row_offsets(refs):
    offs, total = [], 0
    for r in refs:
        offs.append(total)
        total += r.shape[0]
    return offs


def _ffn_rows(j, n_ff, x_refs, o_refs, xn_ref, g_ref, fg_ref, wa, wb, wo, final_norm):
    offs = _row_offsets(x_refs)

    @pl.when(j == 0)
    def _():
        for x_ref, o_ref, off in zip(x_refs, o_refs, offs):
            def norm_rows(sl, x_ref=x_ref, o_ref=o_ref, off=off):
                x = x_ref[sl, :]
                dst = pl.ds(pl.multiple_of(off + sl.start, BF16_ROWS), sl.size)
                xn_ref[dst, :] = (_rms_scale(x) * g_ref[...]).astype(BF16)
                o_ref[sl, :] = x

            _for_row_chunks(x_ref.shape[0], norm_rows)

    xn = xn_ref[...]
    a = _dot(xn, wa)
    b = _dot(xn, wb)
    h = ((0.5 * a) * jax.nn.sigmoid(a) * b).astype(BF16)
    for c in range(D_MODEL // COL_CHUNK):
        cs = slice(COL_CHUNK * c, COL_CHUNK * (c + 1))
        r = _dot(h, wo[:, cs])
        for o_ref, off in zip(o_refs, offs):
            o_ref[:, cs] += r[off:off + o_ref.shape[0]]

    if final_norm:
        @pl.when(j == n_ff - 1)
        def _():
            for o_ref in o_refs:
                def final_rows(sl, o_ref=o_ref):
                    o_ref[sl, :] = _rms_scale(o_ref[sl, :]) * fg_ref[...]

                _for_row_chunks(o_ref.shape[0], final_rows)


def _ffn_first_kernel(x_ref, xs_ref, g_ref, wa_ref, wb_ref, wo_ref, fg_ref,
                      o_ref, os_ref, wa16_ref, wb16_ref, wo16_ref, xn_ref, *, n_ff, final_norm):
    wa = wa_ref[...].astype(BF16)
    wb = wb_ref[...].astype(BF16)
    wo = wo_ref[...].astype(BF16)
    wa16_ref[...] = wa
    wb16_ref[...] = wb
    wo16_ref[...] = wo
    _ffn_rows(pl.program_id(0), n_ff, [x_ref, xs_ref], [o_ref, os_ref], xn_ref, g_ref, fg_ref,
              wa, wb, wo, final_norm)


def _ffn_rest_kernel(x_ref, g_ref, wa_ref, wb_ref, wo_ref, fg_ref, o_first_ref, o_ref, xn_ref,
                     *, n_ff, final_norm):
    del o_first_ref
    _ffn_rows(pl.program_id(1), n_ff, [x_ref], [o_ref], xn_ref, g_ref, fg_ref,
              wa_ref[...], wb_ref[...], wo_ref[...], final_norm)


def _ffn(x, xs, gain, w_in, w_out, final_gain, *, tm, tf_first, tf, final_norm, name):
    rows, rows_s = x.shape[0], xs.shape[0]
    n1 = D_FF // tf_first
    o, os_, wa16, wb16, wo16 = pl.pallas_call(
        functools.partial(_ffn_first_kernel, n_ff=n1, final_norm=final_norm),
        grid=(n1,),
        in_specs=[
            pl.BlockSpec((tm, D_MODEL), lambda j: (0, 0), pipeline_mode=pl.Buffered(1)),
            pl.BlockSpec((rows_s, D_MODEL), lambda j: (0, 0), pipeline_mode=pl.Buffered(1)),
            pl.BlockSpec((1, D_MODEL), lambda j: (0, 0)),
            pl.BlockSpec((D_MODEL, tf_first), lambda j: (0, j)),
            pl.BlockSpec((D_MODEL, tf_first), lambda j: (0, j + n1)),
            pl.BlockSpec((tf_first, D_MODEL), lambda j: (j, 0)),
            pl.BlockSpec((1, D_MODEL), lambda j: (0, 0)),
        ],
        out_specs=[
            pl.BlockSpec((tm, D_MODEL), lambda j: (0, 0)),
            pl.BlockSpec((rows_s, D_MODEL), lambda j: (0, 0)),
            pl.BlockSpec((D_MODEL, tf_first), lambda j: (0, j)),
            pl.BlockSpec((D_MODEL, tf_first), lambda j: (0, j)),
            pl.BlockSpec((tf_first, D_MODEL), lambda j: (j, 0)),
        ],
        out_shape=[
            jax.ShapeDtypeStruct((rows, D_MODEL), F32),
            jax.ShapeDtypeStruct((rows_s, D_MODEL), F32),
            jax.ShapeDtypeStruct((D_MODEL, D_FF), BF16),
            jax.ShapeDtypeStruct((D_MODEL, D_FF), BF16),
            jax.ShapeDtypeStruct((D_FF, D_MODEL), BF16),
        ],
        scratch_shapes=[pltpu.VMEM((tm + rows_s, D_MODEL), BF16)],
        compiler_params=pltpu.CompilerParams(
            dimension_semantics=("arbitrary",), vmem_limit_bytes=58 * MIB),
        name=name + "_first",
    )(x, xs, gain, w_in, w_in, w_out, final_gain)

    n2 = D_FF // tf
    o = pl.pallas_call(
        functools.partial(_ffn_rest_kernel, n_ff=n2, final_norm=final_norm),
        grid=(rows // tm - 1, n2),
        in_specs=[
            pl.BlockSpec((tm, D_MODEL), lambda i, j: (i + 1, 0)),
            pl.BlockSpec((1, D_MODEL), lambda i, j: (0, 0)),
            pl.BlockSpec((D_MODEL, tf), lambda i, j: (0, j)),
            pl.BlockSpec((D_MODEL, tf), lambda i, j: (0, j)),
            pl.BlockSpec((tf, D_MODEL), lambda i, j: (j, 0)),
            pl.BlockSpec((1, D_MODEL), lambda i, j: (0, 0)),
            pl.BlockSpec(memory_space=pl.ANY),
        ],
        out_specs=pl.BlockSpec((tm, D_MODEL), lambda i, j: (i + 1, 0)),
        out_shape=jax.ShapeDtypeStruct((rows, D_MODEL), F32),
        input_output_aliases={6: 0},
        scratch_shapes=[pltpu.VMEM((tm, D_MODEL), BF16)],
        compiler_params=pltpu.CompilerParams(
            dimension_semantics=("parallel", "arbitrary"), vmem_limit_bytes=58 * MIB),
        name=name + "_rest",
    )(x, gain, wa16, wb16, wo16, final_gain, o)
    return o, os_


def _inproj_rows(j, x_refs, xn_ref, g_ref, w, wg, u_refs, gad_refs):
    offs = _row_offsets(x_refs)

    @pl.when(j == 0)
    def _():
        for x_ref, off in zip(x_refs, offs):
            def norm_rows(sl, x_ref=x_ref, off=off):
                dst = pl.ds(pl.multiple_of(off + sl.start, BF16_ROWS), sl.size)
                xn_ref[dst, :] = (_rms_scale(x_ref[sl, :]) * g_ref[...]).astype(BF16)

            _for_row_chunks(x_ref.shape[0], norm_rows)
        gad = _dot_nt(xn_ref[...], wg)
        for gad_ref, off in zip(gad_refs, offs):
            gad_ref[...] = gad[off:off + gad_ref.shape[0]]

    u = _dot(xn_ref[...], w)
    for u_ref, off in zip(u_refs, offs):
        u_ref[...] = u[off:off + u_ref.shape[0]]


def _inproj_first_kernel(x_ref, xs_ref, g_ref, wm_ref, wx_ref, wg_ref,
                         u_ref, gad_ref, us_ref, gads_ref, w16_ref, wg16_ref,
                         xn_ref, *, na, nb, tn):
    j = pl.program_id(0)
    shifted = (j < na) | (j >= na + nb)

    @pl.when(shifted)
    def _():
        tall = jnp.concatenate([wm_ref[...], wx_ref[...]], axis=0)
        w16_ref[...] = tall[GLA_RANK:GLA_RANK + tn].T.astype(BF16)

    @pl.when(jnp.logical_not(shifted))
    def _():
        w16_ref[...] = wm_ref[...].T.astype(BF16)

    w = w16_ref[...]
    wg = wg_ref[...].astype(BF16)

    @pl.when(j == 0)
    def _():
        wg16_ref[...] = wg

    _inproj_rows(j, [x_ref, xs_ref], xn_ref, g_ref, w, wg, [u_ref, us_ref], [gad_ref, gads_ref])


def _inproj_rest_kernel(x_ref, g_ref, w_ref, wg_ref, u_first_ref, gad_first_ref, u_ref, gad_ref, xn_ref):
    del u_first_ref, gad_first_ref
    _inproj_rows(pl.program_id(1), [x_ref], xn_ref, g_ref, w_ref[...], wg_ref[...], [u_ref], [gad_ref])


def _inproj(x, xs, gain, w_in, *, tm, tn_first, tn, name):
    rows, rows_s = x.shape[0], xs.shape[0]
    na, nb, ncc = N_GRP_A // tn_first, N_GRP_B // tn_first, N_GRP_C // tn_first
    nj = na + nb + ncc
    c_gad = N_GRP_B
    c_sq = c_gad + GLA_RANK
    c_ga = c_sq + N_GRP_C
    assert c_gad % tn_first == 0 and (c_ga - GLA_RANK) % tn_first == 0
    assert tn_first % LANE == 0 and c_gad % GLA_RANK == 0

    def window(j):
        return jnp.where(j < na, (c_ga - GLA_RANK) // tn_first + j,
                         jnp.where(j < na + nb, j - na, c_gad // tn_first + j - na - nb))

    u, gad, us, gads, w16, wg16 = pl.pallas_call(
        functools.partial(_inproj_first_kernel, na=na, nb=nb, tn=tn_first),
        grid=(nj,),
        in_specs=[
            pl.BlockSpec((tm, D_MODEL), lambda j: (0, 0), pipeline_mode=pl.Buffered(1)),
            pl.BlockSpec((rows_s, D_MODEL), lambda j: (0, 0), pipeline_mode=pl.Buffered(1)),
            pl.BlockSpec((1, D_MODEL), lambda j: (0, 0)),
            pl.BlockSpec((tn_first, D_MODEL), lambda j: (window(j), 0)),
            pl.BlockSpec((GLA_RANK, D_MODEL), lambda j: ((window(j) + 1) * (tn_first // GLA_RANK), 0)),
            pl.BlockSpec((GLA_RANK, D_MODEL), lambda j: (c_gad // GLA_RANK, 0)),
        ],
        out_specs=[
            pl.BlockSpec((tm, tn_first), lambda j: (0, j)),
            pl.BlockSpec((tm, GLA_RANK), lambda j: (0, 0)),
            pl.BlockSpec((rows_s, tn_first), lambda j: (0, j)),
            pl.BlockSpec((rows_s, GLA_RANK), lambda j: (0, 0)),
            pl.BlockSpec((D_MODEL, tn_first), lambda j: (0, j)),
            pl.BlockSpec((GLA_RANK, D_MODEL), lambda j: (0, 0)),
        ],
        out_shape=[
            jax.ShapeDtypeStruct((rows, N_MAIN), F32),
            jax.ShapeDtypeStruct((rows, GLA_RANK), F32),
            jax.ShapeDtypeStruct((rows_s, N_MAIN), F32),
            jax.ShapeDtypeStruct((rows_s, GLA_RANK), F32),
            jax.ShapeDtypeStruct((D_MODEL, N_MAIN), BF16),
            jax.ShapeDtypeStruct((GLA_RANK, D_MODEL), BF16),
        ],
        scratch_shapes=[pltpu.VMEM((tm + rows_s, D_MODEL), BF16)],
        compiler_params=pltpu.CompilerParams(
            dimension_semantics=("arbitrary",), vmem_limit_bytes=56 * MIB),
        name=name + "_first",
    )(x, xs, gain, w_in, w_in, w_in)

    u, gad = pl.pallas_call(
        _inproj_rest_kernel,
        grid=(rows // tm - 1, N_MAIN // tn),
        in_specs=[
            pl.BlockSpec((tm, D_MODEL), lambda i, j: (i + 1, 0), pipeline_mode=pl.Buffered(1)),
            pl.BlockSpec((1, D_MODEL), lambda i, j: (0, 0)),
            pl.BlockSpec((D_MODEL, tn), lambda i, j: (0, j)),
            pl.BlockSpec((GLA_RANK, D_MODEL), lambda i, j: (0, 0)),
            pl.BlockSpec(memory_space=pl.ANY),
            pl.BlockSpec(memory_space=pl.ANY),
        ],
        out_specs=[
            pl.BlockSpec((tm, tn), lambda i, j: (i + 1, j)),
            pl.BlockSpec((tm, GLA_RANK), lambda i, j: (i + 1, 0)),
        ],
        out_shape=[
            jax.ShapeDtypeStruct((rows, N_MAIN), F32),
            jax.ShapeDtypeStruct((rows, GLA_RANK), F32),
        ],
        input_output_aliases={4: 0, 5: 1},
        scratch_shapes=[pltpu.VMEM((tm, D_MODEL), BF16)],
        compiler_params=pltpu.CompilerParams(
            dimension_semantics=("parallel", "arbitrary"), vmem_limit_bytes=56 * MIB),
        name=name + "_rest",
    )(x, gain, w16, wg16, u, gad)
    return u, gad, us, gads


def _gla_head(q, k, v, b, state, *, C):
    o = _dot((q * jnp.exp(b)).astype(BF16), state.astype(BF16))

    ri = lax.broadcasted_iota(jnp.int32, (C, C), 0)
    ci = lax.broadcasted_iota(jnp.int32, (C, C), 1)
    rowd = lax.broadcasted_iota(jnp.int32, (C, GLA_DK), 0)
    att = None
    m = C // 2
    while m >= GLA_SUB:
        nblk = C // (2 * m)
        pieces = [jnp.broadcast_to(b[i * 2 * m + m - 1:i * 2 * m + m, :], (2 * m, GLA_DK))
                  for i in range(nblk)]
        bref = pieces[0] if nblk == 1 else jnp.concatenate(pieces, axis=0)
        e = jnp.exp(-jnp.abs(b - bref))
        second = (rowd & m) != 0
        ql = jnp.where(second, q * e, 0.0).astype(BF16)
        kl = jnp.where(second, 0.0, k * e).astype(BF16)
        a = _dot_nt(ql, kl)
        if nblk > 1:
            a = jnp.where((ri ^ ci) < 2 * m, a, 0.0)
        att = a if att is None else att + a
        m //= 2

    sub_row = lax.broadcasted_iota(jnp.int32, (GLA_SUB, GLA_DK), 0)
    sub_col = lax.broadcasted_iota(jnp.int32, (GLA_SUB, C), 1)
    diag = []
    for i in range(C // GLA_SUB):
        sl = slice(GLA_SUB * i, GLA_SUB * (i + 1))
        bb, qq, kk = b[sl], q[sl], k[sl]
        blk = jnp.zeros((GLA_SUB, C), F32)
        for s in range(GLA_SUB):
            d = jnp.where(sub_row >= s, bb - bb[s:s + 1], MASK_NEG)
            w = jnp.sum(qq * jnp.exp(d) * kk[s:s + 1], axis=-1, keepdims=True)
            blk = jnp.where(sub_col == GLA_SUB * i + s, w, blk)
        diag.append(blk)
    diag = diag[0] if len(diag) == 1 else jnp.concatenate(diag, axis=0)
    att = diag if att is None else att + diag
    o = o + _dot(att.astype(BF16), v.astype(BF16))

    bl = b[C - 1:C, :]
    kd = k * jnp.exp(bl - b)
    vp = v
    if C < GLA_DK:
        kd = jnp.concatenate([kd, jnp.zeros((GLA_DK - C, GLA_DK), F32)], axis=0)
        vp = jnp.concatenate([v, jnp.zeros((GLA_DK - C, GLA_DV), F32)], axis=0)
    new_state = state * _row_to_col(jnp.exp(bl), GLA_DK) + _dot(kd.T.astype(BF16), vp.astype(BF16))
    return o, new_state


def _gla_kernel(q_ref, k_ref, v_ref, gad_ref, aup_ref, ab_ref, s0_ref, o_ref, sfin_ref, s_ref,
                *, C, sub, nc):
    n = pl.program_id(1)

    @pl.when(n == 0)
    def _():
        s_ref[...] = s0_ref[...]

    def chunk(c, carry):
        rows = pl.ds(pl.multiple_of(c * C, C), C)
        logits = _dot(gad_ref[rows, :].astype(BF16), aup_ref[...].astype(BF16)) + ab_ref[...]
        g = _log_sigmoid(logits) / GLA_TAU

        ri = lax.broadcasted_iota(jnp.int32, (C, C), 0)
        ci = lax.broadcasted_iota(jnp.int32, (C, C), 1)
        tri = jnp.where(ri >= ci, 1.0, 0.0).astype(BF16)
        g_hi = g.astype(BF16)
        r1 = g - g_hi.astype(F32)
        g_mid = r1.astype(BF16)
        g_lo = (r1 - g_mid.astype(F32)).astype(BF16)
        b_all = _dot(tri, g_hi) + _dot(tri, g_mid) + _dot(tri, g_lo)

        for h in range(GLA_HEADS):
            ks = slice(GLA_DK * h, GLA_DK * (h + 1))
            vs = slice(GLA_DV * h, GLA_DV * (h + 1))
            o, new_state = _gla_head(q_ref[rows, ks] * (GLA_DK ** -0.5), k_ref[rows, ks], v_ref[rows, vs],
                                     b_all[:, ks], s_ref[h], C=C)
            o_ref[rows, vs] = o
            s_ref[h] = new_state
        return carry

    if sub == 1:
        chunk(0, 0)
    else:
        lax.fori_loop(0, sub, chunk, 0)

    @pl.when(n == nc - 1)
    def _():
        sfin_ref[0] = s_ref[...]


def _gla(u, gad, a_up, a_bias, s0, *, batch, seq, C, sub, row_off, name):
    rows = C * sub
    nc = seq // rows
    return pl.pallas_call(
        functools.partial(_gla_kernel, C=C, sub=sub, nc=nc),
        grid=(batch, nc),
        in_specs=[
            pl.BlockSpec((rows, GLA_QK), lambda b, n: (row_off + b * nc + n, COL_GQ // GLA_QK)),
            pl.BlockSpec((rows, GLA_QK), lambda b, n: (row_off + b * nc + n, COL_GK // GLA_QK)),
            pl.BlockSpec((rows, GLA_V), lambda b, n: (row_off + b * nc + n, COL_GV // GLA_V)),
            pl.BlockSpec((rows, GLA_RANK), lambda b, n: (row_off + b * nc + n, 0)),
            pl.BlockSpec((GLA_RANK, GLA_QK), lambda b, n: (0, 0)),
            pl.BlockSpec((1, GLA_QK), lambda b, n: (0, 0)),
            pl.BlockSpec((GLA_HEADS, GLA_DK, GLA_DV), lambda b, n: (0, 0, 0)),
        ],
        out_specs=[
            pl.BlockSpec((rows, GLA_V), lambda b, n: (b * nc + n, 0)),
            pl.BlockSpec((1, GLA_HEADS, GLA_DK, GLA_DV), lambda b, n: (b, 0, 0, 0)),
        ],
        out_shape=[
            jax.ShapeDtypeStruct((batch * seq, GLA_V), F32),
            jax.ShapeDtypeStruct((batch, GLA_HEADS, GLA_DK, GLA_DV), F32),
        ],
        scratch_shapes=[pltpu.VMEM((GLA_HEADS, GLA_DK, GLA_DV), F32)],
        compiler_params=pltpu.CompilerParams(
            dimension_semantics=("parallel", "arbitrary"), vmem_limit_bytes=32 * MIB),
        name=name,
    )(u, u, u, gad, a_up, a_bias, s0)


def _gla_step_kernel(q_ref, k_ref, v_ref, gad_ref, aup_ref, ab_ref, s_ref, o_ref, sn_ref, *, bb):
    gl = jnp.concatenate([gad_ref[i] for i in range(bb)] + [jnp.zeros((8 - bb, GLA_RANK), F32)], axis=0)
    logits = _dot(gl.astype(BF16), aup_ref[...].astype(BF16)) + ab_ref[...]
    dec_all = jnp.exp(_log_sigmoid(logits) / GLA_TAU)
    for i in range(bb):
        q = q_ref[i] * (GLA_DK ** -0.5)
        k = k_ref[i]
        v = v_ref[i]
        dec = dec_all[i:i + 1]
        outs = []
        for h in range(GLA_HEADS):
            sl = slice(GLA_DK * h, GLA_DK * (h + 1))
            new_state = (s_ref[i, h] * _row_to_col(dec[:, sl], GLA_DK)
                         + _row_to_col(k[:, sl], GLA_DK) * v[:, GLA_DV * h:GLA_DV * (h + 1)])
            sn_ref[i, h] = new_state
            outs.append(jnp.sum(_row_to_col(q[:, sl], GLA_DK) * new_state, axis=0, keepdims=True))
        o_ref[i] = jnp.concatenate(outs, axis=1)


def _gla_step(u3, gad3, a_up, a_bias, state, *, batch, bb, name):
    return pl.pallas_call(
        functools.partial(_gla_step_kernel, bb=bb),
        grid=(batch // bb,),
        in_specs=[
            pl.BlockSpec((bb, 1, GLA_QK), lambda b: (b, 0, COL_GQ // GLA_QK)),
            pl.BlockSpec((bb, 1, GLA_QK), lambda b: (b, 0, COL_GK // GLA_QK)),
            pl.BlockSpec((bb, 1, GLA_V), lambda b: (b, 0, COL_GV // GLA_V)),
            pl.BlockSpec((bb, 1, GLA_RANK), lambda b: (b, 0, 0)),
            pl.BlockSpec((GLA_RANK, GLA_QK), lambda b: (0, 0)),
            pl.BlockSpec((1, GLA_QK), lambda b: (0, 0)),
            pl.BlockSpec((bb, GLA_HEADS, GLA_DK, GLA_DV), lambda b: (b, 0, 0, 0)),
        ],
        out_specs=[
            pl.BlockSpec((bb, 1, GLA_V), lambda b: (b, 0, 0)),
            pl.BlockSpec((bb, GLA_HEADS, GLA_DK, GLA_DV), lambda b: (b, 0, 0, 0)),
        ],
        out_shape=[
            jax.ShapeDtypeStruct((batch, 1, GLA_V), F32),
            jax.ShapeDtypeStruct((batch, GLA_HEADS, GLA_DK, GLA_DV), F32),
        ],
        compiler_params=pltpu.CompilerParams(dimension_semantics=("parallel",)),
        name=name,
    )(u3, u3, u3, gad3, a_up, a_bias, state)


def _alibi_slope(head):
    return 2.0 ** (-8.0 * (head + 1) / SWA_HEADS)


def _swa_kernel(sink_ref, q_ref, kc_ref, kp_ref, vc_ref, vp_ref, mk_ref, mv_ref, o_ref, bias_ref):
    n = pl.program_id(1)
    rows = SWA_GROUP * SWA_QB
    nsub = WINDOW // SWA_QB

    @pl.when(n <= 1)
    def _():
        r = lax.broadcasted_iota(jnp.int32, (SWA_QB, SWA_KT), 0)
        c = lax.broadcasted_iota(jnp.int32, (SWA_QB, SWA_KT), 1)
        dist = r + WINDOW - c
        distf = dist.astype(F32)
        band = (dist >= 0) & (dist < WINDOW) & (c < SWA_SPAN)
        for a in range(nsub):
            ok = band & ((c >= WINDOW - SWA_QB * a) | (n > 0))
            for head in range(SWA_HEADS):
                val = jnp.where(ok, -_alibi_slope(head) * distf, MASK_NEG)
                val = jnp.where((c >= SWA_SPAN) & (c < SWA_SINK_COL), 0.0, val)
                val = jnp.where(c == SWA_SINK_COL, sink_ref[head], val)
                tile = (head // SWA_GROUP) * nsub + a
                bias_ref[pl.ds(tile * rows + (head % SWA_GROUP) * SWA_QB, SWA_QB), :] = val

    q = q_ref[...] * (SWA_HD ** -0.5)
    pad = jnp.zeros((SWA_KT - SWA_SPAN - N_META, SWA_HD), BF16)
    scores, values = [], []
    for kvh in range(SWA_KV_HEADS):
        cs = slice(SWA_HD * kvh, SWA_HD * (kvh + 1))
        kwin = jnp.concatenate([kp_ref[:, cs], kc_ref[:, cs]], axis=0).astype(BF16)
        vwin = jnp.concatenate([vp_ref[:, cs], vc_ref[:, cs]], axis=0).astype(BF16)
        km = mk_ref[:, cs].astype(BF16)
        vm = mv_ref[:, cs].astype(BF16)
        for a in range(nsub):
            qs = slice(SWA_QB * a, SWA_QB * (a + 1))
            kt = jnp.concatenate([kwin[SWA_QB * a:SWA_QB * a + SWA_SPAN], km, pad], axis=0)
            values.append(jnp.concatenate([vwin[SWA_QB * a:SWA_QB * a + SWA_SPAN], vm, pad], axis=0))
            qg = jnp.concatenate(
                [q[qs, SWA_HD * (kvh * SWA_GROUP + grp):SWA_HD * (kvh * SWA_GROUP + grp + 1)]
                 for grp in range(SWA_GROUP)], axis=0).astype(BF16)
            scores.append(_dot_nt(qg, kt))
    s = jnp.concatenate(scores, axis=0) + bias_ref[...]
    p = jnp.exp(s - jnp.max(s, axis=-1, keepdims=True))
    inv = 1.0 / jnp.sum(p, axis=-1, keepdims=True)
    p = p.astype(BF16)
    for kvh in range(SWA_KV_HEADS):
        for a in range(nsub):
            tile = kvh * nsub + a
            ts = slice(rows * tile, rows * (tile + 1))
            o = _dot(p[ts], values[tile]) * inv[ts]
            for grp in range(SWA_GROUP):
                head = kvh * SWA_GROUP + grp
                o_ref[SWA_QB * a:SWA_QB * (a + 1), SWA_HD * head:SWA_HD * (head + 1)] = (
                    o[SWA_QB * grp:SWA_QB * (grp + 1)].astype(o_ref.dtype))


def _swa(sinks, u, u_small, *, batch, seq, meta_row_block, name):
    nb = seq // WINDOW
    kcol, vcol = COL_SK // SWA_KV, COL_SV // SWA_KV
    return pl.pallas_call(
        _swa_kernel,
        grid=(batch, nb),
        in_specs=[
            pl.BlockSpec(memory_space=pltpu.SMEM),
            pl.BlockSpec((WINDOW, SWA_Q), lambda b, n: (b * nb + n, COL_SQ // SWA_Q)),
            pl.BlockSpec((WINDOW, SWA_KV), lambda b, n: (b * nb + n, kcol)),
            pl.BlockSpec((WINDOW, SWA_KV), lambda b, n: (b * nb + jnp.maximum(n - 1, 0), kcol)),
            pl.BlockSpec((WINDOW, SWA_KV), lambda b, n: (b * nb + n, vcol)),
            pl.BlockSpec((WINDOW, SWA_KV), lambda b, n: (b * nb + jnp.maximum(n - 1, 0), vcol)),
            pl.BlockSpec((N_META, SWA_KV), lambda b, n: (meta_row_block, kcol)),
            pl.BlockSpec((N_META, SWA_KV), lambda b, n: (meta_row_block, vcol)),
        ],
        out_specs=pl.BlockSpec((WINDOW, SWA_Q), lambda b, n: (b * nb + n, 0)),
        out_shape=jax.ShapeDtypeStruct((batch * seq, SWA_Q), BF16),
        scratch_shapes=[pltpu.VMEM((SWA_HEADS * WINDOW, SWA_KT), F32)],
        compiler_params=pltpu.CompilerParams(
            dimension_semantics=("arbitrary", "arbitrary"), vmem_limit_bytes=32 * MIB),
        name=name,
    )(sinks, u, u, u, u, u, u_small, u_small)


def _swa_step_kernel(q_ref, kold_ref, vold_ref, knew_ref, vnew_ref, mk_ref, mv_ref, sink_ref, slope_ref,
                     o_ref, nk_ref, nv_ref, *, bb):
    rows = bb * SWA_GROUP
    wcols = bb * WINDOW
    ncol = wcols + bb * N_META
    win_shift, meta_shift, grp_shift = WINDOW.bit_length() - 1, N_META.bit_length() - 1, SWA_GROUP.bit_length() - 1
    for i in range(bb):
        nk_ref[i] = jnp.concatenate([kold_ref[i, 1:, :], knew_ref[i]], axis=0)
        nv_ref[i] = jnp.concatenate([vold_ref[i, 1:, :], vnew_ref[i]], axis=0)

    r = lax.broadcasted_iota(jnp.int32, (rows, ncol), 0)
    c = lax.broadcasted_iota(jnp.int32, (rows, ncol), 1)
    in_win = c < wcols
    col_batch = jnp.where(in_win, c >> win_shift, (c - wcols) >> meta_shift)
    own = col_batch == (r >> grp_shift)
    distf = jnp.where(in_win, WINDOW - 1 - (c & (WINDOW - 1)), 0).astype(F32)
    for kvh in range(SWA_KV_HEADS):
        cs = slice(SWA_HD * kvh, SWA_HD * (kvh + 1))
        hs = slice(SWA_GROUP * kvh, SWA_GROUP * (kvh + 1))
        kcat = jnp.concatenate([nk_ref[i, :, cs] for i in range(bb)]
                               + [mk_ref[i, :, cs] for i in range(bb)], axis=0).astype(BF16)
        vcat = jnp.concatenate([nv_ref[i, :, cs] for i in range(bb)]
                               + [mv_ref[i, :, cs] for i in range(bb)], axis=0).astype(BF16)
        qh = jnp.concatenate([q_ref[i, hs, :] for i in range(bb)], axis=0)
        qh = (qh * (SWA_HD ** -0.5)).astype(BF16)
        s = jnp.where(own, _dot_nt(qh, kcat) - slope_ref[kvh] * distf, MASK_NEG)
        sink = sink_ref[kvh]
        mx = jnp.maximum(jnp.max(s, axis=-1, keepdims=True), sink)
        p = jnp.exp(s - mx)
        den = jnp.sum(p, axis=-1, keepdims=True) + jnp.exp(sink - mx)
        oh = _dot(p.astype(BF16), vcat) / den
        for i in range(bb):
            o_ref[i, hs, :] = oh[SWA_GROUP * i:SWA_GROUP * (i + 1)]


def _swa_step(q3, k_old, v_old, u3, mk, mv, sinks, *, batch, bb, name):
    def per_row(x16):
        return jnp.tile(x16.reshape(SWA_KV_HEADS, 1, SWA_GROUP), (1, bb, 1)).reshape(
            SWA_KV_HEADS, bb * SWA_GROUP, 1)

    slopes = jnp.asarray([_alibi_slope(h) for h in range(SWA_HEADS)], F32)
    head_spec = pl.BlockSpec((SWA_KV_HEADS, bb * SWA_GROUP, 1), lambda b: (0, 0, 0))
    win_spec = pl.BlockSpec((bb, WINDOW, SWA_KV), lambda b: (b, 0, 0))
    meta_spec = pl.BlockSpec((bb, N_META, SWA_KV), lambda b: (b, 0, 0))
    return pl.pallas_call(
        functools.partial(_swa_step_kernel, bb=bb),
        grid=(batch // bb,),
        in_specs=[
            pl.BlockSpec((bb, SWA_HEADS, SWA_HD), lambda b: (b, 0, 0)),
            win_spec, win_spec,
            pl.BlockSpec((bb, 1, SWA_KV), lambda b: (b, 0, COL_SK // SWA_KV)),
            pl.BlockSpec((bb, 1, SWA_KV), lambda b: (b, 0, COL_SV // SWA_KV)),
            meta_spec, meta_spec, head_spec, head_spec,
        ],
        out_specs=[pl.BlockSpec((bb, SWA_HEADS, SWA_HD), lambda b: (b, 0, 0)), win_spec, win_spec],
        out_shape=[
            jax.ShapeDtypeStruct((batch, SWA_HEADS, SWA_HD), F32),
            jax.ShapeDtypeStruct((batch, WINDOW, SWA_KV), F32),
            jax.ShapeDtypeStruct((batch, WINDOW, SWA_KV), F32),
        ],
        compiler_params=pltpu.CompilerParams(dimension_semantics=("parallel",)),
        name=name,
    )(q3, k_old, v_old, u3, u3, mk, mv, per_row(sinks), per_row(slopes))


def _merge_kernel(h_ref, og_ref, gr_ref, os_ref, ga_ref, gb_ref,
                  hs_ref, ogs_ref, grs_ref, oss_ref, gas_ref, gbs_ref,
                  onorm_ref, gbias_ref, wg_ref, ws_ref, wo_ref, out_ref, outs_ref):
    _merge_rows(h_ref, og_ref, gr_ref, os_ref, ga_ref, gb_ref, onorm_ref, gbias_ref,
                wg_ref, ws_ref, wo_ref, out_ref)

    @pl.when(pl.program_id(0) == 0)
    def _():
        _merge_rows(hs_ref, ogs_ref, grs_ref, oss_ref, gas_ref, gbs_ref, onorm_ref, gbias_ref,
                    wg_ref, ws_ref, wo_ref, outs_ref)


def _merge_rows(h_ref, og_ref, gr_ref, os_ref, ga_ref, gb_ref, onorm_ref, gbias_ref,
                wg_ref, ws_ref, wo_ref, out_ref):
    og = og_ref[...]
    parts = []
    for h in range(GLA_HEADS):
        oh = og[:, GLA_DV * h:GLA_DV * (h + 1)]
        parts.append(oh * lax.rsqrt(jnp.mean(oh * oh, axis=-1, keepdims=True) + EPS))
    gr = gr_ref[...]
    of = jnp.concatenate(parts, axis=1) * onorm_ref[...] * (gr * jax.nn.sigmoid(gr))
    gla_b = _dot(of.astype(BF16), wg_ref[...])
    swa_b = _dot(os_ref[...].astype(BF16), ws_ref[...])
    gbias = gbias_ref[...]
    mix = (jax.nn.sigmoid(ga_ref[...] + gbias[:, :D_MODEL]) * gla_b
           + jax.nn.sigmoid(gb_ref[...] + gbias[:, D_MODEL:]) * swa_b)
    out_ref[...] = h_ref[...] + _dot(mix.astype(BF16), wo_ref[...])


def _merge(h, o_gla, u, o_swa, hs, o_gla_s, us, o_swa_s, onorm, gbias, w_gla_o, w_swa_o, w_out,
           *, tm, name):
    rows, rs = h.shape[0], o_gla_s.shape[0]
    const = dict(pipeline_mode=pl.Buffered(1))

    def row_specs(r, idx):
        return [
            pl.BlockSpec((r, D_MODEL), lambda i: (idx(i), 0)),
            pl.BlockSpec((r, GLA_V), lambda i: (idx(i), 0)),
            pl.BlockSpec((r, GLA_V), lambda i: (idx(i), COL_GR // GLA_V)),
            pl.BlockSpec((r, SWA_Q), lambda i: (idx(i), 0)),
            pl.BlockSpec((r, D_MODEL), lambda i: (idx(i), COL_GA // D_MODEL)),
            pl.BlockSpec((r, D_MODEL), lambda i: (idx(i), COL_GB // D_MODEL)),
        ]

    return pl.pallas_call(
        _merge_kernel,
        grid=(rows // tm,),
        in_specs=row_specs(tm, lambda i: i) + row_specs(rs, lambda i: 0) + [
            pl.BlockSpec((1, GLA_V), lambda i: (0, 0)),
            pl.BlockSpec((1, 2 * D_MODEL), lambda i: (0, 0)),
            pl.BlockSpec((GLA_V, D_MODEL), lambda i: (0, 0), **const),
            pl.BlockSpec((SWA_Q, D_MODEL), lambda i: (0, 0), **const),
            pl.BlockSpec((D_MODEL, D_MODEL), lambda i: (0, 0), **const),
        ],
        out_specs=[
            pl.BlockSpec((tm, D_MODEL), lambda i: (i, 0)),
            pl.BlockSpec((rs, D_MODEL), lambda i: (0, 0)),
        ],
        out_shape=[
            jax.ShapeDtypeStruct((rows, D_MODEL), F32),
            jax.ShapeDtypeStruct((rs, D_MODEL), F32),
        ],
        compiler_params=pltpu.CompilerParams(
            dimension_semantics=("arbitrary",), vmem_limit_bytes=56 * MIB),
        name=name,
    )(h, o_gla, u, o_swa, u, u, hs, o_gla_s, us, o_swa_s, us, us,
      onorm, gbias, w_gla_o, w_swa_o, w_out)


def kernel(x_prompt, x_sample, cache_meta_k, cache_meta_v, cache_win_k, cache_win_v, state_gla,
           meta_tokens, ffn1_norm, ffn1_w_in, ffn1_w_out, mix_norm, w_in, gla_a_up, gla_a_bias,
           gla_out_norm, w_gla_o, swa_sinks, w_swa_o, gate_bias, w_out, ffn2_norm, ffn2_w_in,
           ffn2_w_out, final_norm):
    B, S, _ = x_prompt.shape
    DB = x_sample.shape[0]
    assert x_sample.shape[1] == 1 and ffn1_norm.shape[0] == 1
    n_small = DB + N_META

    wgo, wso, wout = w_gla_o[0].astype(BF16), w_swa_o[0].astype(BF16), w_out[0].astype(BF16)
    g1, gm, g2 = ffn1_norm, mix_norm, ffn2_norm
    gf = final_norm.reshape(1, D_MODEL)
    a_up, a_bias = gla_a_up[0], gla_a_bias
    onorm, gbias = gla_out_norm, gate_bias
    sinks = swa_sinks[0]

    xs = jnp.concatenate([x_sample.reshape(DB, D_MODEL), meta_tokens.astype(F32)], axis=0)
    xp = x_prompt.reshape(B * S, D_MODEL)
    hp1, hs1 = _ffn(xp, xs, g1, ffn1_w_in[0], ffn1_w_out[0], gf, tm=1024, tf_first=256, tf=512,
                    final_norm=False, name="ffn1")
    up, gadp, us, gads = _inproj(hp1, hs1, gm, w_in[0].T, tm=2048, tn_first=256, tn=512, name="inproj")

    meta_blk = DB // N_META
    _, st_meta = _gla(us, gads, a_up, a_bias, jnp.zeros((GLA_HEADS, GLA_DK, GLA_DV), F32),
                      batch=1, seq=N_META, C=N_META, sub=1, row_off=meta_blk, name="gla_meta")
    st_meta = st_meta[0]
    o_gla, st_fin = _gla(up, gadp, a_up, a_bias, st_meta, batch=B, seq=S, C=128, sub=2, row_off=0,
                         name="gla")
    o_swa = _swa(sinks, up, us, batch=B, seq=S, meta_row_block=meta_blk, name="swa")

    us3 = us.reshape(n_small, 1, N_MAIN)
    og_s, st_s = _gla_step(us3, gads.reshape(n_small, 1, GLA_RANK),
                           a_up, a_bias, state_gla[0], batch=DB, bb=4, name="gla_step")
    WB = cache_win_k.shape[2]
    assert WB == WINDOW
    os_s, new_k, new_v = _swa_step(
        us[:DB, COL_SQ:COL_SQ + SWA_Q].reshape(DB, SWA_HEADS, SWA_HD),
        cache_win_k[0].reshape(DB, WB, SWA_KV), cache_win_v[0].reshape(DB, WB, SWA_KV), us3,
        cache_meta_k[0].reshape(DB, N_META, SWA_KV), cache_meta_v[0].reshape(DB, N_META, SWA_KV),
        sinks, batch=DB, bb=8, name="swa_step")
    hp2, hs2 = _merge(hp1, o_gla, up, o_swa, hs1, og_s.reshape(DB, GLA_V), us, os_s.reshape(DB, SWA_Q),
                      onorm, gbias, wgo, wso, wout, tm=256, name="merge")
    y_prompt, y_sample = _ffn(hp2, hs2, g2, ffn2_w_in[0], ffn2_w_out[0], gf, tm=1024, tf_first=256, tf=512,
                              final_norm=True, name="ffn2")

    up3 = up.reshape(B, S, N_MAIN)
    kv_shape = (SWA_KV_HEADS, SWA_HD)
    p_meta_k = jnp.broadcast_to(us[DB:, COL_SK:COL_SK + SWA_KV].reshape(1, 1, N_META, *kv_shape),
                                (1, B, N_META, *kv_shape))
    p_meta_v = jnp.broadcast_to(us[DB:, COL_SV:COL_SV + SWA_KV].reshape(1, 1, N_META, *kv_shape),
                                (1, B, N_META, *kv_shape))
    p_win_k = up3[:, S - WINDOW:, COL_SK:COL_SK + SWA_KV].reshape(1, B, WINDOW, *kv_shape)
    p_win_v = up3[:, S - WINDOW:, COL_SV:COL_SV + SWA_KV].reshape(1, B, WINDOW, *kv_shape)
    return (y_prompt.reshape(B, S, D_MODEL), y_sample.reshape(DB, 1, D_MODEL),
            p_meta_k, p_meta_v, p_win_k, p_win_v, st_fin[None],
            new_k.reshape(1, DB, WB, *kv_shape), new_v.reshape(1, DB, WB, *kv_shape), st_s[None])
```

```python
import functools

import jax
import jax.numpy as jnp
from jax import lax
from jax.experimental import pallas as pl
from jax.experimental.pallas import tpu as pltpu

F32 = jnp.float32
BF16 = jnp.bfloat16

D_MODEL = 2048
N_META = 16
GLA_HEADS = 4
GLA_DK = 128
GLA_DV = 256
GLA_RANK = 16
GLA_TAU = 16.0
SWA_HEADS = 16
SWA_KV_HEADS = 4
SWA_GROUP = SWA_HEADS // SWA_KV_HEADS
SWA_HD = 64
WINDOW = 128
D_FF = 5632
EPS = 1e-6

GLA_QK = GLA_HEADS * GLA_DK
GLA_V = GLA_HEADS * GLA_DV
SWA_Q = SWA_HEADS * SWA_HD
SWA_KV = SWA_KV_HEADS * SWA_HD

COL_GA = 0
COL_GB = COL_GA + D_MODEL
COL_GQ = COL_GB + D_MODEL
COL_GK = COL_GQ + GLA_QK
COL_GV = COL_GK + GLA_QK
COL_GR = COL_GV + GLA_V
COL_SQ = COL_GR + GLA_V
COL_SK = COL_SQ + SWA_Q
COL_SV = COL_SK + SWA_KV
N_MAIN = COL_SV + SWA_KV
N_GRP_A = COL_GQ
N_GRP_B = COL_SQ - COL_GQ
N_GRP_C = N_MAIN - COL_SQ

GLA_SUB = 8
MASK_NEG = -1e30
MIB = 1024 * 1024
LANE = 128
SUBLANE = 8
BF16_ROWS = 16
ROW_CHUNK = 256
COL_CHUNK = 512

FFN_TM = 1024
FFN_TF_FIRST = 256
FFN_TF = 512
FFN_VMEM = 58 * MIB
INPROJ_TM = 2048
INPROJ_TN_FIRST = 256
INPROJ_TN = 512
INPROJ_VMEM = 56 * MIB
MERGE_TM = 256
MERGE_VMEM = 56 * MIB
GLA_CHUNK = 128
GLA_CHUNKS_PER_STEP = 2
GLA_STEP_BATCH = 4
SWA_STEP_BATCH = 8
ATTN_VMEM = 32 * MIB

SWA_QB = 64
SWA_SPAN = WINDOW + SWA_QB
SWA_KT = 256
SWA_SINK_COL = SWA_SPAN + N_META


def _rms_scale(x):
    return x * lax.rsqrt(jnp.mean(x * x, axis=-1, keepdims=True) + EPS)


def _log_sigmoid(x):
    return jnp.minimum(x, 0.0) - jnp.log(1.0 + jnp.exp(-jnp.abs(x)))


def _dot(a, b):
    return jnp.dot(a, b, preferred_element_type=F32)


def _dot_nt(a, b):
    return lax.dot_general(a, b, (((1,), (1,)), ((), ())), preferred_element_type=F32)


def _row_to_col(row, n):
    r = lax.broadcasted_iota(jnp.int32, (n, n), 0)
    c = lax.broadcasted_iota(jnp.int32, (n, n), 1)
    return jnp.sum(jnp.where(r == c, jnp.broadcast_to(row, (n, n)), 0.0), axis=1, keepdims=True)


def _for_row_chunks(rows, fn):
    chunk = ROW_CHUNK if rows % ROW_CHUNK == 0 else rows

    def body(i, carry):
        fn(pl.ds(pl.multiple_of(i * chunk, chunk), chunk))
        return carry

    lax.fori_loop(0, rows // chunk, body, 0)


def _row_offsets(refs):
    offs, total = [], 0
    for r in refs:
        offs.append(total)
        total += r.shape[0]
    return offs


def _ffn_rows(j, n_ff, x_refs, o_refs, xn_ref, g_ref, fg_ref, wa, wb, wo, final_norm):
    offs = _row_offsets(x_refs)

    @pl.when(j == 0)
    def _():
        for x_ref, o_ref, off in zip(x_refs, o_refs, offs):
            def norm_rows(sl, x_ref=x_ref, o_ref=o_ref, off=off):
                x = x_ref[sl, :]
                dst = pl.ds(pl.multiple_of(off + sl.start, BF16_ROWS), sl.size)
                xn_ref[dst, :] = (_rms_scale(x) * g_ref[...]).astype(BF16)
                o_ref[sl, :] = x

            _for_row_chunks(x_ref.shape[0], norm_rows)

    xn = xn_ref[...]
    a = _dot(xn, wa)
    b = _dot(xn, wb)
    h = ((0.5 * a) * jax.nn.sigmoid(a) * b).astype(BF16)
    for c in range(D_MODEL // COL_CHUNK):
        cs = slice(COL_CHUNK * c, COL_CHUNK * (c + 1))
        r = _dot(h, wo[:, cs])
        for o_ref, off in zip(o_refs, offs):
            o_ref[:, cs] += r[off:off + o_ref.shape[0]]

    if final_norm:
        @pl.when(j == n_ff - 1)
        def _():
            for o_ref in o_refs:
                def final_rows(sl, o_ref=o_ref):
                    o_ref[sl, :] = _rms_scale(o_ref[sl, :]) * fg_ref[...]

                _for_row_chunks(o_ref.shape[0], final_rows)


def _ffn_first_kernel(x_ref, xs_ref, g_ref, wa_ref, wb_ref, wo_ref, fg_ref,
                      o_ref, os_ref, wa16_ref, wb16_ref, wo16_ref, xn_ref, *, n_ff, final_norm):
    wa16_ref[...] = wa_ref[...].astype(BF16)
    wb16_ref[...] = wb_ref[...].astype(BF16)
    wo16_ref[...] = wo_ref[...].astype(BF16)
    _ffn_rows(pl.program_id(0), n_ff, [x_ref, xs_ref], [o_ref, os_ref], xn_ref, g_ref, fg_ref,
              wa16_ref[...], wb16_ref[...], wo16_ref[...], final_norm)


def _ffn_rest_kernel(x_ref, g_ref, wa_ref, wb_ref, wo_ref, fg_ref, o_first_ref, o_ref, xn_ref,
                     *, n_ff, final_norm):
    del o_first_ref
    _ffn_rows(pl.program_id(1), n_ff, [x_ref], [o_ref], xn_ref, g_ref, fg_ref,
              wa_ref[...], wb_ref[...], wo_ref[...], final_norm)


def _ffn(x, xs, gain, w_in, w_out, final_gain, *, tm, tf_first, tf, final_norm, name):
    rows, rows_s = x.shape[0], xs.shape[0]
    n1 = D_FF // tf_first
    o, os_, wa16, wb16, wo16 = pl.pallas_call(
        functools.partial(_ffn_first_kernel, n_ff=n1, final_norm=final_norm),
        grid=(n1,),
        in_specs=[
            pl.BlockSpec((tm, D_MODEL), lambda j: (0, 0), pipeline_mode=pl.Buffered(1)),
            pl.BlockSpec((rows_s, D_MODEL), lambda j: (0, 0), pipeline_mode=pl.Buffered(1)),
            pl.BlockSpec((1, D_MODEL), lambda j: (0, 0)),
            pl.BlockSpec((D_MODEL, tf_first), lambda j: (0, j)),
            pl.BlockSpec((D_MODEL, tf_first), lambda j: (0, j + n1)),
            pl.BlockSpec((tf_first, D_MODEL), lambda j: (j, 0)),
            pl.BlockSpec((1, D_MODEL), lambda j: (0, 0)),
        ],
        out_specs=[
            pl.BlockSpec((tm, D_MODEL), lambda j: (0, 0)),
            pl.BlockSpec((rows_s, D_MODEL), lambda j: (0, 0)),
            pl.BlockSpec((D_MODEL, tf_first), lambda j: (0, j)),
            pl.BlockSpec((D_MODEL, tf_first), lambda j: (0, j)),
            pl.BlockSpec((tf_first, D_MODEL), lambda j: (j, 0)),
        ],
        out_shape=[
            jax.ShapeDtypeStruct((rows, D_MODEL), F32),
            jax.ShapeDtypeStruct((rows_s, D_MODEL), F32),
            jax.ShapeDtypeStruct((D_MODEL, D_FF), BF16),
            jax.ShapeDtypeStruct((D_MODEL, D_FF), BF16),
            jax.ShapeDtypeStruct((D_FF, D_MODEL), BF16),
        ],
        scratch_shapes=[pltpu.VMEM((tm + rows_s, D_MODEL), BF16)],
        compiler_params=pltpu.CompilerParams(
            dimension_semantics=("arbitrary",), vmem_limit_bytes=FFN_VMEM),
        name=name + "_first",
    )(x, xs, gain, w_in, w_in, w_out, final_gain)

    n2 = D_FF // tf
    o = pl.pallas_call(
        functools.partial(_ffn_rest_kernel, n_ff=n2, final_norm=final_norm),
        grid=(rows // tm - 1, n2),
        in_specs=[
            pl.BlockSpec((tm, D_MODEL), lambda i, j: (i + 1, 0)),
            pl.BlockSpec((1, D_MODEL), lambda i, j: (0, 0)),
            pl.BlockSpec((D_MODEL, tf), lambda i, j: (0, j)),
            pl.BlockSpec((D_MODEL, tf), lambda i, j: (0, j)),
            pl.BlockSpec((tf, D_MODEL), lambda i, j: (j, 0)),
            pl.BlockSpec((1, D_MODEL), lambda i, j: (0, 0)),
            pl.BlockSpec(memory_space=pl.ANY),
        ],
        out_specs=pl.BlockSpec((tm, D_MODEL), lambda i, j: (i + 1, 0)),
        out_shape=jax.ShapeDtypeStruct((rows, D_MODEL), F32),
        input_output_aliases={6: 0},
        scratch_shapes=[pltpu.VMEM((tm, D_MODEL), BF16)],
        compiler_params=pltpu.CompilerParams(
            dimension_semantics=("parallel", "arbitrary"), vmem_limit_bytes=FFN_VMEM),
        name=name + "_rest",
    )(x, gain, wa16, wb16, wo16, final_gain, o)
    return o, os_


def _inproj_rows(j, x_refs, xn_ref, g_ref, w, wg, u_refs, gad_refs):
    offs = _row_offsets(x_refs)

    @pl.when(j == 0)
    def _():
        for x_ref, off in zip(x_refs, offs):
            def norm_rows(sl, x_ref=x_ref, off=off):
                dst = pl.ds(pl.multiple_of(off + sl.start, BF16_ROWS), sl.size)
                xn_ref[dst, :] = (_rms_scale(x_ref[sl, :]) * g_ref[...]).astype(BF16)

            _for_row_chunks(x_ref.shape[0], norm_rows)
        gad = _dot_nt(xn_ref[...], wg)
        for gad_ref, off in zip(gad_refs, offs):
            gad_ref[...] = gad[off:off + gad_ref.shape[0]]

    u = _dot(xn_ref[...], w)
    for u_ref, off in zip(u_refs, offs):
        u_ref[...] = u[off:off + u_ref.shape[0]]


def _inproj_first_kernel(x_ref, xs_ref, g_ref, wm_ref, wx_ref, wg_ref,
                         u_ref, gad_ref, us_ref, gads_ref, w16_ref, wg16_ref,
                         xn_ref, *, na, nb, tn):
    j = pl.program_id(0)
    shifted = (j < na) | (j >= na + nb)

    @pl.when(shifted)
    def _():
        tall = jnp.concatenate([wm_ref[...], wx_ref[...]], axis=0)
        w16_ref[...] = tall[GLA_RANK:GLA_RANK + tn].T.astype(BF16)

    @pl.when(jnp.logical_not(shifted))
    def _():
        w16_ref[...] = wm_ref[...].T.astype(BF16)

    w = w16_ref[...]
    wg = wg_ref[...].astype(BF16)

    @pl.when(j == 0)
    def _():
        wg16_ref[...] = wg

    _inproj_rows(j, [x_ref, xs_ref], xn_ref, g_ref, w, wg, [u_ref, us_ref], [gad_ref, gads_ref])


def _inproj_rest_kernel(x_ref, g_ref, w_ref, wg_ref, u_first_ref, gad_first_ref, u_ref, gad_ref, xn_ref):
    del u_first_ref, gad_first_ref
    _inproj_rows(pl.program_id(1), [x_ref], xn_ref, g_ref, w_ref[...], wg_ref[...], [u_ref], [gad_ref])


def _inproj(x, xs, gain, w_in, *, tm, tn_first, tn, name):
    rows, rows_s = x.shape[0], xs.shape[0]
    na, nb, ncc = N_GRP_A // tn_first, N_GRP_B // tn_first, N_GRP_C // tn_first
    nj = na + nb + ncc
    c_gad = N_GRP_B
    c_sq = c_gad + GLA_RANK
    c_ga = c_sq + N_GRP_C
    assert c_gad % tn_first == 0 and (c_ga - GLA_RANK) % tn_first == 0
    assert tn_first % LANE == 0 and c_gad % GLA_RANK == 0

    def window(j):
        return jnp.where(j < na, (c_ga - GLA_RANK) // tn_first + j,
                         jnp.where(j < na + nb, j - na, c_gad // tn_first + j - na - nb))

    u, gad, us, gads, w16, wg16 = pl.pallas_call(
        functools.partial(_inproj_first_kernel, na=na, nb=nb, tn=tn_first),
        grid=(nj,),
        in_specs=[
            pl.BlockSpec((tm, D_MODEL), lambda j: (0, 0), pipeline_mode=pl.Buffered(1)),
            pl.BlockSpec((rows_s, D_MODEL), lambda j: (0, 0), pipeline_mode=pl.Buffered(1)),
            pl.BlockSpec((1, D_MODEL), lambda j: (0, 0)),
            pl.BlockSpec((tn_first, D_MODEL), lambda j: (window(j), 0)),
            pl.BlockSpec((GLA_RANK, D_MODEL), lambda j: ((window(j) + 1) * (tn_first // GLA_RANK), 0)),
            pl.BlockSpec((GLA_RANK, D_MODEL), lambda j: (c_gad // GLA_RANK, 0)),
        ],
        out_specs=[
            pl.BlockSpec((tm, tn_first), lambda j: (0, j)),
            pl.BlockSpec((tm, GLA_RANK), lambda j: (0, 0)),
            pl.BlockSpec((rows_s, tn_first), lambda j: (0, j)),
            pl.BlockSpec((rows_s, GLA_RANK), lambda j: (0, 0)),
            pl.BlockSpec((D_MODEL, tn_first), lambda j: (0, j)),
            pl.BlockSpec((GLA_RANK, D_MODEL), lambda j: (0, 0)),
        ],
        out_shape=[
            jax.ShapeDtypeStruct((rows, N_MAIN), F32),
            jax.ShapeDtypeStruct((rows, GLA_RANK), F32),
            jax.ShapeDtypeStruct((rows_s, N_MAIN), F32),
            jax.ShapeDtypeStruct((rows_s, GLA_RANK), F32),
            jax.ShapeDtypeStruct((D_MODEL, N_MAIN), BF16),
            jax.ShapeDtypeStruct((GLA_RANK, D_MODEL), BF16),
        ],
        scratch_shapes=[pltpu.VMEM((tm + rows_s, D_MODEL), BF16)],
        compiler_params=pltpu.CompilerParams(
            dimension_semantics=("arbitrary",), vmem_limit_bytes=INPROJ_VMEM),
        name=name + "_first",
    )(x, xs, gain, w_in, w_in, w_in)

    u, gad = pl.pallas_call(
        _inproj_rest_kernel,
        grid=(rows // tm - 1, N_MAIN // tn),
        in_specs=[
            pl.BlockSpec((tm, D_MODEL), lambda i, j: (i + 1, 0), pipeline_mode=pl.Buffered(1)),
            pl.BlockSpec((1, D_MODEL), lambda i, j: (0, 0)),
            pl.BlockSpec((D_MODEL, tn), lambda i, j: (0, j)),
            pl.BlockSpec((GLA_RANK, D_MODEL), lambda i, j: (0, 0)),
            pl.BlockSpec(memory_space=pl.ANY),
            pl.BlockSpec(memory_space=pl.ANY),
        ],
        out_specs=[
            pl.BlockSpec((tm, tn), lambda i, j: (i + 1, j)),
            pl.BlockSpec((tm, GLA_RANK), lambda i, j: (i + 1, 0)),
        ],
        out_shape=[
            jax.ShapeDtypeStruct((rows, N_MAIN), F32),
            jax.ShapeDtypeStruct((rows, GLA_RANK), F32),
        ],
        input_output_aliases={4: 0, 5: 1},
        scratch_shapes=[pltpu.VMEM((tm, D_MODEL), BF16)],
        compiler_params=pltpu.CompilerParams(
            dimension_semantics=("parallel", "arbitrary"), vmem_limit_bytes=INPROJ_VMEM),
        name=name + "_rest",
    )(x, gain, w16, wg16, u, gad)
    return u, gad, us, gads


def _gla_head(q, k, v, b, state, *, C):
    o = _dot((q * jnp.exp(b)).astype(BF16), state.astype(BF16))

    ri = lax.broadcasted_iota(jnp.int32, (C, C), 0)
    ci = lax.broadcasted_iota(jnp.int32, (C, C), 1)
    rowd = lax.broadcasted_iota(jnp.int32, (C, GLA_DK), 0)
    att = None
    m = C // 2
    while m >= GLA_SUB:
        nblk = C // (2 * m)
        pieces = [jnp.broadcast_to(b[i * 2 * m + m - 1:i * 2 * m + m, :], (2 * m, GLA_DK))
                  for i in range(nblk)]
        bref = pieces[0] if nblk == 1 else jnp.concatenate(pieces, axis=0)
        e = jnp.exp(-jnp.abs(b - bref))
        second = (rowd & m) != 0
        ql = jnp.where(second, q * e, 0.0).astype(BF16)
        kl = jnp.where(second, 0.0, k * e).astype(BF16)
        a = _dot_nt(ql, kl)
        if nblk > 1:
            a = jnp.where((ri ^ ci) < 2 * m, a, 0.0)
        att = a if att is None else att + a
        m //= 2

    sub_row = lax.broadcasted_iota(jnp.int32, (GLA_SUB, GLA_DK), 0)
    sub_col = lax.broadcasted_iota(jnp.int32, (GLA_SUB, C), 1)
    diag = []
    for i in range(C // GLA_SUB):
        sl = slice(GLA_SUB * i, GLA_SUB * (i + 1))
        bb, qq, kk = b[sl], q[sl], k[sl]
        blk = jnp.zeros((GLA_SUB, C), F32)
        for s in range(GLA_SUB):
            d = jnp.where(sub_row >= s, bb - bb[s:s + 1], MASK_NEG)
            w = jnp.sum(qq * jnp.exp(d) * kk[s:s + 1], axis=-1, keepdims=True)
            blk = jnp.where(sub_col == GLA_SUB * i + s, w, blk)
        diag.append(blk)
    diag = diag[0] if len(diag) == 1 else jnp.concatenate(diag, axis=0)
    att = diag if att is None else att + diag
    o = o + _dot(att.astype(BF16), v.astype(BF16))

    bl = b[C - 1:C, :]
    kd = k * jnp.exp(bl - b)
    vp = v
    if C < GLA_DK:
        kd = jnp.concatenate([kd, jnp.zeros((GLA_DK - C, GLA_DK), F32)], axis=0)
        vp = jnp.concatenate([v, jnp.zeros((GLA_DK - C, GLA_DV), F32)], axis=0)
    new_state = state * _row_to_col(jnp.exp(bl), GLA_DK) + _dot(kd.T.astype(BF16), vp.astype(BF16))
    return o, new_state


def _gla_kernel(q_ref, k_ref, v_ref, gad_ref, aup_ref, ab_ref, s0_ref, o_ref, sfin_ref, s_ref,
                *, C, sub, nc):
    n = pl.program_id(1)

    @pl.when(n == 0)
    def _():
        s_ref[...] = s0_ref[...]

    def chunk(c, carry):
        rows = pl.ds(pl.multiple_of(c * C, C), C)
        logits = _dot(gad_ref[rows, :].astype(BF16), aup_ref[...].astype(BF16)) + ab_ref[...]
        g = _log_sigmoid(logits) / GLA_TAU

        ri = lax.broadcasted_iota(jnp.int32, (C, C), 0)
        ci = lax.broadcasted_iota(jnp.int32, (C, C), 1)
        tri = jnp.where(ri >= ci, 1.0, 0.0).astype(BF16)
        g_hi = g.astype(BF16)
        r1 = g - g_hi.astype(F32)
        g_mid = r1.astype(BF16)
        g_lo = (r1 - g_mid.astype(F32)).astype(BF16)
        b_all = _dot(tri, g_hi) + _dot(tri, g_mid) + _dot(tri, g_lo)

        for h in range(GLA_HEADS):
            ks = slice(GLA_DK * h, GLA_DK * (h + 1))
            vs = slice(GLA_DV * h, GLA_DV * (h + 1))
            o, new_state = _gla_head(q_ref[rows, ks] * (GLA_DK ** -0.5), k_ref[rows, ks], v_ref[rows, vs],
                                     b_all[:, ks], s_ref[h], C=C)
            o_ref[rows, vs] = o
            s_ref[h] = new_state
        return carry

    if sub == 1:
        chunk(0, 0)
    else:
        lax.fori_loop(0, sub, chunk, 0)

    @pl.when(n == nc - 1)
    def _():
        sfin_ref[0] = s_ref[...]


def _gla(u, gad, a_up, a_bias, s0, *, batch, seq, C, sub, row_off, name):
    rows = C * sub
    nc = seq // rows
    return pl.pallas_call(
        functools.partial(_gla_kernel, C=C, sub=sub, nc=nc),
        grid=(batch, nc),
        in_specs=[
            pl.BlockSpec((rows, GLA_QK), lambda b, n: (row_off + b * nc + n, COL_GQ // GLA_QK)),
            pl.BlockSpec((rows, GLA_QK), lambda b, n: (row_off + b * nc + n, COL_GK // GLA_QK)),
            pl.BlockSpec((rows, GLA_V), lambda b, n: (row_off + b * nc + n, COL_GV // GLA_V)),
            pl.BlockSpec((rows, GLA_RANK), lambda b, n: (row_off + b * nc + n, 0)),
            pl.BlockSpec((GLA_RANK, GLA_QK), lambda b, n: (0, 0)),
            pl.BlockSpec((1, GLA_QK), lambda b, n: (0, 0)),
            pl.BlockSpec((GLA_HEADS, GLA_DK, GLA_DV), lambda b, n: (0, 0, 0)),
        ],
        out_specs=[
            pl.BlockSpec((rows, GLA_V), lambda b, n: (b * nc + n, 0)),
            pl.BlockSpec((1, GLA_HEADS, GLA_DK, GLA_DV), lambda b, n: (b, 0, 0, 0)),
        ],
        out_shape=[
            jax.ShapeDtypeStruct((batch * seq, GLA_V), F32),
            jax.ShapeDtypeStruct((batch, GLA_HEADS, GLA_DK, GLA_DV), F32),
        ],
        scratch_shapes=[pltpu.VMEM((GLA_HEADS, GLA_DK, GLA_DV), F32)],
        compiler_params=pltpu.CompilerParams(
            dimension_semantics=("parallel", "arbitrary"), vmem_limit_bytes=ATTN_VMEM),
        name=name,
    )(u, u, u, gad, a_up, a_bias, s0)


def _gla_step_kernel(q_ref, k_ref, v_ref, gad_ref, aup_ref, ab_ref, s_ref, o_ref, sn_ref, *, bb):
    gl = jnp.concatenate([gad_ref[i] for i in range(bb)] + [jnp.zeros((SUBLANE - bb, GLA_RANK), F32)], axis=0)
    logits = _dot(gl.astype(BF16), aup_ref[...].astype(BF16)) + ab_ref[...]
    dec_all = jnp.exp(_log_sigmoid(logits) / GLA_TAU)
    for i in range(bb):
        q = q_ref[i] * (GLA_DK ** -0.5)
        k = k_ref[i]
        v = v_ref[i]
        dec = dec_all[i:i + 1]
        outs = []
        for h in range(GLA_HEADS):
            sl = slice(GLA_DK * h, GLA_DK * (h + 1))
            new_state = (s_ref[i, h] * _row_to_col(dec[:, sl], GLA_DK)
                         + _row_to_col(k[:, sl], GLA_DK) * v[:, GLA_DV * h:GLA_DV * (h + 1)])
            sn_ref[i, h] = new_state
            outs.append(jnp.sum(_row_to_col(q[:, sl], GLA_DK) * new_state, axis=0, keepdims=True))
        o_ref[i] = jnp.concatenate(outs, axis=1)


def _gla_step(u3, gad3, a_up, a_bias, state, *, batch, bb, name):
    return pl.pallas_call(
        functools.partial(_gla_step_kernel, bb=bb),
        grid=(batch // bb,),
        in_specs=[
            pl.BlockSpec((bb, 1, GLA_QK), lambda b: (b, 0, COL_GQ // GLA_QK)),
            pl.BlockSpec((bb, 1, GLA_QK), lambda b: (b, 0, COL_GK // GLA_QK)),
            pl.BlockSpec((bb, 1, GLA_V), lambda b: (b, 0, COL_GV // GLA_V)),
            pl.BlockSpec((bb, 1, GLA_RANK), lambda b: (b, 0, 0)),
            pl.BlockSpec((GLA_RANK, GLA_QK), lambda b: (0, 0)),
            pl.BlockSpec((1, GLA_QK), lambda b: (0, 0)),
            pl.BlockSpec((bb, GLA_HEADS, GLA_DK, GLA_DV), lambda b: (b, 0, 0, 0)),
        ],
        out_specs=[
            pl.BlockSpec((bb, 1, GLA_V), lambda b: (b, 0, 0)),
            pl.BlockSpec((bb, GLA_HEADS, GLA_DK, GLA_DV), lambda b: (b, 0, 0, 0)),
        ],
        out_shape=[
            jax.ShapeDtypeStruct((batch, 1, GLA_V), F32),
            jax.ShapeDtypeStruct((batch, GLA_HEADS, GLA_DK, GLA_DV), F32),
        ],
        compiler_params=pltpu.CompilerParams(dimension_semantics=("parallel",)),
        name=name,
    )(u3, u3, u3, gad3, a_up, a_bias, state)


def _alibi_slope(head):
    return 2.0 ** (-8.0 * (head + 1) / SWA_HEADS)


def _swa_kernel(sink_ref, q_ref, kc_ref, kp_ref, vc_ref, vp_ref, mk_ref, mv_ref, o_ref, bias_ref):
    n = pl.program_id(1)
    rows = SWA_GROUP * SWA_QB
    nsub = WINDOW // SWA_QB

    @pl.when(n <= 1)
    def _():
        r = lax.broadcasted_iota(jnp.int32, (SWA_QB, SWA_KT), 0)
        c = lax.broadcasted_iota(jnp.int32, (SWA_QB, SWA_KT), 1)
        dist = r + WINDOW - c
        distf = dist.astype(F32)
        band = (dist >= 0) & (dist < WINDOW) & (c < SWA_SPAN)
        for a in range(nsub):
            ok = band & ((c >= WINDOW - SWA_QB * a) | (n > 0))
            for head in range(SWA_HEADS):
                val = jnp.where(ok, -_alibi_slope(head) * distf, MASK_NEG)
                val = jnp.where((c >= SWA_SPAN) & (c < SWA_SINK_COL), 0.0, val)
                val = jnp.where(c == SWA_SINK_COL, sink_ref[head], val)
                tile = (head // SWA_GROUP) * nsub + a
                bias_ref[pl.ds(tile * rows + (head % SWA_GROUP) * SWA_QB, SWA_QB), :] = val

    q = q_ref[...] * (SWA_HD ** -0.5)
    pad = jnp.zeros((SWA_KT - SWA_SPAN - N_META, SWA_HD), BF16)
    scores, values = [], []
    for kvh in range(SWA_KV_HEADS):
        cs = slice(SWA_HD * kvh, SWA_HD * (kvh + 1))
        kwin = jnp.concatenate([kp_ref[:, cs], kc_ref[:, cs]], axis=0).astype(BF16)
        vwin = jnp.concatenate([vp_ref[:, cs], vc_ref[:, cs]], axis=0).astype(BF16)
        km = mk_ref[:, cs].astype(BF16)
        vm = mv_ref[:, cs].astype(BF16)
        for a in range(nsub):
            qs = slice(SWA_QB * a, SWA_QB * (a + 1))
            kt = jnp.concatenate([kwin[SWA_QB * a:SWA_QB * a + SWA_SPAN], km, pad], axis=0)
            values.append(jnp.concatenate([vwin[SWA_QB * a:SWA_QB * a + SWA_SPAN], vm, pad], axis=0))
            qg = jnp.concatenate(
                [q[qs, SWA_HD * (kvh * SWA_GROUP + grp):SWA_HD * (kvh * SWA_GROUP + grp + 1)]
                 for grp in range(SWA_GROUP)], axis=0).astype(BF16)
            scores.append(_dot_nt(qg, kt))
    s = jnp.concatenate(scores, axis=0) + bias_ref[...]
    p = jnp.exp(s - jnp.max(s, axis=-1, keepdims=True))
    inv = 1.0 / jnp.sum(p, axis=-1, keepdims=True)
    p = p.astype(BF16)
    for kvh in range(SWA_KV_HEADS):
        for a in range(nsub):
            tile = kvh * nsub + a
            ts = slice(rows * tile, rows * (tile + 1))
            o = _dot(p[ts], values[tile]) * inv[ts]
            for grp in range(SWA_GROUP):
                head = kvh * SWA_GROUP + grp
                o_ref[SWA_QB * a:SWA_QB * (a + 1), SWA_HD * head:SWA_HD * (head + 1)] = (
                    o[SWA_QB * grp:SWA_QB * (grp + 1)].astype(o_ref.dtype))


def _swa(sinks, u, u_small, *, batch, seq, meta_row_block, name):
    nb = seq // WINDOW
    kcol, vcol = COL_SK // SWA_KV, COL_SV // SWA_KV
    return pl.pallas_call(
        _swa_kernel,
        grid=(batch, nb),
        in_specs=[
            pl.BlockSpec(memory_space=pltpu.SMEM),
            pl.BlockSpec((WINDOW, SWA_Q), lambda b, n: (b * nb + n, COL_SQ // SWA_Q)),
            pl.BlockSpec((WINDOW, SWA_KV), lambda b, n: (b * nb + n, kcol)),
            pl.BlockSpec((WINDOW, SWA_KV), lambda b, n: (b * nb + jnp.maximum(n - 1, 0), kcol)),
            pl.BlockSpec((WINDOW, SWA_KV), lambda b, n: (b * nb + n, vcol)),
            pl.BlockSpec((WINDOW, SWA_KV), lambda b, n: (b * nb + jnp.maximum(n - 1, 0), vcol)),
            pl.BlockSpec((N_META, SWA_KV), lambda b, n: (meta_row_block, kcol)),
            pl.BlockSpec((N_META, SWA_KV), lambda b, n: (meta_row_block, vcol)),
        ],
        out_specs=pl.BlockSpec((WINDOW, SWA_Q), lambda b, n: (b * nb + n, 0)),
        out_shape=jax.ShapeDtypeStruct((batch * seq, SWA_Q), BF16),
        scratch_shapes=[pltpu.VMEM((SWA_HEADS * WINDOW, SWA_KT), F32)],
        compiler_params=pltpu.CompilerParams(
            dimension_semantics=("arbitrary", "arbitrary"), vmem_limit_bytes=ATTN_VMEM),
        name=name,
    )(sinks, u, u, u, u, u, u_small, u_small)


def _swa_step_kernel(q_ref, kold_ref, vold_ref, knew_ref, vnew_ref, mk_ref, mv_ref, sink_ref, slope_ref,
                     o_ref, nk_ref, nv_ref, *, bb):
    rows = bb * SWA_GROUP
    wcols = bb * WINDOW
    ncol = wcols + bb * N_META
    win_shift, meta_shift, grp_shift = WINDOW.bit_length() - 1, N_META.bit_length() - 1, SWA_GROUP.bit_length() - 1
    pos = lax.broadcasted_iota(jnp.int32, (SWA_KV, WINDOW), 1)
    for i in range(bb):
        nk_ref[i] = jnp.where(pos == WINDOW - 1, _row_to_col(knew_ref[i], SWA_KV),
                              pltpu.roll(kold_ref[i], WINDOW - 1, axis=1))
        nv_ref[i] = jnp.where(pos == WINDOW - 1, _row_to_col(vnew_ref[i], SWA_KV),
                              pltpu.roll(vold_ref[i], WINDOW - 1, axis=1))

    r = lax.broadcasted_iota(jnp.int32, (rows, ncol), 0)
    c = lax.broadcasted_iota(jnp.int32, (rows, ncol), 1)
    in_win = c < wcols
    col_batch = jnp.where(in_win, c >> win_shift, (c - wcols) >> meta_shift)
    own = col_batch == (r >> grp_shift)
    distf = jnp.where(in_win, WINDOW - 1 - (c & (WINDOW - 1)), 0).astype(F32)
    for kvh in range(SWA_KV_HEADS):
        cs = slice(SWA_HD * kvh, SWA_HD * (kvh + 1))
        hs = slice(SWA_GROUP * kvh, SWA_GROUP * (kvh + 1))
        k_t = jnp.concatenate([nk_ref[i, cs, :] for i in range(bb)], axis=1).astype(BF16)
        v_t = jnp.concatenate([nv_ref[i, cs, :] for i in range(bb)], axis=1).astype(BF16)
        km = jnp.concatenate([mk_ref[i, :, cs] for i in range(bb)], axis=0).astype(BF16)
        vm = jnp.concatenate([mv_ref[i, :, cs] for i in range(bb)], axis=0).astype(BF16)
        qh = jnp.concatenate([q_ref[i, hs, :] for i in range(bb)], axis=0)
        qh = (qh * (SWA_HD ** -0.5)).astype(BF16)
        s = jnp.concatenate([_dot(qh, k_t), _dot_nt(qh, km)], axis=1)
        s = jnp.where(own, s - slope_ref[kvh] * distf, MASK_NEG)
        sink = sink_ref[kvh]
        mx = jnp.maximum(jnp.max(s, axis=-1, keepdims=True), sink)
        p = jnp.exp(s - mx)
        den = jnp.sum(p, axis=-1, keepdims=True) + jnp.exp(sink - mx)
        p = p.astype(BF16)
        oh = (_dot_nt(p[:, :wcols], v_t) + _dot(p[:, wcols:], vm)) / den
        for i in range(bb):
            o_ref[i, hs, :] = oh[SWA_GROUP * i:SWA_GROUP * (i + 1)]


def _swa_step(q3, k_old, v_old, u3, mk, mv, sinks, *, batch, bb, name):
    def per_row(x16):
        return jnp.tile(x16.reshape(SWA_KV_HEADS, 1, SWA_GROUP), (1, bb, 1)).reshape(
            SWA_KV_HEADS, bb * SWA_GROUP, 1)

    slopes = jnp.asarray([_alibi_slope(h) for h in range(SWA_HEADS)], F32)
    head_spec = pl.BlockSpec((SWA_KV_HEADS, bb * SWA_GROUP, 1), lambda b: (0, 0, 0))
    win_spec = pl.BlockSpec((bb, SWA_KV, WINDOW), lambda b: (b, 0, 0))
    meta_spec = pl.BlockSpec((bb, N_META, SWA_KV), lambda b: (b, 0, 0))
    return pl.pallas_call(
        functools.partial(_swa_step_kernel, bb=bb),
        grid=(batch // bb,),
        in_specs=[
            pl.BlockSpec((bb, SWA_HEADS, SWA_HD), lambda b: (b, 0, 0)),
            win_spec, win_spec,
            pl.BlockSpec((bb, 1, SWA_KV), lambda b: (b, 0, COL_SK // SWA_KV)),
            pl.BlockSpec((bb, 1, SWA_KV), lambda b: (b, 0, COL_SV // SWA_KV)),
            meta_spec, meta_spec, head_spec, head_spec,
        ],
        out_specs=[pl.BlockSpec((bb, SWA_HEADS, SWA_HD), lambda b: (b, 0, 0)), win_spec, win_spec],
        out_shape=[
            jax.ShapeDtypeStruct((batch, SWA_HEADS, SWA_HD), F32),
            jax.ShapeDtypeStruct((batch, SWA_KV, WINDOW), F32),
            jax.ShapeDtypeStruct((batch, SWA_KV, WINDOW), F32),
        ],
        compiler_params=pltpu.CompilerParams(dimension_semantics=("parallel",)),
        name=name,
    )(q3, k_old, v_old, u3, u3, mk, mv, per_row(sinks), per_row(slopes))


def _merge_kernel(h_ref, og_ref, gr_ref, os_ref, ga_ref, gb_ref,
                  hs_ref, ogs_ref, grs_ref, oss_ref, gas_ref, gbs_ref,
                  onorm_ref, gbias_ref, wg_ref, ws_ref, wo_ref, out_ref, outs_ref):
    _merge_rows(h_ref, og_ref, gr_ref, os_ref, ga_ref, gb_ref, onorm_ref, gbias_ref,
                wg_ref, ws_ref, wo_ref, out_ref)

    @pl.when(pl.program_id(0) == 0)
    def _():
        _merge_rows(hs_ref, ogs_ref, grs_ref, oss_ref, gas_ref, gbs_ref, onorm_ref, gbias_ref,
                    wg_ref, ws_ref, wo_ref, outs_ref)


def _merge_rows(h_ref, og_ref, gr_ref, os_ref, ga_ref, gb_ref, onorm_ref, gbias_ref,
                wg_ref, ws_ref, wo_ref, out_ref):
    og = og_ref[...]
    parts = []
    for h in range(GLA_HEADS):
        oh = og[:, GLA_DV * h:GLA_DV * (h + 1)]
        parts.append(oh * lax.rsqrt(jnp.mean(oh * oh, axis=-1, keepdims=True) + EPS))
    gr = gr_ref[...]
    of = jnp.concatenate(parts, axis=1) * onorm_ref[...] * (gr * jax.nn.sigmoid(gr))
    gla_b = _dot(of.astype(BF16), wg_ref[...])
    swa_b = _dot(os_ref[...].astype(BF16), ws_ref[...])
    gbias = gbias_ref[...]
    mix = (jax.nn.sigmoid(ga_ref[...] + gbias[:, :D_MODEL]) * gla_b
           + jax.nn.sigmoid(gb_ref[...] + gbias[:, D_MODEL:]) * swa_b)
    out_ref[...] = h_ref[...] + _dot(mix.astype(BF16), wo_ref[...])


def _merge(h, o_gla, u, o_swa, hs, o_gla_s, us, o_swa_s, onorm, gbias, w_gla_o, w_swa_o, w_out,
           *, tm, name):
    rows, rs = h.shape[0], o_gla_s.shape[0]
    const = dict(pipeline_mode=pl.Buffered(1))

    def row_specs(r, idx):
        return [
            pl.BlockSpec((r, D_MODEL), lambda i: (idx(i), 0)),
            pl.BlockSpec((r, GLA_V), lambda i: (idx(i), 0)),
            pl.BlockSpec((r, GLA_V), lambda i: (idx(i), COL_GR // GLA_V)),
            pl.BlockSpec((r, SWA_Q), lambda i: (idx(i), 0)),
            pl.BlockSpec((r, D_MODEL), lambda i: (idx(i), COL_GA // D_MODEL)),
            pl.BlockSpec((r, D_MODEL), lambda i: (idx(i), COL_GB // D_MODEL)),
        ]

    return pl.pallas_call(
        _merge_kernel,
        grid=(rows // tm,),
        in_specs=row_specs(tm, lambda i: i) + row_specs(rs, lambda i: 0) + [
            pl.BlockSpec((1, GLA_V), lambda i: (0, 0)),
            pl.BlockSpec((1, 2 * D_MODEL), lambda i: (0, 0)),
            pl.BlockSpec((GLA_V, D_MODEL), lambda i: (0, 0), **const),
            pl.BlockSpec((SWA_Q, D_MODEL), lambda i: (0, 0), **const),
            pl.BlockSpec((D_MODEL, D_MODEL), lambda i: (0, 0), **const),
        ],
        out_specs=[
            pl.BlockSpec((tm, D_MODEL), lambda i: (i, 0)),
            pl.BlockSpec((rs, D_MODEL), lambda i: (0, 0)),
        ],
        out_shape=[
            jax.ShapeDtypeStruct((rows, D_MODEL), F32),
            jax.ShapeDtypeStruct((rs, D_MODEL), F32),
        ],
        compiler_params=pltpu.CompilerParams(
            dimension_semantics=("arbitrary",), vmem_limit_bytes=MERGE_VMEM),
        name=name,
    )(h, o_gla, u, o_swa, u, u, hs, o_gla_s, us, o_swa_s, us, us,
      onorm, gbias, w_gla_o, w_swa_o, w_out)


def kernel(x_prompt, x_sample, cache_meta_k, cache_meta_v, cache_win_k, cache_win_v, state_gla,
           meta_tokens, ffn1_norm, ffn1_w_in, ffn1_w_out, mix_norm, w_in, gla_a_up, gla_a_bias,
           gla_out_norm, w_gla_o, swa_sinks, w_swa_o, gate_bias, w_out, ffn2_norm, ffn2_w_in,
           ffn2_w_out, final_norm):
    B, S, _ = x_prompt.shape
    DB = x_sample.shape[0]
    assert x_sample.shape[1] == 1 and ffn1_norm.shape[0] == 1
    n_small = DB + N_META

    wgo, wso, wout = w_gla_o[0].astype(BF16), w_swa_o[0].astype(BF16), w_out[0].astype(BF16)
    g1, gm, g2 = ffn1_norm, mix_norm, ffn2_norm
    gf = final_norm.reshape(1, D_MODEL)
    a_up, a_bias = gla_a_up[0], gla_a_bias
    onorm, gbias = gla_out_norm, gate_bias
    sinks = swa_sinks[0]

    xs = jnp.concatenate([x_sample.reshape(DB, D_MODEL), meta_tokens.astype(F32)], axis=0)
    xp = x_prompt.reshape(B * S, D_MODEL)
    ffn_tiles = dict(tm=FFN_TM, tf_first=FFN_TF_FIRST, tf=FFN_TF)
    hp1, hs1 = _ffn(xp, xs, g1, ffn1_w_in[0], ffn1_w_out[0], gf, final_norm=False, name="ffn1", **ffn_tiles)
    up, gadp, us, gads = _inproj(hp1, hs1, gm, w_in[0].T, tm=INPROJ_TM, tn_first=INPROJ_TN_FIRST,
                                 tn=INPROJ_TN, name="inproj")

    meta_blk = DB // N_META
    _, st_meta = _gla(us, gads, a_up, a_bias, jnp.zeros((GLA_HEADS, GLA_DK, GLA_DV), F32),
                      batch=1, seq=N_META, C=N_META, sub=1, row_off=meta_blk, name="gla_meta")
    st_meta = st_meta[0]
    o_gla, st_fin = _gla(up, gadp, a_up, a_bias, st_meta, batch=B, seq=S, C=GLA_CHUNK,
                         sub=GLA_CHUNKS_PER_STEP, row_off=0, name="gla")
    o_swa = _swa(sinks, up, us, batch=B, seq=S, meta_row_block=meta_blk, name="swa")

    us3 = us.reshape(n_small, 1, N_MAIN)
    og_s, st_s = _gla_step(us3, gads.reshape(n_small, 1, GLA_RANK),
                           a_up, a_bias, state_gla[0], batch=DB, bb=GLA_STEP_BATCH, name="gla_step")
    WB = cache_win_k.shape[2]
    assert WB == WINDOW
    os_s, new_kt, new_vt = _swa_step(
        us[:DB, COL_SQ:COL_SQ + SWA_Q].reshape(DB, SWA_HEADS, SWA_HD),
        jnp.swapaxes(cache_win_k[0].reshape(DB, WB, SWA_KV), 1, 2),
        jnp.swapaxes(cache_win_v[0].reshape(DB, WB, SWA_KV), 1, 2), us3,
        cache_meta_k[0].reshape(DB, N_META, SWA_KV), cache_meta_v[0].reshape(DB, N_META, SWA_KV),
        sinks, batch=DB, bb=SWA_STEP_BATCH, name="swa_step")
    hp2, hs2 = _merge(hp1, o_gla, up, o_swa, hs1, og_s.reshape(DB, GLA_V), us, os_s.reshape(DB, SWA_Q),
                      onorm, gbias, wgo, wso, wout, tm=MERGE_TM, name="merge")
    y_prompt, y_sample = _ffn(hp2, hs2, g2, ffn2_w_in[0], ffn2_w_out[0], gf, final_norm=True, name="ffn2",
                              **ffn_tiles)

    up3 = up.reshape(B, S, N_MAIN)
    kv_shape = (SWA_KV_HEADS, SWA_HD)
    p_meta_k = jnp.broadcast_to(us[DB:, COL_SK:COL_SK + SWA_KV].reshape(1, 1, N_META, *kv_shape),
                                (1, B, N_META, *kv_shape))
    p_meta_v = jnp.broadcast_to(us[DB:, COL_SV:COL_SV + SWA_KV].reshape(1, 1, N_META, *kv_shape),
                                (1, B, N_META, *kv_shape))
    p_win_k = up3[:, S - WINDOW:, COL_SK:COL_SK + SWA_KV].reshape(1, B, WINDOW, *kv_shape)
    p_win_v = up3[:, S - WINDOW:, COL_SV:COL_SV + SWA_KV].reshape(1, B, WINDOW, *kv_shape)
    return (y_prompt.reshape(B, S, D_MODEL), y_sample.reshape(DB, 1, D_MODEL),
            p_meta_k, p_meta_v, p_win_k, p_win_v, st_fin[None],
            jnp.swapaxes(new_kt, 1, 2).reshape(1, DB, WB, *kv_shape),
            jnp.swapaxes(new_vt, 1, 2).reshape(1, DB, WB, *kv_shape), st_s[None])
```

```python
import functools

import jax
import jax.numpy as jnp
from jax import lax
from jax.experimental import pallas as pl
from jax.experimental.pallas import tpu as pltpu

F32 = jnp.float32
BF16 = jnp.bfloat16

D_MODEL = 2048
N_META = 16
GLA_HEADS = 4
GLA_DK = 128
GLA_DV = 256
GLA_RANK = 16
GLA_TAU = 16.0
SWA_HEADS = 16
SWA_KV_HEADS = 4
SWA_GROUP = SWA_HEADS // SWA_KV_HEADS
SWA_HD = 64
WINDOW = 128
D_FF = 5632
EPS = 1e-6

GLA_QK = GLA_HEADS * GLA_DK
GLA_V = GLA_HEADS * GLA_DV
SWA_Q = SWA_HEADS * SWA_HD
SWA_KV = SWA_KV_HEADS * SWA_HD

COL_GA = 0
COL_GB = COL_GA + D_MODEL
COL_GQ = COL_GB + D_MODEL
COL_GK = COL_GQ + GLA_QK
COL_GV = COL_GK + GLA_QK
COL_GR = COL_GV + GLA_V
COL_SQ = COL_GR + GLA_V
COL_SK = COL_SQ + SWA_Q
COL_SV = COL_SK + SWA_KV
N_MAIN = COL_SV + SWA_KV
N_GRP_A = COL_GQ
N_GRP_B = COL_SQ - COL_GQ
N_GRP_C = N_MAIN - COL_SQ

GLA_SUB = 8
MASK_NEG = -1e30
MIB = 1024 * 1024
LANE = 128
SUBLANE = 8
BF16_ROWS = 16
ROW_CHUNK = 256
COL_CHUNK = 512

FFN_TM = 1024
FFN_TF_FIRST = 256
FFN_TF = 512
FFN_VMEM = 58 * MIB
INPROJ_TM = 2048
INPROJ_TN_FIRST = 256
INPROJ_TN = 512
INPROJ_VMEM = 56 * MIB
MERGE_TM = 256
MERGE_VMEM = 56 * MIB
GLA_CHUNK = 256
GLA_CHUNKS_PER_STEP = 1
GLA_STEP_BATCH = 4
SWA_STEP_BATCH = 8
ATTN_VMEM = 32 * MIB

SWA_QB = 64
SWA_SPAN = WINDOW + SWA_QB
SWA_KT = 256
SWA_SINK_COL = SWA_SPAN + N_META


def _rms_scale(x):
    return x * lax.rsqrt(jnp.mean(x * x, axis=-1, keepdims=True) + EPS)


def _log_sigmoid(x):
    return jnp.minimum(x, 0.0) - jnp.log(1.0 + jnp.exp(-jnp.abs(x)))


def _dot(a, b):
    return jnp.dot(a, b, preferred_element_type=F32)


def _dot_nt(a, b):
    return lax.dot_general(a, b, (((1,), (1,)), ((), ())), preferred_element_type=F32)


def _row_to_col(row, n):
    r = lax.broadcasted_iota(jnp.int32, (n, n), 0)
    c = lax.broadcasted_iota(jnp.int32, (n, n), 1)
    return jnp.sum(jnp.where(r == c, jnp.broadcast_to(row, (n, n)), 0.0), axis=1, keepdims=True)


def _for_row_chunks(rows, fn):
    chunk = ROW_CHUNK if rows % ROW_CHUNK == 0 else rows

    def body(i, carry):
        fn(pl.ds(pl.multiple_of(i * chunk, chunk), chunk))
        return carry

    lax.fori_loop(0, rows // chunk, body, 0)


def _row_offsets(refs):
    offs, total = [], 0
    for r in refs:
        offs.append(total)
        total += r.shape[0]
    return offs


def _ffn_rows(j, n_ff, x_refs, o_refs, xn_ref, g_ref, fg_ref, wa, wb, wo, final_norm):
    offs = _row_offsets(x_refs)

    @pl.when(j == 0)
    def _():
        for x_ref, o_ref, off in zip(x_refs, o_refs, offs):
            def norm_rows(sl, x_ref=x_ref, o_ref=o_ref, off=off):
                x = x_ref[sl, :]
                dst = pl.ds(pl.multiple_of(off + sl.start, BF16_ROWS), sl.size)
                xn_ref[dst, :] = (_rms_scale(x) * g_ref[...]).astype(BF16)
                o_ref[sl, :] = x

            _for_row_chunks(x_ref.shape[0], norm_rows)

    xn = xn_ref[...]
    a = _dot(xn, wa)
    b = _dot(xn, wb)
    h = ((0.5 * a) * jax.nn.sigmoid(a) * b).astype(BF16)
    for c in range(D_MODEL // COL_CHUNK):
        cs = slice(COL_CHUNK * c, COL_CHUNK * (c + 1))
        r = _dot(h, wo[:, cs])
        for o_ref, off in zip(o_refs, offs):
            o_ref[:, cs] += r[off:off + o_ref.shape[0]]

    if final_norm:
        @pl.when(j == n_ff - 1)
        def _():
            for o_ref in o_refs:
                def final_rows(sl, o_ref=o_ref):
                    o_ref[sl, :] = _rms_scale(o_ref[sl, :]) * fg_ref[...]

                _for_row_chunks(o_ref.shape[0], final_rows)


def _ffn_first_kernel(x_ref, xs_ref, g_ref, wa_ref, wb_ref, wo_ref, fg_ref,
                      o_ref, os_ref, wa16_ref, wb16_ref, wo16_ref, xn_ref, *, n_ff, final_norm):
    wa16_ref[...] = wa_ref[...].astype(BF16)
    wb16_ref[...] = wb_ref[...].astype(BF16)
    wo16_ref[...] = wo_ref[...].astype(BF16)
    _ffn_rows(pl.program_id(0), n_ff, [x_ref, xs_ref], [o_ref, os_ref], xn_ref, g_ref, fg_ref,
              wa16_ref[...], wb16_ref[...], wo16_ref[...], final_norm)


def _ffn_rest_kernel(x_ref, g_ref, wa_ref, wb_ref, wo_ref, fg_ref, o_first_ref, o_ref, xn_ref,
                     *, n_ff, final_norm):
    del o_first_ref
    _ffn_rows(pl.program_id(1), n_ff, [x_ref], [o_ref], xn_ref, g_ref, fg_ref,
              wa_ref[...], wb_ref[...], wo_ref[...], final_norm)


def _ffn(x, xs, gain, w_in, w_out, final_gain, *, tm, tf_first, tf, final_norm, name):
    rows, rows_s = x.shape[0], xs.shape[0]
    n1 = D_FF // tf_first
    o, os_, wa16, wb16, wo16 = pl.pallas_call(
        functools.partial(_ffn_first_kernel, n_ff=n1, final_norm=final_norm),
        grid=(n1,),
        in_specs=[
            pl.BlockSpec((tm, D_MODEL), lambda j: (0, 0), pipeline_mode=pl.Buffered(1)),
            pl.BlockSpec((rows_s, D_MODEL), lambda j: (0, 0), pipeline_mode=pl.Buffered(1)),
            pl.BlockSpec((1, D_MODEL), lambda j: (0, 0)),
            pl.BlockSpec((D_MODEL, tf_first), lambda j: (0, j)),
            pl.BlockSpec((D_MODEL, tf_first), lambda j: (0, j + n1)),
            pl.BlockSpec((tf_first, D_MODEL), lambda j: (j, 0)),
            pl.BlockSpec((1, D_MODEL), lambda j: (0, 0)),
        ],
        out_specs=[
            pl.BlockSpec((tm, D_MODEL), lambda j: (0, 0)),
            pl.BlockSpec((rows_s, D_MODEL), lambda j: (0, 0)),
            pl.BlockSpec((D_MODEL, tf_first), lambda j: (0, j)),
            pl.BlockSpec((D_MODEL, tf_first), lambda j: (0, j)),
            pl.BlockSpec((tf_first, D_MODEL), lambda j: (j, 0)),
        ],
        out_shape=[
            jax.ShapeDtypeStruct((rows, D_MODEL), F32),
            jax.ShapeDtypeStruct((rows_s, D_MODEL), F32),
            jax.ShapeDtypeStruct((D_MODEL, D_FF), BF16),
            jax.ShapeDtypeStruct((D_MODEL, D_FF), BF16),
            jax.ShapeDtypeStruct((D_FF, D_MODEL), BF16),
        ],
        scratch_shapes=[pltpu.VMEM((tm + rows_s, D_MODEL), BF16)],
        compiler_params=pltpu.CompilerParams(
            dimension_semantics=("arbitrary",), vmem_limit_bytes=FFN_VMEM),
        name=name + "_first",
    )(x, xs, gain, w_in, w_in, w_out, final_gain)

    n2 = D_FF // tf
    o = pl.pallas_call(
        functools.partial(_ffn_rest_kernel, n_ff=n2, final_norm=final_norm),
        grid=(rows // tm - 1, n2),
        in_specs=[
            pl.BlockSpec((tm, D_MODEL), lambda i, j: (i + 1, 0)),
            pl.BlockSpec((1, D_MODEL), lambda i, j: (0, 0)),
            pl.BlockSpec((D_MODEL, tf), lambda i, j: (0, j)),
            pl.BlockSpec((D_MODEL, tf), lambda i, j: (0, j)),
            pl.BlockSpec((tf, D_MODEL), lambda i, j: (j, 0)),
            pl.BlockSpec((1, D_MODEL), lambda i, j: (0, 0)),
            pl.BlockSpec(memory_space=pl.ANY),
        ],
        out_specs=pl.BlockSpec((tm, D_MODEL), lambda i, j: (i + 1, 0)),
        out_shape=jax.ShapeDtypeStruct((rows, D_MODEL), F32),
        input_output_aliases={6: 0},
        scratch_shapes=[pltpu.VMEM((tm, D_MODEL), BF16)],
        compiler_params=pltpu.CompilerParams(
            dimension_semantics=("parallel", "arbitrary"), vmem_limit_bytes=FFN_VMEM),
        name=name + "_rest",
    )(x, gain, wa16, wb16, wo16, final_gain, o)
    return o, os_


def _inproj_rows(j, x_refs, xn_ref, g_ref, w, wg, u_refs, gad_refs):
    offs = _row_offsets(x_refs)

    @pl.when(j == 0)
    def _():
        for x_ref, off in zip(x_refs, offs):
            def norm_rows(sl, x_ref=x_ref, off=off):
                dst = pl.ds(pl.multiple_of(off + sl.start, BF16_ROWS), sl.size)
                xn_ref[dst, :] = (_rms_scale(x_ref[sl, :]) * g_ref[...]).astype(BF16)

            _for_row_chunks(x_ref.shape[0], norm_rows)
        gad = _dot_nt(xn_ref[...], wg)
        for gad_ref, off in zip(gad_refs, offs):
            gad_ref[...] = gad[off:off + gad_ref.shape[0]]

    u = _dot(xn_ref[...], w)
    for u_ref, off in zip(u_refs, offs):
        u_ref[...] = u[off:off + u_ref.shape[0]]


def _inproj_first_kernel(x_ref, xs_ref, g_ref, wm_ref, wx_ref, wg_ref,
                         u_ref, gad_ref, us_ref, gads_ref, w16_ref, wg16_ref,
                         xn_ref, *, na, nb, tn):
    j = pl.program_id(0)
    shifted = (j < na) | (j >= na + nb)

    @pl.when(shifted)
    def _():
        tall = jnp.concatenate([wm_ref[...], wx_ref[...]], axis=0)
        w16_ref[...] = tall[GLA_RANK:GLA_RANK + tn].T.astype(BF16)

    @pl.when(jnp.logical_not(shifted))
    def _():
        w16_ref[...] = wm_ref[...].T.astype(BF16)

    w = w16_ref[...]
    wg = wg_ref[...].astype(BF16)

    @pl.when(j == 0)
    def _():
        wg16_ref[...] = wg

    _inproj_rows(j, [x_ref, xs_ref], xn_ref, g_ref, w, wg, [u_ref, us_ref], [gad_ref, gads_ref])


def _inproj_rest_kernel(x_ref, g_ref, w_ref, wg_ref, u_first_ref, gad_first_ref, u_ref, gad_ref, xn_ref):
    del u_first_ref, gad_first_ref
    _inproj_rows(pl.program_id(1), [x_ref], xn_ref, g_ref, w_ref[...], wg_ref[...], [u_ref], [gad_ref])


def _inproj(x, xs, gain, w_in, *, tm, tn_first, tn, name):
    rows, rows_s = x.shape[0], xs.shape[0]
    na, nb, ncc = N_GRP_A // tn_first, N_GRP_B // tn_first, N_GRP_C // tn_first
    nj = na + nb + ncc
    c_gad = N_GRP_B
    c_sq = c_gad + GLA_RANK
    c_ga = c_sq + N_GRP_C
    assert c_gad % tn_first == 0 and (c_ga - GLA_RANK) % tn_first == 0
    assert tn_first % LANE == 0 and c_gad % GLA_RANK == 0

    def window(j):
        return jnp.where(j < na, (c_ga - GLA_RANK) // tn_first + j,
                         jnp.where(j < na + nb, j - na, c_gad // tn_first + j - na - nb))

    u, gad, us, gads, w16, wg16 = pl.pallas_call(
        functools.partial(_inproj_first_kernel, na=na, nb=nb, tn=tn_first),
        grid=(nj,),
        in_specs=[
            pl.BlockSpec((tm, D_MODEL), lambda j: (0, 0), pipeline_mode=pl.Buffered(1)),
            pl.BlockSpec((rows_s, D_MODEL), lambda j: (0, 0), pipeline_mode=pl.Buffered(1)),
            pl.BlockSpec((1, D_MODEL), lambda j: (0, 0)),
            pl.BlockSpec((tn_first, D_MODEL), lambda j: (window(j), 0)),
            pl.BlockSpec((GLA_RANK, D_MODEL), lambda j: ((window(j) + 1) * (tn_first // GLA_RANK), 0)),
            pl.BlockSpec((GLA_RANK, D_MODEL), lambda j: (c_gad // GLA_RANK, 0)),
        ],
        out_specs=[
            pl.BlockSpec((tm, tn_first), lambda j: (0, j)),
            pl.BlockSpec((tm, GLA_RANK), lambda j: (0, 0)),
            pl.BlockSpec((rows_s, tn_first), lambda j: (0, j)),
            pl.BlockSpec((rows_s, GLA_RANK), lambda j: (0, 0)),
            pl.BlockSpec((D_MODEL, tn_first), lambda j: (0, j)),
            pl.BlockSpec((GLA_RANK, D_MODEL), lambda j: (0, 0)),
        ],
        out_shape=[
            jax.ShapeDtypeStruct((rows, N_MAIN), F32),
            jax.ShapeDtypeStruct((rows, GLA_RANK), F32),
            jax.ShapeDtypeStruct((rows_s, N_MAIN), F32),
            jax.ShapeDtypeStruct((rows_s, GLA_RANK), F32),
            jax.ShapeDtypeStruct((D_MODEL, N_MAIN), BF16),
            jax.ShapeDtypeStruct((GLA_RANK, D_MODEL), BF16),
        ],
        scratch_shapes=[pltpu.VMEM((tm + rows_s, D_MODEL), BF16)],
        compiler_params=pltpu.CompilerParams(
            dimension_semantics=("arbitrary",), vmem_limit_bytes=INPROJ_VMEM),
        name=name + "_first",
    )(x, xs, gain, w_in, w_in, w_in)

    u, gad = pl.pallas_call(
        _inproj_rest_kernel,
        grid=(rows // tm - 1, N_MAIN // tn),
        in_specs=[
            pl.BlockSpec((tm, D_MODEL), lambda i, j: (i + 1, 0), pipeline_mode=pl.Buffered(1)),
            pl.BlockSpec((1, D_MODEL), lambda i, j: (0, 0)),
            pl.BlockSpec((D_MODEL, tn), lambda i, j: (0, j)),
            pl.BlockSpec((GLA_RANK, D_MODEL), lambda i, j: (0, 0)),
            pl.BlockSpec(memory_space=pl.ANY),
            pl.BlockSpec(memory_space=pl.ANY),
        ],
        out_specs=[
            pl.BlockSpec((tm, tn), lambda i, j: (i + 1, j)),
            pl.BlockSpec((tm, GLA_RANK), lambda i, j: (i + 1, 0)),
        ],
        out_shape=[
            jax.ShapeDtypeStruct((rows, N_MAIN), F32),
            jax.ShapeDtypeStruct((rows, GLA_RANK), F32),
        ],
        input_output_aliases={4: 0, 5: 1},
        scratch_shapes=[pltpu.VMEM((tm, D_MODEL), BF16)],
        compiler_params=pltpu.CompilerParams(
            dimension_semantics=("parallel", "arbitrary"), vmem_limit_bytes=INPROJ_VMEM),
        name=name + "_rest",
    )(x, gain, w16, wg16, u, gad)
    return u, gad, us, gads


def _gla_masks(C):
    ri = lax.broadcasted_iota(jnp.int32, (C, C), 0)
    ci = lax.broadcasted_iota(jnp.int32, (C, C), 1)
    rowd = lax.broadcasted_iota(jnp.int32, (C, GLA_DK), 0)
    sub_row = lax.broadcasted_iota(jnp.int32, (GLA_SUB, GLA_DK), 0)
    levels = []
    m = C // 2
    while m >= GLA_SUB:
        levels.append((m, (rowd & m) != 0, (ri ^ ci) < 2 * m))
        m //= 2
    return dict(levels=levels, causal=[sub_row >= s for s in range(GLA_SUB)],
                sub_col=lax.broadcasted_iota(jnp.int32, (GLA_SUB, C), 1))


def _gla_head(q, k, v, b, state, masks, *, C):
    o = _dot((q * jnp.exp(b)).astype(BF16), state.astype(BF16))

    att = None
    for m, second, same_block in masks["levels"]:
        nblk = C // (2 * m)
        pieces = [jnp.broadcast_to(b[i * 2 * m + m - 1:i * 2 * m + m, :], (2 * m, GLA_DK))
                  for i in range(nblk)]
        bref = pieces[0] if nblk == 1 else jnp.concatenate(pieces, axis=0)
        e = jnp.exp(-jnp.abs(b - bref))
        ql = jnp.where(second, q * e, 0.0).astype(BF16)
        kl = jnp.where(second, 0.0, k * e).astype(BF16)
        a = _dot_nt(ql, kl)
        if nblk > 1:
            a = jnp.where(same_block, a, 0.0)
        att = a if att is None else att + a

    sub_col = masks["sub_col"]
    diag = []
    for i in range(C // GLA_SUB):
        sl = slice(GLA_SUB * i, GLA_SUB * (i + 1))
        bb, qq, kk = b[sl], q[sl], k[sl]
        blk = jnp.zeros((GLA_SUB, C), F32)
        for s in range(GLA_SUB):
            d = jnp.where(masks["causal"][s], bb - bb[s:s + 1], MASK_NEG)
            w = jnp.sum(qq * jnp.exp(d) * kk[s:s + 1], axis=-1, keepdims=True)
            blk = jnp.where(sub_col == GLA_SUB * i + s, w, blk)
        diag.append(blk)
    diag = diag[0] if len(diag) == 1 else jnp.concatenate(diag, axis=0)
    att = diag if att is None else att + diag
    o = o + _dot(att.astype(BF16), v.astype(BF16))

    bl = b[C - 1:C, :]
    kd = k * jnp.exp(bl - b)
    vp = v
    if C < GLA_DK:
        kd = jnp.concatenate([kd, jnp.zeros((GLA_DK - C, GLA_DK), F32)], axis=0)
        vp = jnp.concatenate([v, jnp.zeros((GLA_DK - C, GLA_DV), F32)], axis=0)
    new_state = state * _row_to_col(jnp.exp(bl), GLA_DK) + _dot(kd.T.astype(BF16), vp.astype(BF16))
    return o, new_state


def _gla_kernel(q_ref, k_ref, v_ref, gad_ref, aup_ref, ab_ref, s0_ref, o_ref, sfin_ref, s_ref,
                *, C, sub, nc):
    n = pl.program_id(1)

    @pl.when(n == 0)
    def _():
        s_ref[...] = s0_ref[...]

    def chunk(c, carry):
        rows = pl.ds(pl.multiple_of(c * C, C), C)
        logits = _dot(gad_ref[rows, :].astype(BF16), aup_ref[...].astype(BF16)) + ab_ref[...]
        g = _log_sigmoid(logits) / GLA_TAU

        ri = lax.broadcasted_iota(jnp.int32, (C, C), 0)
        ci = lax.broadcasted_iota(jnp.int32, (C, C), 1)
        tri = jnp.where(ri >= ci, 1.0, 0.0).astype(BF16)
        g_hi = g.astype(BF16)
        r1 = g - g_hi.astype(F32)
        g_mid = r1.astype(BF16)
        g_lo = (r1 - g_mid.astype(F32)).astype(BF16)
        b_all = _dot(tri, g_hi) + _dot(tri, g_mid) + _dot(tri, g_lo)

        masks = _gla_masks(C)
        for h in range(GLA_HEADS):
            ks = slice(GLA_DK * h, GLA_DK * (h + 1))
            vs = slice(GLA_DV * h, GLA_DV * (h + 1))
            o, new_state = _gla_head(q_ref[rows, ks] * (GLA_DK ** -0.5), k_ref[rows, ks], v_ref[rows, vs],
                                     b_all[:, ks], s_ref[h], masks, C=C)
            o_ref[rows, vs] = o
            s_ref[h] = new_state
        return carry

    if sub == 1:
        chunk(0, 0)
    else:
        lax.fori_loop(0, sub, chunk, 0)

    @pl.when(n == nc - 1)
    def _():
        sfin_ref[0] = s_ref[...]


def _gla(u, gad, a_up, a_bias, s0, *, batch, seq, C, sub, row_off, name):
    rows = C * sub
    nc = seq // rows
    return pl.pallas_call(
        functools.partial(_gla_kernel, C=C, sub=sub, nc=nc),
        grid=(batch, nc),
        in_specs=[
            pl.BlockSpec((rows, GLA_QK), lambda b, n: (row_off + b * nc + n, COL_GQ // GLA_QK)),
            pl.BlockSpec((rows, GLA_QK), lambda b, n: (row_off + b * nc + n, COL_GK // GLA_QK)),
            pl.BlockSpec((rows, GLA_V), lambda b, n: (row_off + b * nc + n, COL_GV // GLA_V)),
            pl.BlockSpec((rows, GLA_RANK), lambda b, n: (row_off + b * nc + n, 0)),
            pl.BlockSpec((GLA_RANK, GLA_QK), lambda b, n: (0, 0)),
            pl.BlockSpec((1, GLA_QK), lambda b, n: (0, 0)),
            pl.BlockSpec((GLA_HEADS, GLA_DK, GLA_DV), lambda b, n: (0, 0, 0)),
        ],
        out_specs=[
            pl.BlockSpec((rows, GLA_V), lambda b, n: (b * nc + n, 0)),
            pl.BlockSpec((1, GLA_HEADS, GLA_DK, GLA_DV), lambda b, n: (b, 0, 0, 0)),
        ],
        out_shape=[
            jax.ShapeDtypeStruct((batch * seq, GLA_V), F32),
            jax.ShapeDtypeStruct((batch, GLA_HEADS, GLA_DK, GLA_DV), F32),
        ],
        scratch_shapes=[pltpu.VMEM((GLA_HEADS, GLA_DK, GLA_DV), F32)],
        compiler_params=pltpu.CompilerParams(
            dimension_semantics=("parallel", "arbitrary"), vmem_limit_bytes=ATTN_VMEM),
        name=name,
    )(u, u, u, gad, a_up, a_bias, s0)


def _gla_step_kernel(q_ref, k_ref, v_ref, gad_ref, aup_ref, ab_ref, s_ref, o_ref, sn_ref, *, bb):
    gl = jnp.concatenate([gad_ref[i] for i in range(bb)] + [jnp.zeros((SUBLANE - bb, GLA_RANK), F32)], axis=0)
    logits = _dot(gl.astype(BF16), aup_ref[...].astype(BF16)) + ab_ref[...]
    dec_all = jnp.exp(_log_sigmoid(logits) / GLA_TAU)
    for i in range(bb):
        q = q_ref[i] * (GLA_DK ** -0.5)
        k = k_ref[i]
        v = v_ref[i]
        dec = dec_all[i:i + 1]
        outs = []
        for h in range(GLA_HEADS):
            sl = slice(GLA_DK * h, GLA_DK * (h + 1))
            new_state = (s_ref[i, h] * _row_to_col(dec[:, sl], GLA_DK)
                         + _row_to_col(k[:, sl], GLA_DK) * v[:, GLA_DV * h:GLA_DV * (h + 1)])
            sn_ref[i, h] = new_state
            outs.append(jnp.sum(_row_to_col(q[:, sl], GLA_DK) * new_state, axis=0, keepdims=True))
        o_ref[i] = jnp.concatenate(outs, axis=1)


def _gla_step(u3, gad3, a_up, a_bias, state, *, batch, bb, name):
    return pl.pallas_call(
        functools.partial(_gla_step_kernel, bb=bb),
        grid=(batch // bb,),
        in_specs=[
            pl.BlockSpec((bb, 1, GLA_QK), lambda b: (b, 0, COL_GQ // GLA_QK)),
            pl.BlockSpec((bb, 1, GLA_QK), lambda b: (b, 0, COL_GK // GLA_QK)),
            pl.BlockSpec((bb, 1, GLA_V), lambda b: (b, 0, COL_GV // GLA_V)),
            pl.BlockSpec((bb, 1, GLA_RANK), lambda b: (b, 0, 0)),
            pl.BlockSpec((GLA_RANK, GLA_QK), lambda b: (0, 0)),
            pl.BlockSpec((1, GLA_QK), lambda b: (0, 0)),
            pl.BlockSpec((bb, GLA_HEADS, GLA_DK, GLA_DV), lambda b: (b, 0, 0, 0)),
        ],
        out_specs=[
            pl.BlockSpec((bb, 1, GLA_V), lambda b: (b, 0, 0)),
            pl.BlockSpec((bb, GLA_HEADS, GLA_DK, GLA_DV), lambda b: (b, 0, 0, 0)),
        ],
        out_shape=[
            jax.ShapeDtypeStruct((batch, 1, GLA_V), F32),
            jax.ShapeDtypeStruct((batch, GLA_HEADS, GLA_DK, GLA_DV), F32),
        ],
        compiler_params=pltpu.CompilerParams(dimension_semantics=("parallel",)),
        name=name,
    )(u3, u3, u3, gad3, a_up, a_bias, state)


def _alibi_slope(head):
    return 2.0 ** (-8.0 * (head + 1) / SWA_HEADS)


def _swa_kernel(sink_ref, q_ref, kc_ref, kp_ref, vc_ref, vp_ref, mk_ref, mv_ref, o_ref, bias_ref):
    n = pl.program_id(1)
    rows = SWA_GROUP * SWA_QB
    nsub = WINDOW // SWA_QB

    @pl.when(n <= 1)
    def _():
        r = lax.broadcasted_iota(jnp.int32, (SWA_QB, SWA_KT), 0)
        c = lax.broadcasted_iota(jnp.int32, (SWA_QB, SWA_KT), 1)
        dist = r + WINDOW - c
        distf = dist.astype(F32)
        band = (dist >= 0) & (dist < WINDOW) & (c < SWA_SPAN)
        for a in range(nsub):
            ok = band & ((c >= WINDOW - SWA_QB * a) | (n > 0))
            for head in range(SWA_HEADS):
                val = jnp.where(ok, -_alibi_slope(head) * distf, MASK_NEG)
                val = jnp.where((c >= SWA_SPAN) & (c < SWA_SINK_COL), 0.0, val)
                val = jnp.where(c == SWA_SINK_COL, sink_ref[head], val)
                tile = (head // SWA_GROUP) * nsub + a
                bias_ref[pl.ds(tile * rows + (head % SWA_GROUP) * SWA_QB, SWA_QB), :] = val

    q = q_ref[...] * (SWA_HD ** -0.5)
    pad = jnp.zeros((SWA_KT - SWA_SPAN - N_META, SWA_HD), BF16)
    scores, values = [], []
    for kvh in range(SWA_KV_HEADS):
        cs = slice(SWA_HD * kvh, SWA_HD * (kvh + 1))
        kwin = jnp.concatenate([kp_ref[:, cs], kc_ref[:, cs]], axis=0).astype(BF16)
        vwin = jnp.concatenate([vp_ref[:, cs], vc_ref[:, cs]], axis=0).astype(BF16)
        km = mk_ref[:, cs].astype(BF16)
        vm = mv_ref[:, cs].astype(BF16)
        for a in range(nsub):
            qs = slice(SWA_QB * a, SWA_QB * (a + 1))
            kt = jnp.concatenate([kwin[SWA_QB * a:SWA_QB * a + SWA_SPAN], km, pad], axis=0)
            values.append(jnp.concatenate([vwin[SWA_QB * a:SWA_QB * a + SWA_SPAN], vm, pad], axis=0))
            qg = jnp.concatenate(
                [q[qs, SWA_HD * (kvh * SWA_GROUP + grp):SWA_HD * (kvh * SWA_GROUP + grp + 1)]
                 for grp in range(SWA_GROUP)], axis=0).astype(BF16)
            scores.append(_dot_nt(qg, kt))
    s = jnp.concatenate(scores, axis=0) + bias_ref[...]
    p = jnp.exp(s - jnp.max(s, axis=-1, keepdims=True))
    inv = 1.0 / jnp.sum(p, axis=-1, keepdims=True)
    p = p.astype(BF16)
    for kvh in range(SWA_KV_HEADS):
        for a in range(nsub):
            tile = kvh * nsub + a
            ts = slice(rows * tile, rows * (tile + 1))
            o = _dot(p[ts], values[tile]) * inv[ts]
            for grp in range(SWA_GROUP):
                head = kvh * SWA_GROUP + grp
                o_ref[SWA_QB * a:SWA_QB * (a + 1), SWA_HD * head:SWA_HD * (head + 1)] = (
                    o[SWA_QB * grp:SWA_QB * (grp + 1)].astype(o_ref.dtype))


def _swa(sinks, u, u_small, *, batch, seq, meta_row_block, name):
    nb = seq // WINDOW
    kcol, vcol = COL_SK // SWA_KV, COL_SV // SWA_KV
    return pl.pallas_call(
        _swa_kernel,
        grid=(batch, nb),
        in_specs=[
            pl.BlockSpec(memory_space=pltpu.SMEM),
            pl.BlockSpec((WINDOW, SWA_Q), lambda b, n: (b * nb + n, COL_SQ // SWA_Q)),
            pl.BlockSpec((WINDOW, SWA_KV), lambda b, n: (b * nb + n, kcol)),
            pl.BlockSpec((WINDOW, SWA_KV), lambda b, n: (b * nb + jnp.maximum(n - 1, 0), kcol)),
            pl.BlockSpec((WINDOW, SWA_KV), lambda b, n: (b * nb + n, vcol)),
            pl.BlockSpec((WINDOW, SWA_KV), lambda b, n: (b * nb + jnp.maximum(n - 1, 0), vcol)),
            pl.BlockSpec((N_META, SWA_KV), lambda b, n: (meta_row_block, kcol)),
            pl.BlockSpec((N_META, SWA_KV), lambda b, n: (meta_row_block, vcol)),
        ],
        out_specs=pl.BlockSpec((WINDOW, SWA_Q), lambda b, n: (b * nb + n, 0)),
        out_shape=jax.ShapeDtypeStruct((batch * seq, SWA_Q), BF16),
        scratch_shapes=[pltpu.VMEM((SWA_HEADS * WINDOW, SWA_KT), F32)],
        compiler_params=pltpu.CompilerParams(
            dimension_semantics=("arbitrary", "arbitrary"), vmem_limit_bytes=ATTN_VMEM),
        name=name,
    )(sinks, u, u, u, u, u, u_small, u_small)


def _swa_step_kernel(q_ref, kold_ref, vold_ref, knew_ref, vnew_ref, mk_ref, mv_ref, sink_ref, slope_ref,
                     o_ref, nk_ref, nv_ref, *, bb):
    rows = bb * SWA_GROUP
    wcols = bb * WINDOW
    ncol = wcols + bb * N_META
    win_shift, meta_shift, grp_shift = WINDOW.bit_length() - 1, N_META.bit_length() - 1, SWA_GROUP.bit_length() - 1
    pos = lax.broadcasted_iota(jnp.int32, (SWA_KV, WINDOW), 1)
    knew_t = jnp.concatenate([knew_ref[i] for i in range(bb)], axis=0).T
    vnew_t = jnp.concatenate([vnew_ref[i] for i in range(bb)], axis=0).T
    for i in range(bb):
        nk_ref[i] = jnp.where(pos == WINDOW - 1, knew_t[:, i:i + 1], pltpu.roll(kold_ref[i], WINDOW - 1, axis=1))
        nv_ref[i] = jnp.where(pos == WINDOW - 1, vnew_t[:, i:i + 1], pltpu.roll(vold_ref[i], WINDOW - 1, axis=1))

    r = lax.broadcasted_iota(jnp.int32, (rows, ncol), 0)
    c = lax.broadcasted_iota(jnp.int32, (rows, ncol), 1)
    in_win = c < wcols
    col_batch = jnp.where(in_win, c >> win_shift, (c - wcols) >> meta_shift)
    own = col_batch == (r >> grp_shift)
    distf = jnp.where(in_win, WINDOW - 1 - (c & (WINDOW - 1)), 0).astype(F32)
    for kvh in range(SWA_KV_HEADS):
        cs = slice(SWA_HD * kvh, SWA_HD * (kvh + 1))
        hs = slice(SWA_GROUP * kvh, SWA_GROUP * (kvh + 1))
        k_t = jnp.concatenate([nk_ref[i, cs, :] for i in range(bb)], axis=1).astype(BF16)
        v_t = jnp.concatenate([nv_ref[i, cs, :] for i in range(bb)], axis=1).astype(BF16)
        km = jnp.concatenate([mk_ref[i, :, cs] for i in range(bb)], axis=0).astype(BF16)
        vm = jnp.concatenate([mv_ref[i, :, cs] for i in range(bb)], axis=0).astype(BF16)
        qh = jnp.concatenate([q_ref[i, hs, :] for i in range(bb)], axis=0)
        qh = (qh * (SWA_HD ** -0.5)).astype(BF16)
        s = jnp.concatenate([_dot(qh, k_t), _dot_nt(qh, km)], axis=1)
        s = jnp.where(own, s - slope_ref[kvh] * distf, MASK_NEG)
        sink = sink_ref[kvh]
        mx = jnp.maximum(jnp.max(s, axis=-1, keepdims=True), sink)
        p = jnp.exp(s - mx)
        den = jnp.sum(p, axis=-1, keepdims=True) + jnp.exp(sink - mx)
        p = p.astype(BF16)
        oh = (_dot_nt(p[:, :wcols], v_t) + _dot(p[:, wcols:], vm)) / den
        for i in range(bb):
            o_ref[i, hs, :] = oh[SWA_GROUP * i:SWA_GROUP * (i + 1)]


def _swa_step(q3, k_old, v_old, u3, mk, mv, sinks, *, batch, bb, name):
    def per_row(x16):
        return jnp.tile(x16.reshape(SWA_KV_HEADS, 1, SWA_GROUP), (1, bb, 1)).reshape(
            SWA_KV_HEADS, bb * SWA_GROUP, 1)

    slopes = jnp.asarray([_alibi_slope(h) for h in range(SWA_HEADS)], F32)
    head_spec = pl.BlockSpec((SWA_KV_HEADS, bb * SWA_GROUP, 1), lambda b: (0, 0, 0))
    win_spec = pl.BlockSpec((bb, SWA_KV, WINDOW), lambda b: (b, 0, 0))
    meta_spec = pl.BlockSpec((bb, N_META, SWA_KV), lambda b: (b, 0, 0))
    return pl.pallas_call(
        functools.partial(_swa_step_kernel, bb=bb),
        grid=(batch // bb,),
        in_specs=[
            pl.BlockSpec((bb, SWA_HEADS, SWA_HD), lambda b: (b, 0, 0)),
            win_spec, win_spec,
            pl.BlockSpec((bb, 1, SWA_KV), lambda b: (b, 0, COL_SK // SWA_KV)),
            pl.BlockSpec((bb, 1, SWA_KV), lambda b: (b, 0, COL_SV // SWA_KV)),
            meta_spec, meta_spec, head_spec, head_spec,
        ],
        out_specs=[pl.BlockSpec((bb, SWA_HEADS, SWA_HD), lambda b: (b, 0, 0)), win_spec, win_spec],
        out_shape=[
            jax.ShapeDtypeStruct((batch, SWA_HEADS, SWA_HD), F32),
            jax.ShapeDtypeStruct((batch, SWA_KV, WINDOW), F32),
            jax.ShapeDtypeStruct((batch, SWA_KV, WINDOW), F32),
        ],
        compiler_params=pltpu.CompilerParams(dimension_semantics=("parallel",)),
        name=name,
    )(q3, k_old, v_old, u3, u3, mk, mv, per_row(sinks), per_row(slopes))


def _merge_kernel(h_ref, og_ref, gr_ref, os_ref, ga_ref, gb_ref,
                  hs_ref, ogs_ref, grs_ref, oss_ref, gas_ref, gbs_ref,
                  onorm_ref, gbias_ref, wg_ref, ws_ref, wo_ref, out_ref, outs_ref):
    _merge_rows(h_ref, og_ref, gr_ref, os_ref, ga_ref, gb_ref, onorm_ref, gbias_ref,
                wg_ref, ws_ref, wo_ref, out_ref)

    @pl.when(pl.program_id(0) == 0)
    def _():
        _merge_rows(hs_ref, ogs_ref, grs_ref, oss_ref, gas_ref, gbs_ref, onorm_ref, gbias_ref,
                    wg_ref, ws_ref, wo_ref, outs_ref)


def _merge_rows(h_ref, og_ref, gr_ref, os_ref, ga_ref, gb_ref, onorm_ref, gbias_ref,
                wg_ref, ws_ref, wo_ref, out_ref):
    og = og_ref[...]
    parts = []
    for h in range(GLA_HEADS):
        oh = og[:, GLA_DV * h:GLA_DV * (h + 1)]
        parts.append(oh * lax.rsqrt(jnp.mean(oh * oh, axis=-1, keepdims=True) + EPS))
    gr = gr_ref[...]
    of = jnp.concatenate(parts, axis=1) * onorm_ref[...] * (gr * jax.nn.sigmoid(gr))
    gla_b = _dot(of.astype(BF16), wg_ref[...])
    swa_b = _dot(os_ref[...].astype(BF16), ws_ref[...])
    gbias = gbias_ref[...]
    mix = (jax.nn.sigmoid(ga_ref[...] + gbias[:, :D_MODEL]) * gla_b
           + jax.nn.sigmoid(gb_ref[...] + gbias[:, D_MODEL:]) * swa_b)
    out_ref[...] = h_ref[...] + _dot(mix.astype(BF16), wo_ref[...])


def _merge(h, o_gla, u, o_swa, hs, o_gla_s, us, o_swa_s, onorm, gbias, w_gla_o, w_swa_o, w_out,
           *, tm, name):
    rows, rs = h.shape[0], o_gla_s.shape[0]
    const = dict(pipeline_mode=pl.Buffered(1))

    def row_specs(r, idx):
        return [
            pl.BlockSpec((r, D_MODEL), lambda i: (idx(i), 0)),
            pl.BlockSpec((r, GLA_V), lambda i: (idx(i), 0)),
            pl.BlockSpec((r, GLA_V), lambda i: (idx(i), COL_GR // GLA_V)),
            pl.BlockSpec((r, SWA_Q), lambda i: (idx(i), 0)),
            pl.BlockSpec((r, D_MODEL), lambda i: (idx(i), COL_GA // D_MODEL)),
            pl.BlockSpec((r, D_MODEL), lambda i: (idx(i), COL_GB // D_MODEL)),
        ]

    return pl.pallas_call(
        _merge_kernel,
        grid=(rows // tm,),
        in_specs=row_specs(tm, lambda i: i) + row_specs(rs, lambda i: 0) + [
            pl.BlockSpec((1, GLA_V), lambda i: (0, 0)),
            pl.BlockSpec((1, 2 * D_MODEL), lambda i: (0, 0)),
            pl.BlockSpec((GLA_V, D_MODEL), lambda i: (0, 0), **const),
            pl.BlockSpec((SWA_Q, D_MODEL), lambda i: (0, 0), **const),
            pl.BlockSpec((D_MODEL, D_MODEL), lambda i: (0, 0), **const),
        ],
        out_specs=[
            pl.BlockSpec((tm, D_MODEL), lambda i: (i, 0)),
            pl.BlockSpec((rs, D_MODEL), lambda i: (0, 0)),
        ],
        out_shape=[
            jax.ShapeDtypeStruct((rows, D_MODEL), F32),
            jax.ShapeDtypeStruct((rs, D_MODEL), F32),
        ],
        compiler_params=pltpu.CompilerParams(
            dimension_semantics=("arbitrary",), vmem_limit_bytes=MERGE_VMEM),
        name=name,
    )(h, o_gla, u, o_swa, u, u, hs, o_gla_s, us, o_swa_s, us, us,
      onorm, gbias, w_gla_o, w_swa_o, w_out)


def kernel(x_prompt, x_sample, cache_meta_k, cache_meta_v, cache_win_k, cache_win_v, state_gla,
           meta_tokens, ffn1_norm, ffn1_w_in, ffn1_w_out, mix_norm, w_in, gla_a_up, gla_a_bias,
           gla_out_norm, w_gla_o, swa_sinks, w_swa_o, gate_bias, w_out, ffn2_norm, ffn2_w_in,
           ffn2_w_out, final_norm):
    B, S, _ = x_prompt.shape
    DB = x_sample.shape[0]
    assert x_sample.shape[1] == 1 and ffn1_norm.shape[0] == 1
    n_small = DB + N_META

    wgo, wso, wout = w_gla_o[0].astype(BF16), w_swa_o[0].astype(BF16), w_out[0].astype(BF16)
    g1, gm, g2 = ffn1_norm, mix_norm, ffn2_norm
    gf = final_norm.reshape(1, D_MODEL)
    a_up, a_bias = gla_a_up[0], gla_a_bias
    onorm, gbias = gla_out_norm, gate_bias
    sinks = swa_sinks[0]

    xs = jnp.concatenate([x_sample.reshape(DB, D_MODEL), meta_tokens.astype(F32)], axis=0)
    xp = x_prompt.reshape(B * S, D_MODEL)
    ffn_tiles = dict(tm=FFN_TM, tf_first=FFN_TF_FIRST, tf=FFN_TF)
    hp1, hs1 = _ffn(xp, xs, g1, ffn1_w_in[0], ffn1_w_out[0], gf, final_norm=False, name="ffn1", **ffn_tiles)
    up, gadp, us, gads = _inproj(hp1, hs1, gm, w_in[0].T, tm=INPROJ_TM, tn_first=INPROJ_TN_FIRST,
                                 tn=INPROJ_TN, name="inproj")

    meta_blk = DB // N_META
    _, st_meta = _gla(us, gads, a_up, a_bias, jnp.zeros((GLA_HEADS, GLA_DK, GLA_DV), F32),
                      batch=1, seq=N_META, C=N_META, sub=1, row_off=meta_blk, name="gla_meta")
    st_meta = st_meta[0]
    o_gla, st_fin = _gla(up, gadp, a_up, a_bias, st_meta, batch=B, seq=S, C=GLA_CHUNK,
                         sub=GLA_CHUNKS_PER_STEP, row_off=0, name="gla")
    o_swa = _swa(sinks, up, us, batch=B, seq=S, meta_row_block=meta_blk, name="swa")

    us3 = us.reshape(n_small, 1, N_MAIN)
    og_s, st_s = _gla_step(us3, gads.reshape(n_small, 1, GLA_RANK),
                           a_up, a_bias, state_gla[0], batch=DB, bb=GLA_STEP_BATCH, name="gla_step")
    WB = cache_win_k.shape[2]
    assert WB == WINDOW
    os_s, new_kt, new_vt = _swa_step(
        us[:DB, COL_SQ:COL_SQ + SWA_Q].reshape(DB, SWA_HEADS, SWA_HD),
        jnp.swapaxes(cache_win_k[0].reshape(DB, WB, SWA_KV), 1, 2),
        jnp.swapaxes(cache_win_v[0].reshape(DB, WB, SWA_KV), 1, 2), us3,
        cache_meta_k[0].reshape(DB, N_META, SWA_KV), cache_meta_v[0].reshape(DB, N_META, SWA_KV),
        sinks, batch=DB, bb=SWA_STEP_BATCH, name="swa_step")
    hp2, hs2 = _merge(hp1, o_gla, up, o_swa, hs1, og_s.reshape(DB, GLA_V), us, os_s.reshape(DB, SWA_Q),
                      onorm, gbias, wgo, wso, wout, tm=MERGE_TM, name="merge")
    y_prompt, y_sample = _ffn(hp2, hs2, g2, ffn2_w_in[0], ffn2_w_out[0], gf, final_norm=True, name="ffn2",
                              **ffn_tiles)

    up3 = up.reshape(B, S, N_MAIN)
    kv_shape = (SWA_KV_HEADS, SWA_HD)
    p_meta_k = jnp.broadcast_to(us[DB:, COL_SK:COL_SK + SWA_KV].reshape(1, 1, N_META, *kv_shape),
                                (1, B, N_META, *kv_shape))
    p_meta_v = jnp.broadcast_to(us[DB:, COL_SV:COL_SV + SWA_KV].reshape(1, 1, N_META, *kv_shape),
                                (1, B, N_META, *kv_shape))
    p_win_k = up3[:, S - WINDOW:, COL_SK:COL_SK + SWA_KV].reshape(1, B, WINDOW, *kv_shape)
    p_win_v = up3[:, S - WINDOW:, COL_SV:COL_SV + SWA_KV].reshape(1, B, WINDOW, *kv_shape)
    return (y_prompt.reshape(B, S, D_MODEL), y_sample.reshape(DB, 1, D_MODEL),
            p_meta_k, p_meta_v, p_win_k, p_win_v, st_fin[None],
            jnp.swapaxes(new_kt, 1, 2).reshape(1, DB, WB, *kv_shape),
            jnp.swapaxes(new_vt, 1, 2).reshape(1, DB, WB, *kv_shape), st_s[None])
```

```python
import functools

import jax
import jax.numpy as jnp
from jax import lax
from jax.experimental import pallas as pl
from jax.experimental.pallas import tpu as pltpu

F32 = jnp.float32
BF16 = jnp.bfloat16

D_MODEL = 2048
N_META = 16
GLA_HEADS = 4
GLA_DK = 128
GLA_DV = 256
GLA_RANK = 16
GLA_TAU = 16.0
SWA_HEADS = 16
SWA_KV_HEADS = 4
SWA_GROUP = SWA_HEADS // SWA_KV_HEADS
SWA_HD = 64
WINDOW = 128
D_FF = 5632
EPS = 1e-6

GLA_QK = GLA_HEADS * GLA_DK
GLA_V = GLA_HEADS * GLA_DV
SWA_Q = SWA_HEADS * SWA_HD
SWA_KV = SWA_KV_HEADS * SWA_HD

COL_GA = 0
COL_GB = COL_GA + D_MODEL
COL_GQ = COL_GB + D_MODEL
COL_GK = COL_GQ + GLA_QK
COL_GV = COL_GK + GLA_QK
COL_GR = COL_GV + GLA_V
COL_SQ = COL_GR + GLA_V
COL_SK = COL_SQ + SWA_Q
COL_SV = COL_SK + SWA_KV
N_MAIN = COL_SV + SWA_KV
N_GRP_A = COL_GQ
N_GRP_B = COL_SQ - COL_GQ
N_GRP_C = N_MAIN - COL_SQ

GLA_SUB = 8
MASK_NEG = -1e30
MIB = 1024 * 1024
LANE = 128
SUBLANE = 8
BF16_ROWS = 16
ROW_CHUNK = 256
COL_CHUNK = 512

FFN_TM = 1024
FFN_TF_FIRST = 256
FFN_TF = 512
FFN_VMEM = 58 * MIB
INPROJ_TM = 2048
INPROJ_TN_FIRST = 512
INPROJ_TN = 512
INPROJ_VMEM = 56 * MIB
MERGE_TM = 256
MERGE_VMEM = 56 * MIB
GLA_CHUNK = 256
GLA_CHUNKS_PER_STEP = 1
GLA_STEP_BATCH = 4
SWA_STEP_BATCH = 8
ATTN_VMEM = 32 * MIB

SWA_QB = 64
SWA_SPAN = WINDOW + SWA_QB
SWA_KT = 256
SWA_SINK_COL = SWA_SPAN + N_META


def _rms_scale(x):
    return x * lax.rsqrt(jnp.mean(x * x, axis=-1, keepdims=True) + EPS)


def _log_sigmoid(x):
    return jnp.minimum(x, 0.0) - jnp.log(1.0 + jnp.exp(-jnp.abs(x)))


def _dot(a, b):
    return jnp.dot(a, b, preferred_element_type=F32)


def _dot_nt(a, b):
    return lax.dot_general(a, b, (((1,), (1,)), ((), ())), preferred_element_type=F32)


def _row_to_col(row, n):
    r = lax.broadcasted_iota(jnp.int32, (n, n), 0)
    c = lax.broadcasted_iota(jnp.int32, (n, n), 1)
    return jnp.sum(jnp.where(r == c, jnp.broadcast_to(row, (n, n)), 0.0), axis=1, keepdims=True)


def _for_row_chunks(rows, fn):
    chunk = ROW_CHUNK if rows % ROW_CHUNK == 0 else rows

    def body(i, carry):
        fn(pl.ds(pl.multiple_of(i * chunk, chunk), chunk))
        return carry

    lax.fori_loop(0, rows // chunk, body, 0)


def _row_offsets(refs):
    offs, total = [], 0
    for r in refs:
        offs.append(total)
        total += r.shape[0]
    return offs


def _ffn_rows(j, n_ff, x_refs, o_refs, xn_ref, g_ref, fg_ref, wa, wb, wo, final_norm):
    offs = _row_offsets(x_refs)

    @pl.when(j == 0)
    def _():
        for x_ref, o_ref, off in zip(x_refs, o_refs, offs):
            def norm_rows(sl, x_ref=x_ref, o_ref=o_ref, off=off):
                x = x_ref[sl, :]
                dst = pl.ds(pl.multiple_of(off + sl.start, BF16_ROWS), sl.size)
                xn_ref[dst, :] = (_rms_scale(x) * g_ref[...]).astype(BF16)
                o_ref[sl, :] = x

            _for_row_chunks(x_ref.shape[0], norm_rows)

    xn = xn_ref[...]
    for t in range(wa.shape[0]):
        a = _dot(xn, wa[t])
        b = _dot(xn, wb[t])
        h = ((0.5 * a) * jax.nn.sigmoid(a) * b).astype(BF16)
        for c in range(D_MODEL // COL_CHUNK):
            cs = slice(COL_CHUNK * c, COL_CHUNK * (c + 1))
            r = _dot(h, wo[t][:, cs])
            for o_ref, off in zip(o_refs, offs):
                o_ref[:, cs] += r[off:off + o_ref.shape[0]]

    if final_norm:
        @pl.when(j == n_ff - 1)
        def _():
            for o_ref in o_refs:
                def final_rows(sl, o_ref=o_ref):
                    o_ref[sl, :] = _rms_scale(o_ref[sl, :]) * fg_ref[...]

                _for_row_chunks(o_ref.shape[0], final_rows)


def _ffn_first_kernel(x_ref, xs_ref, g_ref, wa_ref, wb_ref, wo_ref, fg_ref,
                      o_ref, os_ref, wa16_ref, wb16_ref, wo16_ref, xn_ref, *, n_ff, final_norm):
    wa16_ref[0] = wa_ref[...].astype(BF16)
    wb16_ref[0] = wb_ref[...].astype(BF16)
    wo16_ref[0] = wo_ref[...].astype(BF16)
    _ffn_rows(pl.program_id(0), n_ff, [x_ref, xs_ref], [o_ref, os_ref], xn_ref, g_ref, fg_ref,
              wa16_ref[...], wb16_ref[...], wo16_ref[...], final_norm)


def _ffn_rest_kernel(x_ref, g_ref, wa_ref, wb_ref, wo_ref, fg_ref, o_first_ref, o_ref, xn_ref,
                     *, n_ff, final_norm):
    del o_first_ref
    _ffn_rows(pl.program_id(1), n_ff, [x_ref], [o_ref], xn_ref, g_ref, fg_ref,
              wa_ref[...], wb_ref[...], wo_ref[...], final_norm)


def _ffn(x, xs, gain, w_in, w_out, final_gain, *, tm, tf_first, tf, final_norm, name):
    rows, rows_s = x.shape[0], xs.shape[0]
    n1 = D_FF // tf_first
    o, os_, wa16, wb16, wo16 = pl.pallas_call(
        functools.partial(_ffn_first_kernel, n_ff=n1, final_norm=final_norm),
        grid=(n1,),
        in_specs=[
            pl.BlockSpec((tm, D_MODEL), lambda j: (0, 0), pipeline_mode=pl.Buffered(1)),
            pl.BlockSpec((rows_s, D_MODEL), lambda j: (0, 0), pipeline_mode=pl.Buffered(1)),
            pl.BlockSpec((1, D_MODEL), lambda j: (0, 0)),
            pl.BlockSpec((D_MODEL, tf_first), lambda j: (0, j)),
            pl.BlockSpec((D_MODEL, tf_first), lambda j: (0, j + n1)),
            pl.BlockSpec((tf_first, D_MODEL), lambda j: (j, 0)),
            pl.BlockSpec((1, D_MODEL), lambda j: (0, 0)),
        ],
        out_specs=[
            pl.BlockSpec((tm, D_MODEL), lambda j: (0, 0)),
            pl.BlockSpec((rows_s, D_MODEL), lambda j: (0, 0)),
            pl.BlockSpec((1, D_MODEL, tf_first), lambda j: (j, 0, 0)),
            pl.BlockSpec((1, D_MODEL, tf_first), lambda j: (j, 0, 0)),
            pl.BlockSpec((1, tf_first, D_MODEL), lambda j: (j, 0, 0)),
        ],
        out_shape=[
            jax.ShapeDtypeStruct((rows, D_MODEL), F32),
            jax.ShapeDtypeStruct((rows_s, D_MODEL), F32),
            jax.ShapeDtypeStruct((n1, D_MODEL, tf_first), BF16),
            jax.ShapeDtypeStruct((n1, D_MODEL, tf_first), BF16),
            jax.ShapeDtypeStruct((n1, tf_first, D_MODEL), BF16),
        ],
        scratch_shapes=[pltpu.VMEM((tm + rows_s, D_MODEL), BF16)],
        compiler_params=pltpu.CompilerParams(
            dimension_semantics=("arbitrary",), vmem_limit_bytes=FFN_VMEM),
        name=name + "_first",
    )(x, xs, gain, w_in, w_in, w_out, final_gain)

    n2 = D_FF // tf
    per = tf // tf_first
    o = pl.pallas_call(
        functools.partial(_ffn_rest_kernel, n_ff=n2, final_norm=final_norm),
        grid=(rows // tm - 1, n2),
        in_specs=[
            pl.BlockSpec((tm, D_MODEL), lambda i, j: (i + 1, 0)),
            pl.BlockSpec((1, D_MODEL), lambda i, j: (0, 0)),
            pl.BlockSpec((per, D_MODEL, tf_first), lambda i, j: (j, 0, 0)),
            pl.BlockSpec((per, D_MODEL, tf_first), lambda i, j: (j, 0, 0)),
            pl.BlockSpec((per, tf_first, D_MODEL), lambda i, j: (j, 0, 0)),
            pl.BlockSpec((1, D_MODEL), lambda i, j: (0, 0)),
            pl.BlockSpec(memory_space=pl.ANY),
        ],
        out_specs=pl.BlockSpec((tm, D_MODEL), lambda i, j: (i + 1, 0)),
        out_shape=jax.ShapeDtypeStruct((rows, D_MODEL), F32),
        input_output_aliases={6: 0},
        scratch_shapes=[pltpu.VMEM((tm, D_MODEL), BF16)],
        compiler_params=pltpu.CompilerParams(
            dimension_semantics=("parallel", "arbitrary"), vmem_limit_bytes=FFN_VMEM),
        name=name + "_rest",
    )(x, gain, wa16, wb16, wo16, final_gain, o)
    return o, os_


def _inproj_rows(j, x_refs, xn_ref, g_ref, w, wg, u_refs, gad_refs):
    offs = _row_offsets(x_refs)

    @pl.when(j == 0)
    def _():
        for x_ref, off in zip(x_refs, offs):
            def norm_rows(sl, x_ref=x_ref, off=off):
                dst = pl.ds(pl.multiple_of(off + sl.start, BF16_ROWS), sl.size)
                xn_ref[dst, :] = (_rms_scale(x_ref[sl, :]) * g_ref[...]).astype(BF16)

            _for_row_chunks(x_ref.shape[0], norm_rows)
        gad = _dot_nt(xn_ref[...], wg)
        for gad_ref, off in zip(gad_refs, offs):
            gad_ref[...] = gad[off:off + gad_ref.shape[0]]

    u = _dot(xn_ref[...], w)
    for u_ref, off in zip(u_refs, offs):
        u_ref[...] = u[off:off + u_ref.shape[0]]


def _inproj_first_kernel(x_ref, xs_ref, g_ref, wm_ref, wx_ref, wg_ref,
                         u_ref, gad_ref, us_ref, gads_ref, w16_ref, wg16_ref,
                         xn_ref, *, na, nb, tn):
    j = pl.program_id(0)
    shifted = (j < na) | (j >= na + nb)

    @pl.when(shifted)
    def _():
        tall = jnp.concatenate([wm_ref[...], wx_ref[...]], axis=0)
        w16_ref[...] = tall[GLA_RANK:GLA_RANK + tn].T.astype(BF16)

    @pl.when(jnp.logical_not(shifted))
    def _():
        w16_ref[...] = wm_ref[...].T.astype(BF16)

    w = w16_ref[...]
    wg = wg_ref[...].astype(BF16)

    @pl.when(j == 0)
    def _():
        wg16_ref[...] = wg

    _inproj_rows(j, [x_ref, xs_ref], xn_ref, g_ref, w, wg, [u_ref, us_ref], [gad_ref, gads_ref])


def _inproj_rest_kernel(x_ref, g_ref, w_ref, wg_ref, u_first_ref, gad_first_ref, u_ref, gad_ref, xn_ref):
    del u_first_ref, gad_first_ref
    _inproj_rows(pl.program_id(1), [x_ref], xn_ref, g_ref, w_ref[...], wg_ref[...], [u_ref], [gad_ref])


def _inproj(x, xs, gain, w_in, *, tm, tn_first, tn, name):
    rows, rows_s = x.shape[0], xs.shape[0]
    na, nb, ncc = N_GRP_A // tn_first, N_GRP_B // tn_first, N_GRP_C // tn_first
    nj = na + nb + ncc
    c_gad = N_GRP_B
    c_sq = c_gad + GLA_RANK
    c_ga = c_sq + N_GRP_C
    assert c_gad % tn_first == 0 and (c_ga - GLA_RANK) % tn_first == 0
    assert tn_first % LANE == 0 and c_gad % GLA_RANK == 0

    def window(j):
        return jnp.where(j < na, (c_ga - GLA_RANK) // tn_first + j,
                         jnp.where(j < na + nb, j - na, c_gad // tn_first + j - na - nb))

    u, gad, us, gads, w16, wg16 = pl.pallas_call(
        functools.partial(_inproj_first_kernel, na=na, nb=nb, tn=tn_first),
        grid=(nj,),
        in_specs=[
            pl.BlockSpec((tm, D_MODEL), lambda j: (0, 0), pipeline_mode=pl.Buffered(1)),
            pl.BlockSpec((rows_s, D_MODEL), lambda j: (0, 0), pipeline_mode=pl.Buffered(1)),
            pl.BlockSpec((1, D_MODEL), lambda j: (0, 0)),
            pl.BlockSpec((tn_first, D_MODEL), lambda j: (window(j), 0)),
            pl.BlockSpec((GLA_RANK, D_MODEL), lambda j: ((window(j) + 1) * (tn_first // GLA_RANK), 0)),
            pl.BlockSpec((GLA_RANK, D_MODEL), lambda j: (c_gad // GLA_RANK, 0)),
        ],
        out_specs=[
            pl.BlockSpec((tm, tn_first), lambda j: (0, j)),
            pl.BlockSpec((tm, GLA_RANK), lambda j: (0, 0)),
            pl.BlockSpec((rows_s, tn_first), lambda j: (0, j)),
            pl.BlockSpec((rows_s, GLA_RANK), lambda j: (0, 0)),
            pl.BlockSpec((D_MODEL, tn_first), lambda j: (0, j)),
            pl.BlockSpec((GLA_RANK, D_MODEL), lambda j: (0, 0)),
        ],
        out_shape=[
            jax.ShapeDtypeStruct((rows, N_MAIN), F32),
            jax.ShapeDtypeStruct((rows, GLA_RANK), F32),
            jax.ShapeDtypeStruct((rows_s, N_MAIN), F32),
            jax.ShapeDtypeStruct((rows_s, GLA_RANK), F32),
            jax.ShapeDtypeStruct((D_MODEL, N_MAIN), BF16),
            jax.ShapeDtypeStruct((GLA_RANK, D_MODEL), BF16),
        ],
        scratch_shapes=[pltpu.VMEM((tm + rows_s, D_MODEL), BF16)],
        compiler_params=pltpu.CompilerParams(
            dimension_semantics=("arbitrary",), vmem_limit_bytes=INPROJ_VMEM),
        name=name + "_first",
    )(x, xs, gain, w_in, w_in, w_in)

    u, gad = pl.pallas_call(
        _inproj_rest_kernel,
        grid=(rows // tm - 1, N_MAIN // tn),
        in_specs=[
            pl.BlockSpec((tm, D_MODEL), lambda i, j: (i + 1, 0), pipeline_mode=pl.Buffered(1)),
            pl.BlockSpec((1, D_MODEL), lambda i, j: (0, 0)),
            pl.BlockSpec((D_MODEL, tn), lambda i, j: (0, j)),
            pl.BlockSpec((GLA_RANK, D_MODEL), lambda i, j: (0, 0)),
            pl.BlockSpec(memory_space=pl.ANY),
            pl.BlockSpec(memory_space=pl.ANY),
        ],
        out_specs=[
            pl.BlockSpec((tm, tn), lambda i, j: (i + 1, j)),
            pl.BlockSpec((tm, GLA_RANK), lambda i, j: (i + 1, 0)),
        ],
        out_shape=[
            jax.ShapeDtypeStruct((rows, N_MAIN), F32),
            jax.ShapeDtypeStruct((rows, GLA_RANK), F32),
        ],
        input_output_aliases={4: 0, 5: 1},
        scratch_shapes=[pltpu.VMEM((tm, D_MODEL), BF16)],
        compiler_params=pltpu.CompilerParams(
            dimension_semantics=("parallel", "arbitrary"), vmem_limit_bytes=INPROJ_VMEM),
        name=name + "_rest",
    )(x, gain, w16, wg16, u, gad)
    return u, gad, us, gads


def _gla_masks(C):
    ri = lax.broadcasted_iota(jnp.int32, (C, C), 0)
    ci = lax.broadcasted_iota(jnp.int32, (C, C), 1)
    rowd = lax.broadcasted_iota(jnp.int32, (C, GLA_DK), 0)
    sub_row = lax.broadcasted_iota(jnp.int32, (GLA_SUB, GLA_DK), 0)
    levels = []
    m = C // 2
    while m >= GLA_SUB:
        levels.append((m, (rowd & m) != 0, (ri ^ ci) < 2 * m))
        m //= 2
    return dict(levels=levels, causal=[sub_row >= s for s in range(GLA_SUB)],
                sub_col=lax.broadcasted_iota(jnp.int32, (GLA_SUB, C), 1))


def _gla_head(q, k, v, b, state, masks, *, C):
    o = _dot((q * jnp.exp(b)).astype(BF16), state.astype(BF16))

    att = None
    for m, second, same_block in masks["levels"]:
        nblk = C // (2 * m)
        pieces = [jnp.broadcast_to(b[i * 2 * m + m - 1:i * 2 * m + m, :], (2 * m, GLA_DK))
                  for i in range(nblk)]
        bref = pieces[0] if nblk == 1 else jnp.concatenate(pieces, axis=0)
        e = jnp.exp(-jnp.abs(b - bref))
        ql = jnp.where(second, q * e, 0.0).astype(BF16)
        kl = jnp.where(second, 0.0, k * e).astype(BF16)
        a = _dot_nt(ql, kl)
        if nblk > 1:
            a = jnp.where(same_block, a, 0.0)
        att = a if att is None else att + a

    sub_col = masks["sub_col"]
    diag = []
    for i in range(C // GLA_SUB):
        sl = slice(GLA_SUB * i, GLA_SUB * (i + 1))
        bb, qq, kk = b[sl], q[sl], k[sl]
        blk = jnp.zeros((GLA_SUB, C), F32)
        for s in range(GLA_SUB):
            d = jnp.where(masks["causal"][s], bb - bb[s:s + 1], MASK_NEG)
            w = jnp.sum(qq * jnp.exp(d) * kk[s:s + 1], axis=-1, keepdims=True)
            blk = jnp.where(sub_col == GLA_SUB * i + s, w, blk)
        diag.append(blk)
    diag = diag[0] if len(diag) == 1 else jnp.concatenate(diag, axis=0)
    att = diag if att is None else att + diag
    o = o + _dot(att.astype(BF16), v.astype(BF16))

    bl = b[C - 1:C, :]
    kd = k * jnp.exp(bl - b)
    vp = v
    if C < GLA_DK:
        kd = jnp.concatenate([kd, jnp.zeros((GLA_DK - C, GLA_DK), F32)], axis=0)
        vp = jnp.concatenate([v, jnp.zeros((GLA_DK - C, GLA_DV), F32)], axis=0)
    new_state = state * _row_to_col(jnp.exp(bl), GLA_DK) + _dot(kd.T.astype(BF16), vp.astype(BF16))
    return o, new_state


def _gla_kernel(q_ref, k_ref, v_ref, gad_ref, aup_ref, ab_ref, s0_ref, o_ref, sfin_ref, s_ref,
                *, C, sub, nc):
    n = pl.program_id(1)

    @pl.when(n == 0)
    def _():
        s_ref[...] = s0_ref[...]

    def chunk(c, carry):
        rows = pl.ds(pl.multiple_of(c * C, C), C)
        logits = _dot(gad_ref[rows, :].astype(BF16), aup_ref[...].astype(BF16)) + ab_ref[...]
        g = _log_sigmoid(logits) / GLA_TAU

        ri = lax.broadcasted_iota(jnp.int32, (C, C), 0)
        ci = lax.broadcasted_iota(jnp.int32, (C, C), 1)
        tri = jnp.where(ri >= ci, 1.0, 0.0).astype(BF16)
        g_hi = g.astype(BF16)
        r1 = g - g_hi.astype(F32)
        g_mid = r1.astype(BF16)
        g_lo = (r1 - g_mid.astype(F32)).astype(BF16)
        b_all = _dot(tri, g_hi) + _dot(tri, g_mid) + _dot(tri, g_lo)

        masks = _gla_masks(C)
        for h in range(GLA_HEADS):
            ks = slice(GLA_DK * h, GLA_DK * (h + 1))
            vs = slice(GLA_DV * h, GLA_DV * (h + 1))
            o, new_state = _gla_head(q_ref[rows, ks] * (GLA_DK ** -0.5), k_ref[rows, ks], v_ref[rows, vs],
                                     b_all[:, ks], s_ref[h], masks, C=C)
            o_ref[rows, vs] = o
            s_ref[h] = new_state
        return carry

    if sub == 1:
        chunk(0, 0)
    else:
        lax.fori_loop(0, sub, chunk, 0)

    @pl.when(n == nc - 1)
    def _():
        sfin_ref[0] = s_ref[...]


def _gla(u, gad, a_up, a_bias, s0, *, batch, seq, C, sub, row_off, name):
    rows = C * sub
    nc = seq // rows
    return pl.pallas_call(
        functools.partial(_gla_kernel, C=C, sub=sub, nc=nc),
        grid=(batch, nc),
        in_specs=[
            pl.BlockSpec((rows, GLA_QK), lambda b, n: (row_off + b * nc + n, COL_GQ // GLA_QK)),
            pl.BlockSpec((rows, GLA_QK), lambda b, n: (row_off + b * nc + n, COL_GK // GLA_QK)),
            pl.BlockSpec((rows, GLA_V), lambda b, n: (row_off + b * nc + n, COL_GV // GLA_V)),
            pl.BlockSpec((rows, GLA_RANK), lambda b, n: (row_off + b * nc + n, 0)),
            pl.BlockSpec((GLA_RANK, GLA_QK), lambda b, n: (0, 0)),
            pl.BlockSpec((1, GLA_QK), lambda b, n: (0, 0)),
            pl.BlockSpec((GLA_HEADS, GLA_DK, GLA_DV), lambda b, n: (0, 0, 0)),
        ],
        out_specs=[
            pl.BlockSpec((rows, GLA_V), lambda b, n: (b * nc + n, 0)),
            pl.BlockSpec((1, GLA_HEADS, GLA_DK, GLA_DV), lambda b, n: (b, 0, 0, 0)),
        ],
        out_shape=[
            jax.ShapeDtypeStruct((batch * seq, GLA_V), F32),
            jax.ShapeDtypeStruct((batch, GLA_HEADS, GLA_DK, GLA_DV), F32),
        ],
        scratch_shapes=[pltpu.VMEM((GLA_HEADS, GLA_DK, GLA_DV), F32)],
        compiler_params=pltpu.CompilerParams(
            dimension_semantics=("parallel", "arbitrary"), vmem_limit_bytes=ATTN_VMEM),
        name=name,
    )(u, u, u, gad, a_up, a_bias, s0)


def _gla_step_kernel(q_ref, k_ref, v_ref, gad_ref, aup_ref, ab_ref, s_ref, o_ref, sn_ref, *, bb):
    gl = jnp.concatenate([gad_ref[i] for i in range(bb)] + [jnp.zeros((SUBLANE - bb, GLA_RANK), F32)], axis=0)
    logits = _dot(gl.astype(BF16), aup_ref[...].astype(BF16)) + ab_ref[...]
    dec_all = jnp.exp(_log_sigmoid(logits) / GLA_TAU)
    for i in range(bb):
        q = q_ref[i] * (GLA_DK ** -0.5)
        k = k_ref[i]
        v = v_ref[i]
        dec = dec_all[i:i + 1]
        outs = []
        for h in range(GLA_HEADS):
            sl = slice(GLA_DK * h, GLA_DK * (h + 1))
            new_state = (s_ref[i, h] * _row_to_col(dec[:, sl], GLA_DK)
                         + _row_to_col(k[:, sl], GLA_DK) * v[:, GLA_DV * h:GLA_DV * (h + 1)])
            sn_ref[i, h] = new_state
            outs.append(jnp.sum(_row_to_col(q[:, sl], GLA_DK) * new_state, axis=0, keepdims=True))
        o_ref[i] = jnp.concatenate(outs, axis=1)


def _gla_step(u3, gad3, a_up, a_bias, state, *, batch, bb, name):
    return pl.pallas_call(
        functools.partial(_gla_step_kernel, bb=bb),
        grid=(batch // bb,),
        in_specs=[
            pl.BlockSpec((bb, 1, GLA_QK), lambda b: (b, 0, COL_GQ // GLA_QK)),
            pl.BlockSpec((bb, 1, GLA_QK), lambda b: (b, 0, COL_GK // GLA_QK)),
            pl.BlockSpec((bb, 1, GLA_V), lambda b: (b, 0, COL_GV // GLA_V)),
            pl.BlockSpec((bb, 1, GLA_RANK), lambda b: (b, 0, 0)),
            pl.BlockSpec((GLA_RANK, GLA_QK), lambda b: (0, 0)),
            pl.BlockSpec((1, GLA_QK), lambda b: (0, 0)),
            pl.BlockSpec((bb, GLA_HEADS, GLA_DK, GLA_DV), lambda b: (b, 0, 0, 0)),
        ],
        out_specs=[
            pl.BlockSpec((bb, 1, GLA_V), lambda b: (b, 0, 0)),
            pl.BlockSpec((bb, GLA_HEADS, GLA_DK, GLA_DV), lambda b: (b, 0, 0, 0)),
        ],
        out_shape=[
            jax.ShapeDtypeStruct((batch, 1, GLA_V), F32),
            jax.ShapeDtypeStruct((batch, GLA_HEADS, GLA_DK, GLA_DV), F32),
        ],
        compiler_params=pltpu.CompilerParams(dimension_semantics=("parallel",)),
        name=name,
    )(u3, u3, u3, gad3, a_up, a_bias, state)


def _alibi_slope(head):
    return 2.0 ** (-8.0 * (head + 1) / SWA_HEADS)


def _swa_kernel(sink_ref, q_ref, kc_ref, kp_ref, vc_ref, vp_ref, mk_ref, mv_ref, o_ref, bias_ref):
    n = pl.program_id(1)
    rows = SWA_GROUP * SWA_QB
    nsub = WINDOW // SWA_QB

    @pl.when(n <= 1)
    def _():
        r = lax.broadcasted_iota(jnp.int32, (SWA_QB, SWA_KT), 0)
        c = lax.broadcasted_iota(jnp.int32, (SWA_QB, SWA_KT), 1)
        dist = r + WINDOW - c
        distf = dist.astype(F32)
        band = (dist >= 0) & (dist < WINDOW) & (c < SWA_SPAN)
        for a in range(nsub):
            ok = band & ((c >= WINDOW - SWA_QB * a) | (n > 0))
            for head in range(SWA_HEADS):
                val = jnp.where(ok, -_alibi_slope(head) * distf, MASK_NEG)
                val = jnp.where((c >= SWA_SPAN) & (c < SWA_SINK_COL), 0.0, val)
                val = jnp.where(c == SWA_SINK_COL, sink_ref[head], val)
                tile = (head // SWA_GROUP) * nsub + a
                bias_ref[pl.ds(tile * rows + (head % SWA_GROUP) * SWA_QB, SWA_QB), :] = val

    q = q_ref[...] * (SWA_HD ** -0.5)
    pad = jnp.zeros((SWA_KT - SWA_SPAN - N_META, SWA_HD), BF16)
    scores, values = [], []
    for kvh in range(SWA_KV_HEADS):
        cs = slice(SWA_HD * kvh, SWA_HD * (kvh + 1))
        kwin = jnp.concatenate([kp_ref[:, cs], kc_ref[:, cs]], axis=0).astype(BF16)
        vwin = jnp.concatenate([vp_ref[:, cs], vc_ref[:, cs]], axis=0).astype(BF16)
        km = mk_ref[:, cs].astype(BF16)
        vm = mv_ref[:, cs].astype(BF16)
        for a in range(nsub):
            qs = slice(SWA_QB * a, SWA_QB * (a + 1))
            kt = jnp.concatenate([kwin[SWA_QB * a:SWA_QB * a + SWA_SPAN], km, pad], axis=0)
            values.append(jnp.concatenate([vwin[SWA_QB * a:SWA_QB * a + SWA_SPAN], vm, pad], axis=0))
            qg = jnp.concatenate(
                [q[qs, SWA_HD * (kvh * SWA_GROUP + grp):SWA_HD * (kvh * SWA_GROUP + grp + 1)]
                 for grp in range(SWA_GROUP)], axis=0).astype(BF16)
            scores.append(_dot_nt(qg, kt))
    s = jnp.concatenate(scores, axis=0) + bias_ref[...]
    p = jnp.exp(s - jnp.max(s, axis=-1, keepdims=True))
    inv = 1.0 / jnp.sum(p, axis=-1, keepdims=True)
    p = p.astype(BF16)
    for kvh in range(SWA_KV_HEADS):
        for a in range(nsub):
            tile = kvh * nsub + a
            ts = slice(rows * tile, rows * (tile + 1))
            o = _dot(p[ts], values[tile]) * inv[ts]
            for grp in range(SWA_GROUP):
                head = kvh * SWA_GROUP + grp
                o_ref[SWA_QB * a:SWA_QB * (a + 1), SWA_HD * head:SWA_HD * (head + 1)] = (
                    o[SWA_QB * grp:SWA_QB * (grp + 1)].astype(o_ref.dtype))


def _swa(sinks, u, u_small, *, batch, seq, meta_row_block, name):
    nb = seq // WINDOW
    kcol, vcol = COL_SK // SWA_KV, COL_SV // SWA_KV
    return pl.pallas_call(
        _swa_kernel,
        grid=(batch, nb),
        in_specs=[
            pl.BlockSpec(memory_space=pltpu.SMEM),
            pl.BlockSpec((WINDOW, SWA_Q), lambda b, n: (b * nb + n, COL_SQ // SWA_Q)),
            pl.BlockSpec((WINDOW, SWA_KV), lambda b, n: (b * nb + n, kcol)),
            pl.BlockSpec((WINDOW, SWA_KV), lambda b, n: (b * nb + jnp.maximum(n - 1, 0), kcol)),
            pl.BlockSpec((WINDOW, SWA_KV), lambda b, n: (b * nb + n, vcol)),
            pl.BlockSpec((WINDOW, SWA_KV), lambda b, n: (b * nb + jnp.maximum(n - 1, 0), vcol)),
            pl.BlockSpec((N_META, SWA_KV), lambda b, n: (meta_row_block, kcol)),
            pl.BlockSpec((N_META, SWA_KV), lambda b, n: (meta_row_block, vcol)),
        ],
        out_specs=pl.BlockSpec((WINDOW, SWA_Q), lambda b, n: (b * nb + n, 0)),
        out_shape=jax.ShapeDtypeStruct((batch * seq, SWA_Q), BF16),
        scratch_shapes=[pltpu.VMEM((SWA_HEADS * WINDOW, SWA_KT), F32)],
        compiler_params=pltpu.CompilerParams(
            dimension_semantics=("arbitrary", "arbitrary"), vmem_limit_bytes=ATTN_VMEM),
        name=name,
    )(sinks, u, u, u, u, u, u_small, u_small)


def _swa_step_kernel(q_ref, kold_ref, vold_ref, knew_ref, vnew_ref, mk_ref, mv_ref, sink_ref, slope_ref,
                     o_ref, nk_ref, nv_ref, *, bb):
    rows = bb * SWA_GROUP
    wcols = bb * WINDOW
    ncol = wcols + bb * N_META
    win_shift, meta_shift, grp_shift = WINDOW.bit_length() - 1, N_META.bit_length() - 1, SWA_GROUP.bit_length() - 1
    pos = lax.broadcasted_iota(jnp.int32, (SWA_KV, WINDOW), 1)
    knew_t = jnp.concatenate([knew_ref[i] for i in range(bb)], axis=0).T
    vnew_t = jnp.concatenate([vnew_ref[i] for i in range(bb)], axis=0).T
    for i in range(bb):
        nk_ref[i] = jnp.where(pos == WINDOW - 1, knew_t[:, i:i + 1], pltpu.roll(kold_ref[i], WINDOW - 1, axis=1))
        nv_ref[i] = jnp.where(pos == WINDOW - 1, vnew_t[:, i:i + 1], pltpu.roll(vold_ref[i], WINDOW - 1, axis=1))

    r = lax.broadcasted_iota(jnp.int32, (rows, ncol), 0)
    c = lax.broadcasted_iota(jnp.int32, (rows, ncol), 1)
    in_win = c < wcols
    col_batch = jnp.where(in_win, c >> win_shift, (c - wcols) >> meta_shift)
    own = col_batch == (r >> grp_shift)
    distf = jnp.where(in_win, WINDOW - 1 - (c & (WINDOW - 1)), 0).astype(F32)
    for kvh in range(SWA_KV_HEADS):
        cs = slice(SWA_HD * kvh, SWA_HD * (kvh + 1))
        hs = slice(SWA_GROUP * kvh, SWA_GROUP * (kvh + 1))
        k_t = jnp.concatenate([nk_ref[i, cs, :] for i in range(bb)], axis=1).astype(BF16)
        v_t = jnp.concatenate([nv_ref[i, cs, :] for i in range(bb)], axis=1).astype(BF16)
        km = jnp.concatenate([mk_ref[i, :, cs] for i in range(bb)], axis=0).astype(BF16)
        vm = jnp.concatenate([mv_ref[i, :, cs] for i in range(bb)], axis=0).astype(BF16)
        qh = jnp.concatenate([q_ref[i, hs, :] for i in range(bb)], axis=0)
        qh = (qh * (SWA_HD ** -0.5)).astype(BF16)
        s = jnp.concatenate([_dot(qh, k_t), _dot_nt(qh, km)], axis=1)
        s = jnp.where(own, s - slope_ref[kvh] * distf, MASK_NEG)
        sink = sink_ref[kvh]
        mx = jnp.maximum(jnp.max(s, axis=-1, keepdims=True), sink)
        p = jnp.exp(s - mx)
        den = jnp.sum(p, axis=-1, keepdims=True) + jnp.exp(sink - mx)
        p = p.astype(BF16)
        oh = (_dot_nt(p[:, :wcols], v_t) + _dot(p[:, wcols:], vm)) / den
        for i in range(bb):
            o_ref[i, hs, :] = oh[SWA_GROUP * i:SWA_GROUP * (i + 1)]


def _swa_step(q3, k_old, v_old, u3, mk, mv, sinks, *, batch, bb, name):
    def per_row(x16):
        return jnp.tile(x16.reshape(SWA_KV_HEADS, 1, SWA_GROUP), (1, bb, 1)).reshape(
            SWA_KV_HEADS, bb * SWA_GROUP, 1)

    slopes = jnp.asarray([_alibi_slope(h) for h in range(SWA_HEADS)], F32)
    head_spec = pl.BlockSpec((SWA_KV_HEADS, bb * SWA_GROUP, 1), lambda b: (0, 0, 0))
    win_spec = pl.BlockSpec((bb, SWA_KV, WINDOW), lambda b: (b, 0, 0))
    meta_spec = pl.BlockSpec((bb, N_META, SWA_KV), lambda b: (b, 0, 0))
    return pl.pallas_call(
        functools.partial(_swa_step_kernel, bb=bb),
        grid=(batch // bb,),
        in_specs=[
            pl.BlockSpec((bb, SWA_HEADS, SWA_HD), lambda b: (b, 0, 0)),
            win_spec, win_spec,
            pl.BlockSpec((bb, 1, SWA_KV), lambda b: (b, 0, COL_SK // SWA_KV)),
            pl.BlockSpec((bb, 1, SWA_KV), lambda b: (b, 0, COL_SV // SWA_KV)),
            meta_spec, meta_spec, head_spec, head_spec,
        ],
        out_specs=[pl.BlockSpec((bb, SWA_HEADS, SWA_HD), lambda b: (b, 0, 0)), win_spec, win_spec],
        out_shape=[
            jax.ShapeDtypeStruct((batch, SWA_HEADS, SWA_HD), F32),
            jax.ShapeDtypeStruct((batch, SWA_KV, WINDOW), F32),
            jax.ShapeDtypeStruct((batch, SWA_KV, WINDOW), F32),
        ],
        compiler_params=pltpu.CompilerParams(dimension_semantics=("parallel",)),
        name=name,
    )(q3, k_old, v_old, u3, u3, mk, mv, per_row(sinks), per_row(slopes))


def _merge_kernel(h_ref, og_ref, gr_ref, os_ref, ga_ref, gb_ref,
                  hs_ref, ogs_ref, grs_ref, oss_ref, gas_ref, gbs_ref,
                  onorm_ref, gbias_ref, wg_ref, ws_ref, wo_ref, out_ref, outs_ref):
    _merge_rows(h_ref, og_ref, gr_ref, os_ref, ga_ref, gb_ref, onorm_ref, gbias_ref,
                wg_ref, ws_ref, wo_ref, out_ref)

    @pl.when(pl.program_id(0) == 0)
    def _():
        _merge_rows(hs_ref, ogs_ref, grs_ref, oss_ref, gas_ref, gbs_ref, onorm_ref, gbias_ref,
                    wg_ref, ws_ref, wo_ref, outs_ref)


def _merge_rows(h_ref, og_ref, gr_ref, os_ref, ga_ref, gb_ref, onorm_ref, gbias_ref,
                wg_ref, ws_ref, wo_ref, out_ref):
    og = og_ref[...]
    parts = []
    for h in range(GLA_HEADS):
        oh = og[:, GLA_DV * h:GLA_DV * (h + 1)]
        parts.append(oh * lax.rsqrt(jnp.mean(oh * oh, axis=-1, keepdims=True) + EPS))
    gr = gr_ref[...]
    of = jnp.concatenate(parts, axis=1) * onorm_ref[...] * (gr * jax.nn.sigmoid(gr))
    gla_b = _dot(of.astype(BF16), wg_ref[...])
    swa_b = _dot(os_ref[...].astype(BF16), ws_ref[...])
    gbias = gbias_ref[...]
    mix = (jax.nn.sigmoid(ga_ref[...] + gbias[:, :D_MODEL]) * gla_b
           + jax.nn.sigmoid(gb_ref[...] + gbias[:, D_MODEL:]) * swa_b)
    out_ref[...] = h_ref[...] + _dot(mix.astype(BF16), wo_ref[...])


def _merge(h, o_gla, u, o_swa, hs, o_gla_s, us, o_swa_s, onorm, gbias, w_gla_o, w_swa_o, w_out,
           *, tm, name):
    rows, rs = h.shape[0], o_gla_s.shape[0]
    const = dict(pipeline_mode=pl.Buffered(1))

    def row_specs(r, idx):
        return [
            pl.BlockSpec((r, D_MODEL), lambda i: (idx(i), 0)),
            pl.BlockSpec((r, GLA_V), lambda i: (idx(i), 0)),
            pl.BlockSpec((r, GLA_V), lambda i: (idx(i), COL_GR // GLA_V)),
            pl.BlockSpec((r, SWA_Q), lambda i: (idx(i), 0)),
            pl.BlockSpec((r, D_MODEL), lambda i: (idx(i), COL_GA // D_MODEL)),
            pl.BlockSpec((r, D_MODEL), lambda i: (idx(i), COL_GB // D_MODEL)),
        ]

    return pl.pallas_call(
        _merge_kernel,
        grid=(rows // tm,),
        in_specs=row_specs(tm, lambda i: i) + row_specs(rs, lambda i: 0) + [
            pl.BlockSpec((1, GLA_V), lambda i: (0, 0)),
            pl.BlockSpec((1, 2 * D_MODEL), lambda i: (0, 0)),
            pl.BlockSpec((GLA_V, D_MODEL), lambda i: (0, 0), **const),
            pl.BlockSpec((SWA_Q, D_MODEL), lambda i: (0, 0), **const),
            pl.BlockSpec((D_MODEL, D_MODEL), lambda i: (0, 0), **const),
        ],
        out_specs=[
            pl.BlockSpec((tm, D_MODEL), lambda i: (i, 0)),
            pl.BlockSpec((rs, D_MODEL), lambda i: (0, 0)),
        ],
        out_shape=[
            jax.ShapeDtypeStruct((rows, D_MODEL), F32),
            jax.ShapeDtypeStruct((rs, D_MODEL), F32),
        ],
        compiler_params=pltpu.CompilerParams(
            dimension_semantics=("arbitrary",), vmem_limit_bytes=MERGE_VMEM),
        name=name,
    )(h, o_gla, u, o_swa, u, u, hs, o_gla_s, us, o_swa_s, us, us,
      onorm, gbias, w_gla_o, w_swa_o, w_out)


def kernel(x_prompt, x_sample, cache_meta_k, cache_meta_v, cache_win_k, cache_win_v, state_gla,
           meta_tokens, ffn1_norm, ffn1_w_in, ffn1_w_out, mix_norm, w_in, gla_a_up, gla_a_bias,
           gla_out_norm, w_gla_o, swa_sinks, w_swa_o, gate_bias, w_out, ffn2_norm, ffn2_w_in,
           ffn2_w_out, final_norm):
    B, S, _ = x_prompt.shape
    DB = x_sample.shape[0]
    assert x_sample.shape[1] == 1 and ffn1_norm.shape[0] == 1
    n_small = DB + N_META

    wgo, wso, wout = w_gla_o[0].astype(BF16), w_swa_o[0].astype(BF16), w_out[0].astype(BF16)
    g1, gm, g2 = ffn1_norm, mix_norm, ffn2_norm
    gf = final_norm.reshape(1, D_MODEL)
    a_up, a_bias = gla_a_up[0], gla_a_bias
    onorm, gbias = gla_out_norm, gate_bias
    sinks = swa_sinks[0]

    xs = jnp.concatenate([x_sample.reshape(DB, D_MODEL), meta_tokens.astype(F32)], axis=0)
    xp = x_prompt.reshape(B * S, D_MODEL)
    ffn_tiles = dict(tm=FFN_TM, tf_first=FFN_TF_FIRST, tf=FFN_TF)
    hp1, hs1 = _ffn(xp, xs, g1, ffn1_w_in[0], ffn1_w_out[0], gf, final_norm=False, name="ffn1", **ffn_tiles)
    up, gadp, us, gads = _inproj(hp1, hs1, gm, w_in[0].T, tm=INPROJ_TM, tn_first=INPROJ_TN_FIRST,
                                 tn=INPROJ_TN, name="inproj")

    meta_blk = DB // N_META
    _, st_meta = _gla(us, gads, a_up, a_bias, jnp.zeros((GLA_HEADS, GLA_DK, GLA_DV), F32),
                      batch=1, seq=N_META, C=N_META, sub=1, row_off=meta_blk, name="gla_meta")
    st_meta = st_meta[0]
    o_gla, st_fin = _gla(up, gadp, a_up, a_bias, st_meta, batch=B, seq=S, C=GLA_CHUNK,
                         sub=GLA_CHUNKS_PER_STEP, row_off=0, name="gla")
    o_swa = _swa(sinks, up, us, batch=B, seq=S, meta_row_block=meta_blk, name="swa")

    us3 = us.reshape(n_small, 1, N_MAIN)
    og_s, st_s = _gla_step(us3, gads.reshape(n_small, 1, GLA_RANK),
                           a_up, a_bias, state_gla[0], batch=DB, bb=GLA_STEP_BATCH, name="gla_step")
    WB = cache_win_k.shape[2]
    assert WB == WINDOW
    os_s, new_kt, new_vt = _swa_step(
        us[:DB, COL_SQ:COL_SQ + SWA_Q].reshape(DB, SWA_HEADS, SWA_HD),
        jnp.swapaxes(cache_win_k[0].reshape(DB, WB, SWA_KV), 1, 2),
        jnp.swapaxes(cache_win_v[0].reshape(DB, WB, SWA_KV), 1, 2), us3,
        cache_meta_k[0].reshape(DB, N_META, SWA_KV), cache_meta_v[0].reshape(DB, N_META, SWA_KV),
        sinks, batch=DB, bb=SWA_STEP_BATCH, name="swa_step")
    hp2, hs2 = _merge(hp1, o_gla, up, o_swa, hs1, og_s.reshape(DB, GLA_V), us, os_s.reshape(DB, SWA_Q),
                      onorm, gbias, wgo, wso, wout, tm=MERGE_TM, name="merge")
    y_prompt, y_sample = _ffn(hp2, hs2, g2, ffn2_w_in[0], ffn2_w_out[0], gf, final_norm=True, name="ffn2",
                              **ffn_tiles)

    up3 = up.reshape(B, S, N_MAIN)
    kv_shape = (SWA_KV_HEADS, SWA_HD)
    p_meta_k = jnp.broadcast_to(us[DB:, COL_SK:COL_SK + SWA_KV].reshape(1, 1, N_META, *kv_shape),
                                (1, B, N_META, *kv_shape))
    p_meta_v = jnp.broadcast_to(us[DB:, COL_SV:COL_SV + SWA_KV].reshape(1, 1, N_META, *kv_shape),
                                (1, B, N_META, *kv_shape))
    p_win_k = up3[:, S - WINDOW:, COL_SK:COL_SK + SWA_KV].reshape(1, B, WINDOW, *kv_shape)
    p_win_v = up3[:, S - WINDOW:, COL_SV:COL_SV + SWA_KV].reshape(1, B, WINDOW, *kv_shape)
    return (y_prompt.reshape(B, S, D_MODEL), y_sample.reshape(DB, 1, D_MODEL),
            p_meta_k, p_meta_v, p_win_k, p_win_v, st_fin[None],
            jnp.swapaxes(new_kt, 1, 2).reshape(1, DB, WB, *kv_shape),
            jnp.swapaxes(new_vt, 1, 2).reshape(1, DB, WB, *kv_shape), st_s[None])
```

```python
import functools

import jax
import jax.numpy as jnp
from jax import lax
from jax.experimental import pallas as pl
from jax.experimental.pallas import tpu as pltpu

F32 = jnp.float32
BF16 = jnp.bfloat16

D_MODEL = 2048
N_META = 16
GLA_HEADS = 4
GLA_DK = 128
GLA_DV = 256
GLA_RANK = 16
GLA_TAU = 16.0
SWA_HEADS = 16
SWA_KV_HEADS = 4
SWA_GROUP = SWA_HEADS // SWA_KV_HEADS
SWA_HD = 64
WINDOW = 128
D_FF = 5632
EPS = 1e-6

GLA_QK = GLA_HEADS * GLA_DK
GLA_V = GLA_HEADS * GLA_DV
SWA_Q = SWA_HEADS * SWA_HD
SWA_KV = SWA_KV_HEADS * SWA_HD

COL_GA = 0
COL_GB = COL_GA + D_MODEL
COL_GQ = COL_GB + D_MODEL
COL_GK = COL_GQ + GLA_QK
COL_GV = COL_GK + GLA_QK
COL_GR = COL_GV + GLA_V
COL_SQ = COL_GR + GLA_V
COL_SK = COL_SQ + SWA_Q
COL_SV = COL_SK + SWA_KV
N_MAIN = COL_SV + SWA_KV
N_GRP_A = COL_GQ
N_GRP_B = COL_SQ - COL_GQ
N_GRP_C = N_MAIN - COL_SQ

GLA_SUB = 8
MASK_NEG = -1e30
MIB = 1024 * 1024
LANE = 128
SUBLANE = 8
BF16_ROWS = 16
ROW_CHUNK = 256
COL_CHUNK = 512

FFN_TM = 1024
FFN_TF_FIRST = 256
FFN_TF = 512
FFN_VMEM = 58 * MIB
INPROJ_TM = 2048
INPROJ_TN_FIRST = 512
INPROJ_TN = 512
INPROJ_VMEM = 56 * MIB
MERGE_TM = 256
MERGE_VMEM = 56 * MIB
PROJ_TM = 512
GLA_CHUNK = 256
GLA_CHUNKS_PER_STEP = 1
GLA_STEP_BATCH = 4
SWA_STEP_BATCH = 8
ATTN_VMEM = 32 * MIB

SWA_QB = 64
SWA_SPAN = WINDOW + SWA_QB
SWA_KT = 256
SWA_SINK_COL = SWA_SPAN + N_META


def _rms_scale(x):
    return x * lax.rsqrt(jnp.mean(x * x, axis=-1, keepdims=True) + EPS)


def _log_sigmoid(x):
    return jnp.minimum(x, 0.0) - jnp.log(1.0 + jnp.exp(-jnp.abs(x)))


def _dot(a, b):
    return jnp.dot(a, b, preferred_element_type=F32)


def _dot_nt(a, b):
    return lax.dot_general(a, b, (((1,), (1,)), ((), ())), preferred_element_type=F32)


def _row_to_col(row, n):
    r = lax.broadcasted_iota(jnp.int32, (n, n), 0)
    c = lax.broadcasted_iota(jnp.int32, (n, n), 1)
    return jnp.sum(jnp.where(r == c, jnp.broadcast_to(row, (n, n)), 0.0), axis=1, keepdims=True)


def _for_row_chunks(rows, fn):
    chunk = ROW_CHUNK if rows % ROW_CHUNK == 0 else rows

    def body(i, carry):
        fn(pl.ds(pl.multiple_of(i * chunk, chunk), chunk))
        return carry

    lax.fori_loop(0, rows // chunk, body, 0)


def _row_offsets(refs):
    offs, total = [], 0
    for r in refs:
        offs.append(total)
        total += r.shape[0]
    return offs


def _ffn_rows(j, n_ff, x_refs, o_refs, xn_ref, g_ref, fg_ref, wa, wb, wo, final_norm):
    offs = _row_offsets(x_refs)

    @pl.when(j == 0)
    def _():
        for x_ref, o_ref, off in zip(x_refs, o_refs, offs):
            def norm_rows(sl, x_ref=x_ref, o_ref=o_ref, off=off):
                x = x_ref[sl, :]
                dst = pl.ds(pl.multiple_of(off + sl.start, BF16_ROWS), sl.size)
                xn_ref[dst, :] = (_rms_scale(x) * g_ref[...]).astype(BF16)
                o_ref[sl, :] = x

            _for_row_chunks(x_ref.shape[0], norm_rows)

    xn = xn_ref[...]
    for t in range(wa.shape[0]):
        a = _dot(xn, wa[t])
        b = _dot(xn, wb[t])
        h = ((0.5 * a) * jax.nn.sigmoid(a) * b).astype(BF16)
        for c in range(D_MODEL // COL_CHUNK):
            cs = slice(COL_CHUNK * c, COL_CHUNK * (c + 1))
            r = _dot(h, wo[t][:, cs])
            for o_ref, off in zip(o_refs, offs):
                o_ref[:, cs] += r[off:off + o_ref.shape[0]]

    if final_norm:
        @pl.when(j == n_ff - 1)
        def _():
            for o_ref in o_refs:
                def final_rows(sl, o_ref=o_ref):
                    o_ref[sl, :] = _rms_scale(o_ref[sl, :]) * fg_ref[...]

                _for_row_chunks(o_ref.shape[0], final_rows)


def _ffn_first_kernel(x_ref, xs_ref, g_ref, wa_ref, wb_ref, wo_ref, fg_ref,
                      o_ref, os_ref, wa16_ref, wb16_ref, wo16_ref, xn_ref, *, n_ff, final_norm):
    wa16_ref[0] = wa_ref[...].astype(BF16)
    wb16_ref[0] = wb_ref[...].astype(BF16)
    wo16_ref[0] = wo_ref[...].astype(BF16)
    _ffn_rows(pl.program_id(0), n_ff, [x_ref, xs_ref], [o_ref, os_ref], xn_ref, g_ref, fg_ref,
              wa16_ref[...], wb16_ref[...], wo16_ref[...], final_norm)


def _ffn_rest_kernel(x_ref, g_ref, wa_ref, wb_ref, wo_ref, fg_ref, o_first_ref, o_ref, xn_ref,
                     *, n_ff, final_norm):
    del o_first_ref
    _ffn_rows(pl.program_id(1), n_ff, [x_ref], [o_ref], xn_ref, g_ref, fg_ref,
              wa_ref[...], wb_ref[...], wo_ref[...], final_norm)


def _ffn(x, xs, gain, w_in, w_out, final_gain, *, tm, tf_first, tf, final_norm, name):
    rows, rows_s = x.shape[0], xs.shape[0]
    n1 = D_FF // tf_first
    o, os_, wa16, wb16, wo16 = pl.pallas_call(
        functools.partial(_ffn_first_kernel, n_ff=n1, final_norm=final_norm),
        grid=(n1,),
        in_specs=[
            pl.BlockSpec((tm, D_MODEL), lambda j: (0, 0), pipeline_mode=pl.Buffered(1)),
            pl.BlockSpec((rows_s, D_MODEL), lambda j: (0, 0), pipeline_mode=pl.Buffered(1)),
            pl.BlockSpec((1, D_MODEL), lambda j: (0, 0)),
            pl.BlockSpec((D_MODEL, tf_first), lambda j: (0, j)),
            pl.BlockSpec((D_MODEL, tf_first), lambda j: (0, j + n1)),
            pl.BlockSpec((tf_first, D_MODEL), lambda j: (j, 0)),
            pl.BlockSpec((1, D_MODEL), lambda j: (0, 0)),
        ],
        out_specs=[
            pl.BlockSpec((tm, D_MODEL), lambda j: (0, 0)),
            pl.BlockSpec((rows_s, D_MODEL), lambda j: (0, 0)),
            pl.BlockSpec((1, D_MODEL, tf_first), lambda j: (j, 0, 0)),
            pl.BlockSpec((1, D_MODEL, tf_first), lambda j: (j, 0, 0)),
            pl.BlockSpec((1, tf_first, D_MODEL), lambda j: (j, 0, 0)),
        ],
        out_shape=[
            jax.ShapeDtypeStruct((rows, D_MODEL), F32),
            jax.ShapeDtypeStruct((rows_s, D_MODEL), F32),
            jax.ShapeDtypeStruct((n1, D_MODEL, tf_first), BF16),
            jax.ShapeDtypeStruct((n1, D_MODEL, tf_first), BF16),
            jax.ShapeDtypeStruct((n1, tf_first, D_MODEL), BF16),
        ],
        scratch_shapes=[pltpu.VMEM((tm + rows_s, D_MODEL), BF16)],
        compiler_params=pltpu.CompilerParams(
            dimension_semantics=("arbitrary",), vmem_limit_bytes=FFN_VMEM),
        name=name + "_first",
    )(x, xs, gain, w_in, w_in, w_out, final_gain)

    n2 = D_FF // tf
    per = tf // tf_first
    o = pl.pallas_call(
        functools.partial(_ffn_rest_kernel, n_ff=n2, final_norm=final_norm),
        grid=(rows // tm - 1, n2),
        in_specs=[
            pl.BlockSpec((tm, D_MODEL), lambda i, j: (i + 1, 0)),
            pl.BlockSpec((1, D_MODEL), lambda i, j: (0, 0)),
            pl.BlockSpec((per, D_MODEL, tf_first), lambda i, j: (j, 0, 0)),
            pl.BlockSpec((per, D_MODEL, tf_first), lambda i, j: (j, 0, 0)),
            pl.BlockSpec((per, tf_first, D_MODEL), lambda i, j: (j, 0, 0)),
            pl.BlockSpec((1, D_MODEL), lambda i, j: (0, 0)),
            pl.BlockSpec(memory_space=pl.ANY),
        ],
        out_specs=pl.BlockSpec((tm, D_MODEL), lambda i, j: (i + 1, 0)),
        out_shape=jax.ShapeDtypeStruct((rows, D_MODEL), F32),
        input_output_aliases={6: 0},
        scratch_shapes=[pltpu.VMEM((tm, D_MODEL), BF16)],
        compiler_params=pltpu.CompilerParams(
            dimension_semantics=("parallel", "arbitrary"), vmem_limit_bytes=FFN_VMEM),
        name=name + "_rest",
    )(x, gain, wa16, wb16, wo16, final_gain, o)
    return o, os_


def _inproj_rows(j, x_refs, xn_ref, g_ref, w, wg, u_refs, gad_refs):
    offs = _row_offsets(x_refs)

    @pl.when(j == 0)
    def _():
        for x_ref, off in zip(x_refs, offs):
            def norm_rows(sl, x_ref=x_ref, off=off):
                dst = pl.ds(pl.multiple_of(off + sl.start, BF16_ROWS), sl.size)
                xn_ref[dst, :] = (_rms_scale(x_ref[sl, :]) * g_ref[...]).astype(BF16)

            _for_row_chunks(x_ref.shape[0], norm_rows)
        gad = _dot_nt(xn_ref[...], wg)
        for gad_ref, off in zip(gad_refs, offs):
            gad_ref[...] = gad[off:off + gad_ref.shape[0]]

    u = _dot(xn_ref[...], w)
    for u_ref, off in zip(u_refs, offs):
        u_ref[...] = u[off:off + u_ref.shape[0]]


def _inproj_first_kernel(x_ref, xs_ref, g_ref, wm_ref, wx_ref, wg_ref,
                         u_ref, gad_ref, us_ref, gads_ref, w16_ref, wg16_ref,
                         xn_ref, *, na, nb, tn):
    j = pl.program_id(0)
    shifted = (j < na) | (j >= na + nb)

    @pl.when(shifted)
    def _():
        tall = jnp.concatenate([wm_ref[...], wx_ref[...]], axis=0)
        w16_ref[...] = tall[GLA_RANK:GLA_RANK + tn].T.astype(BF16)

    @pl.when(jnp.logical_not(shifted))
    def _():
        w16_ref[...] = wm_ref[...].T.astype(BF16)

    w = w16_ref[...]
    wg = wg_ref[...].astype(BF16)

    @pl.when(j == 0)
    def _():
        wg16_ref[...] = wg

    _inproj_rows(j, [x_ref, xs_ref], xn_ref, g_ref, w, wg, [u_ref, us_ref], [gad_ref, gads_ref])


def _inproj_rest_kernel(x_ref, g_ref, w_ref, wg_ref, u_first_ref, gad_first_ref, u_ref, gad_ref, xn_ref):
    del u_first_ref, gad_first_ref
    _inproj_rows(pl.program_id(1), [x_ref], xn_ref, g_ref, w_ref[...], wg_ref[...], [u_ref], [gad_ref])


def _inproj(x, xs, gain, w_in, *, tm, tn_first, tn, name):
    rows, rows_s = x.shape[0], xs.shape[0]
    na, nb, ncc = N_GRP_A // tn_first, N_GRP_B // tn_first, N_GRP_C // tn_first
    nj = na + nb + ncc
    c_gad = N_GRP_B
    c_sq = c_gad + GLA_RANK
    c_ga = c_sq + N_GRP_C
    assert c_gad % tn_first == 0 and (c_ga - GLA_RANK) % tn_first == 0
    assert tn_first % LANE == 0 and c_gad % GLA_RANK == 0

    def window(j):
        return jnp.where(j < na, (c_ga - GLA_RANK) // tn_first + j,
                         jnp.where(j < na + nb, j - na, c_gad // tn_first + j - na - nb))

    u, gad, us, gads, w16, wg16 = pl.pallas_call(
        functools.partial(_inproj_first_kernel, na=na, nb=nb, tn=tn_first),
        grid=(nj,),
        in_specs=[
            pl.BlockSpec((tm, D_MODEL), lambda j: (0, 0), pipeline_mode=pl.Buffered(1)),
            pl.BlockSpec((rows_s, D_MODEL), lambda j: (0, 0), pipeline_mode=pl.Buffered(1)),
            pl.BlockSpec((1, D_MODEL), lambda j: (0, 0)),
            pl.BlockSpec((tn_first, D_MODEL), lambda j: (window(j), 0)),
            pl.BlockSpec((GLA_RANK, D_MODEL), lambda j: ((window(j) + 1) * (tn_first // GLA_RANK), 0)),
            pl.BlockSpec((GLA_RANK, D_MODEL), lambda j: (c_gad // GLA_RANK, 0)),
        ],
        out_specs=[
            pl.BlockSpec((tm, tn_first), lambda j: (0, j)),
            pl.BlockSpec((tm, GLA_RANK), lambda j: (0, 0)),
            pl.BlockSpec((rows_s, tn_first), lambda j: (0, j)),
            pl.BlockSpec((rows_s, GLA_RANK), lambda j: (0, 0)),
            pl.BlockSpec((D_MODEL, tn_first), lambda j: (0, j)),
            pl.BlockSpec((GLA_RANK, D_MODEL), lambda j: (0, 0)),
        ],
        out_shape=[
            jax.ShapeDtypeStruct((rows, N_MAIN), F32),
            jax.ShapeDtypeStruct((rows, GLA_RANK), F32),
            jax.ShapeDtypeStruct((rows_s, N_MAIN), F32),
            jax.ShapeDtypeStruct((rows_s, GLA_RANK), F32),
            jax.ShapeDtypeStruct((D_MODEL, N_MAIN), BF16),
            jax.ShapeDtypeStruct((GLA_RANK, D_MODEL), BF16),
        ],
        scratch_shapes=[pltpu.VMEM((tm + rows_s, D_MODEL), BF16)],
        compiler_params=pltpu.CompilerParams(
            dimension_semantics=("arbitrary",), vmem_limit_bytes=INPROJ_VMEM),
        name=name + "_first",
    )(x, xs, gain, w_in, w_in, w_in)

    u, gad = pl.pallas_call(
        _inproj_rest_kernel,
        grid=(rows // tm - 1, N_MAIN // tn),
        in_specs=[
            pl.BlockSpec((tm, D_MODEL), lambda i, j: (i + 1, 0), pipeline_mode=pl.Buffered(1)),
            pl.BlockSpec((1, D_MODEL), lambda i, j: (0, 0)),
            pl.BlockSpec((D_MODEL, tn), lambda i, j: (0, j)),
            pl.BlockSpec((GLA_RANK, D_MODEL), lambda i, j: (0, 0)),
            pl.BlockSpec(memory_space=pl.ANY),
            pl.BlockSpec(memory_space=pl.ANY),
        ],
        out_specs=[
            pl.BlockSpec((tm, tn), lambda i, j: (i + 1, j)),
            pl.BlockSpec((tm, GLA_RANK), lambda i, j: (i + 1, 0)),
        ],
        out_shape=[
            jax.ShapeDtypeStruct((rows, N_MAIN), F32),
            jax.ShapeDtypeStruct((rows, GLA_RANK), F32),
        ],
        input_output_aliases={4: 0, 5: 1},
        scratch_shapes=[pltpu.VMEM((tm, D_MODEL), BF16)],
        compiler_params=pltpu.CompilerParams(
            dimension_semantics=("parallel", "arbitrary"), vmem_limit_bytes=INPROJ_VMEM),
        name=name + "_rest",
    )(x, gain, w16, wg16, u, gad)
    return u, gad, us, gads


def _gla_masks(C):
    ri = lax.broadcasted_iota(jnp.int32, (C, C), 0)
    ci = lax.broadcasted_iota(jnp.int32, (C, C), 1)
    rowd = lax.broadcasted_iota(jnp.int32, (C, GLA_DK), 0)
    sub_row = lax.broadcasted_iota(jnp.int32, (GLA_SUB, GLA_DK), 0)
    levels = []
    m = C // 2
    while m >= GLA_SUB:
        levels.append((m, (rowd & m) != 0, (ri ^ ci) < 2 * m))
        m //= 2
    return dict(levels=levels, causal=[sub_row >= s for s in range(GLA_SUB)],
                sub_col=lax.broadcasted_iota(jnp.int32, (GLA_SUB, C), 1))


def _gla_head(q, k, v, b, state, masks, *, C):
    o = _dot((q * jnp.exp(b)).astype(BF16), state.astype(BF16))

    att = None
    for m, second, same_block in masks["levels"]:
        nblk = C // (2 * m)
        pieces = [jnp.broadcast_to(b[i * 2 * m + m - 1:i * 2 * m + m, :], (2 * m, GLA_DK))
                  for i in range(nblk)]
        bref = pieces[0] if nblk == 1 else jnp.concatenate(pieces, axis=0)
        e = jnp.exp(-jnp.abs(b - bref))
        ql = jnp.where(second, q * e, 0.0).astype(BF16)
        kl = jnp.where(second, 0.0, k * e).astype(BF16)
        a = _dot_nt(ql, kl)
        if nblk > 1:
            a = jnp.where(same_block, a, 0.0)
        att = a if att is None else att + a

    sub_col = masks["sub_col"]
    diag = []
    for i in range(C // GLA_SUB):
        sl = slice(GLA_SUB * i, GLA_SUB * (i + 1))
        bb, qq, kk = b[sl], q[sl], k[sl]
        blk = jnp.zeros((GLA_SUB, C), F32)
        for s in range(GLA_SUB):
            d = jnp.where(masks["causal"][s], bb - bb[s:s + 1], MASK_NEG)
            w = jnp.sum(qq * jnp.exp(d) * kk[s:s + 1], axis=-1, keepdims=True)
            blk = jnp.where(sub_col == GLA_SUB * i + s, w, blk)
        diag.append(blk)
    diag = diag[0] if len(diag) == 1 else jnp.concatenate(diag, axis=0)
    att = diag if att is None else att + diag
    o = o + _dot(att.astype(BF16), v.astype(BF16))

    bl = b[C - 1:C, :]
    kd = k * jnp.exp(bl - b)
    vp = v
    if C < GLA_DK:
        kd = jnp.concatenate([kd, jnp.zeros((GLA_DK - C, GLA_DK), F32)], axis=0)
        vp = jnp.concatenate([v, jnp.zeros((GLA_DK - C, GLA_DV), F32)], axis=0)
    new_state = state * _row_to_col(jnp.exp(bl), GLA_DK) + _dot(kd.T.astype(BF16), vp.astype(BF16))
    return o, new_state


def _gla_kernel(q_ref, k_ref, v_ref, gad_ref, aup_ref, ab_ref, s0_ref, o_ref, sfin_ref, s_ref,
                *, C, sub, nc):
    n = pl.program_id(1)

    @pl.when(n == 0)
    def _():
        s_ref[...] = s0_ref[...]

    def chunk(c, carry):
        rows = pl.ds(pl.multiple_of(c * C, C), C)
        logits = _dot(gad_ref[rows, :].astype(BF16), aup_ref[...].astype(BF16)) + ab_ref[...]
        g = _log_sigmoid(logits) / GLA_TAU

        ri = lax.broadcasted_iota(jnp.int32, (C, C), 0)
        ci = lax.broadcasted_iota(jnp.int32, (C, C), 1)
        tri = jnp.where(ri >= ci, 1.0, 0.0).astype(BF16)
        g_hi = g.astype(BF16)
        r1 = g - g_hi.astype(F32)
        g_mid = r1.astype(BF16)
        g_lo = (r1 - g_mid.astype(F32)).astype(BF16)
        b_all = _dot(tri, g_hi) + _dot(tri, g_mid) + _dot(tri, g_lo)

        masks = _gla_masks(C)
        for h in range(GLA_HEADS):
            ks = slice(GLA_DK * h, GLA_DK * (h + 1))
            vs = slice(GLA_DV * h, GLA_DV * (h + 1))
            o, new_state = _gla_head(q_ref[rows, ks] * (GLA_DK ** -0.5), k_ref[rows, ks], v_ref[rows, vs],
                                     b_all[:, ks], s_ref[h], masks, C=C)
            o_ref[rows, vs] = o
            s_ref[h] = new_state
        return carry

    if sub == 1:
        chunk(0, 0)
    else:
        lax.fori_loop(0, sub, chunk, 0)

    @pl.when(n == nc - 1)
    def _():
        sfin_ref[0] = s_ref[...]


def _gla(u, gad, a_up, a_bias, s0, *, batch, seq, C, sub, row_off, name):
    rows = C * sub
    nc = seq // rows
    return pl.pallas_call(
        functools.partial(_gla_kernel, C=C, sub=sub, nc=nc),
        grid=(batch, nc),
        in_specs=[
            pl.BlockSpec((rows, GLA_QK), lambda b, n: (row_off + b * nc + n, COL_GQ // GLA_QK)),
            pl.BlockSpec((rows, GLA_QK), lambda b, n: (row_off + b * nc + n, COL_GK // GLA_QK)),
            pl.BlockSpec((rows, GLA_V), lambda b, n: (row_off + b * nc + n, COL_GV // GLA_V)),
            pl.BlockSpec((rows, GLA_RANK), lambda b, n: (row_off + b * nc + n, 0)),
            pl.BlockSpec((GLA_RANK, GLA_QK), lambda b, n: (0, 0)),
            pl.BlockSpec((1, GLA_QK), lambda b, n: (0, 0)),
            pl.BlockSpec((GLA_HEADS, GLA_DK, GLA_DV), lambda b, n: (0, 0, 0)),
        ],
        out_specs=[
            pl.BlockSpec((rows, GLA_V), lambda b, n: (b * nc + n, 0)),
            pl.BlockSpec((1, GLA_HEADS, GLA_DK, GLA_DV), lambda b, n: (b, 0, 0, 0)),
        ],
        out_shape=[
            jax.ShapeDtypeStruct((batch * seq, GLA_V), F32),
            jax.ShapeDtypeStruct((batch, GLA_HEADS, GLA_DK, GLA_DV), F32),
        ],
        scratch_shapes=[pltpu.VMEM((GLA_HEADS, GLA_DK, GLA_DV), F32)],
        compiler_params=pltpu.CompilerParams(
            dimension_semantics=("parallel", "arbitrary"), vmem_limit_bytes=ATTN_VMEM),
        name=name,
    )(u, u, u, gad, a_up, a_bias, s0)


def _gla_step_kernel(q_ref, k_ref, v_ref, gad_ref, aup_ref, ab_ref, s_ref, o_ref, sn_ref, *, bb):
    gl = jnp.concatenate([gad_ref[i] for i in range(bb)] + [jnp.zeros((SUBLANE - bb, GLA_RANK), F32)], axis=0)
    logits = _dot(gl.astype(BF16), aup_ref[...].astype(BF16)) + ab_ref[...]
    dec_all = jnp.exp(_log_sigmoid(logits) / GLA_TAU)
    for i in range(bb):
        q = q_ref[i] * (GLA_DK ** -0.5)
        k = k_ref[i]
        v = v_ref[i]
        dec = dec_all[i:i + 1]
        outs = []
        for h in range(GLA_HEADS):
            sl = slice(GLA_DK * h, GLA_DK * (h + 1))
            new_state = (s_ref[i, h] * _row_to_col(dec[:, sl], GLA_DK)
                         + _row_to_col(k[:, sl], GLA_DK) * v[:, GLA_DV * h:GLA_DV * (h + 1)])
            sn_ref[i, h] = new_state
            outs.append(jnp.sum(_row_to_col(q[:, sl], GLA_DK) * new_state, axis=0, keepdims=True))
        o_ref[i] = jnp.concatenate(outs, axis=1)


def _gla_step(u3, gad3, a_up, a_bias, state, *, batch, bb, name):
    return pl.pallas_call(
        functools.partial(_gla_step_kernel, bb=bb),
        grid=(batch // bb,),
        in_specs=[
            pl.BlockSpec((bb, 1, GLA_QK), lambda b: (b, 0, COL_GQ // GLA_QK)),
            pl.BlockSpec((bb, 1, GLA_QK), lambda b: (b, 0, COL_GK // GLA_QK)),
            pl.BlockSpec((bb, 1, GLA_V), lambda b: (b, 0, COL_GV // GLA_V)),
            pl.BlockSpec((bb, 1, GLA_RANK), lambda b: (b, 0, 0)),
            pl.BlockSpec((GLA_RANK, GLA_QK), lambda b: (0, 0)),
            pl.BlockSpec((1, GLA_QK), lambda b: (0, 0)),
            pl.BlockSpec((bb, GLA_HEADS, GLA_DK, GLA_DV), lambda b: (b, 0, 0, 0)),
        ],
        out_specs=[
            pl.BlockSpec((bb, 1, GLA_V), lambda b: (b, 0, 0)),
            pl.BlockSpec((bb, GLA_HEADS, GLA_DK, GLA_DV), lambda b: (b, 0, 0, 0)),
        ],
        out_shape=[
            jax.ShapeDtypeStruct((batch, 1, GLA_V), F32),
            jax.ShapeDtypeStruct((batch, GLA_HEADS, GLA_DK, GLA_DV), F32),
        ],
        compiler_params=pltpu.CompilerParams(dimension_semantics=("parallel",)),
        name=name,
    )(u3, u3, u3, gad3, a_up, a_bias, state)


def _alibi_slope(head):
    return 2.0 ** (-8.0 * (head + 1) / SWA_HEADS)


def _swa_kernel(sink_ref, q_ref, kc_ref, kp_ref, vc_ref, vp_ref, mk_ref, mv_ref, o_ref, bias_ref):
    n = pl.program_id(1)
    rows = SWA_GROUP * SWA_QB
    nsub = WINDOW // SWA_QB

    @pl.when(n <= 1)
    def _():
        r = lax.broadcasted_iota(jnp.int32, (SWA_QB, SWA_KT), 0)
        c = lax.broadcasted_iota(jnp.int32, (SWA_QB, SWA_KT), 1)
        dist = r + WINDOW - c
        distf = dist.astype(F32)
        band = (dist >= 0) & (dist < WINDOW) & (c < SWA_SPAN)
        for a in range(nsub):
            ok = band & ((c >= WINDOW - SWA_QB * a) | (n > 0))
            for head in range(SWA_HEADS):
                val = jnp.where(ok, -_alibi_slope(head) * distf, MASK_NEG)
                val = jnp.where((c >= SWA_SPAN) & (c < SWA_SINK_COL), 0.0, val)
                val = jnp.where(c == SWA_SINK_COL, sink_ref[head], val)
                tile = (head // SWA_GROUP) * nsub + a
                bias_ref[pl.ds(tile * rows + (head % SWA_GROUP) * SWA_QB, SWA_QB), :] = val

    q = q_ref[...] * (SWA_HD ** -0.5)
    pad = jnp.zeros((SWA_KT - SWA_SPAN - N_META, SWA_HD), BF16)
    scores, values = [], []
    for kvh in range(SWA_KV_HEADS):
        cs = slice(SWA_HD * kvh, SWA_HD * (kvh + 1))
        kwin = jnp.concatenate([kp_ref[:, cs], kc_ref[:, cs]], axis=0).astype(BF16)
        vwin = jnp.concatenate([vp_ref[:, cs], vc_ref[:, cs]], axis=0).astype(BF16)
        km = mk_ref[:, cs].astype(BF16)
        vm = mv_ref[:, cs].astype(BF16)
        for a in range(nsub):
            qs = slice(SWA_QB * a, SWA_QB * (a + 1))
            kt = jnp.concatenate([kwin[SWA_QB * a:SWA_QB * a + SWA_SPAN], km, pad], axis=0)
            values.append(jnp.concatenate([vwin[SWA_QB * a:SWA_QB * a + SWA_SPAN], vm, pad], axis=0))
            qg = jnp.concatenate(
                [q[qs, SWA_HD * (kvh * SWA_GROUP + grp):SWA_HD * (kvh * SWA_GROUP + grp + 1)]
                 for grp in range(SWA_GROUP)], axis=0).astype(BF16)
            scores.append(_dot_nt(qg, kt))
    s = jnp.concatenate(scores, axis=0) + bias_ref[...]
    p = jnp.exp(s - jnp.max(s, axis=-1, keepdims=True))
    inv = 1.0 / jnp.sum(p, axis=-1, keepdims=True)
    p = p.astype(BF16)
    for kvh in range(SWA_KV_HEADS):
        for a in range(nsub):
            tile = kvh * nsub + a
            ts = slice(rows * tile, rows * (tile + 1))
            o = _dot(p[ts], values[tile]) * inv[ts]
            for grp in range(SWA_GROUP):
                head = kvh * SWA_GROUP + grp
                o_ref[SWA_QB * a:SWA_QB * (a + 1), SWA_HD * head:SWA_HD * (head + 1)] = (
                    o[SWA_QB * grp:SWA_QB * (grp + 1)].astype(o_ref.dtype))


def _swa(sinks, u, u_small, *, batch, seq, meta_row_block, name):
    nb = seq // WINDOW
    kcol, vcol = COL_SK // SWA_KV, COL_SV // SWA_KV
    return pl.pallas_call(
        _swa_kernel,
        grid=(batch, nb),
        in_specs=[
            pl.BlockSpec(memory_space=pltpu.SMEM),
            pl.BlockSpec((WINDOW, SWA_Q), lambda b, n: (b * nb + n, COL_SQ // SWA_Q)),
            pl.BlockSpec((WINDOW, SWA_KV), lambda b, n: (b * nb + n, kcol)),
            pl.BlockSpec((WINDOW, SWA_KV), lambda b, n: (b * nb + jnp.maximum(n - 1, 0), kcol)),
            pl.BlockSpec((WINDOW, SWA_KV), lambda b, n: (b * nb + n, vcol)),
            pl.BlockSpec((WINDOW, SWA_KV), lambda b, n: (b * nb + jnp.maximum(n - 1, 0), vcol)),
            pl.BlockSpec((N_META, SWA_KV), lambda b, n: (meta_row_block, kcol)),
            pl.BlockSpec((N_META, SWA_KV), lambda b, n: (meta_row_block, vcol)),
        ],
        out_specs=pl.BlockSpec((WINDOW, SWA_Q), lambda b, n: (b * nb + n, 0)),
        out_shape=jax.ShapeDtypeStruct((batch * seq, SWA_Q), BF16),
        scratch_shapes=[pltpu.VMEM((SWA_HEADS * WINDOW, SWA_KT), F32)],
        compiler_params=pltpu.CompilerParams(
            dimension_semantics=("arbitrary", "arbitrary"), vmem_limit_bytes=ATTN_VMEM),
        name=name,
    )(sinks, u, u, u, u, u, u_small, u_small)


def _swa_step_kernel(q_ref, kold_ref, vold_ref, knew_ref, vnew_ref, mk_ref, mv_ref, sink_ref, slope_ref,
                     o_ref, nk_ref, nv_ref, *, bb):
    rows = bb * SWA_GROUP
    wcols = bb * WINDOW
    ncol = wcols + bb * N_META
    win_shift, meta_shift, grp_shift = WINDOW.bit_length() - 1, N_META.bit_length() - 1, SWA_GROUP.bit_length() - 1
    pos = lax.broadcasted_iota(jnp.int32, (SWA_KV, WINDOW), 1)
    knew_t = jnp.concatenate([knew_ref[i] for i in range(bb)], axis=0).T
    vnew_t = jnp.concatenate([vnew_ref[i] for i in range(bb)], axis=0).T
    for i in range(bb):
        nk_ref[i] = jnp.where(pos == WINDOW - 1, knew_t[:, i:i + 1], pltpu.roll(kold_ref[i], WINDOW - 1, axis=1))
        nv_ref[i] = jnp.where(pos == WINDOW - 1, vnew_t[:, i:i + 1], pltpu.roll(vold_ref[i], WINDOW - 1, axis=1))

    r = lax.broadcasted_iota(jnp.int32, (rows, ncol), 0)
    c = lax.broadcasted_iota(jnp.int32, (rows, ncol), 1)
    in_win = c < wcols
    col_batch = jnp.where(in_win, c >> win_shift, (c - wcols) >> meta_shift)
    own = col_batch == (r >> grp_shift)
    distf = jnp.where(in_win, WINDOW - 1 - (c & (WINDOW - 1)), 0).astype(F32)
    for kvh in range(SWA_KV_HEADS):
        cs = slice(SWA_HD * kvh, SWA_HD * (kvh + 1))
        hs = slice(SWA_GROUP * kvh, SWA_GROUP * (kvh + 1))
        k_t = jnp.concatenate([nk_ref[i, cs, :] for i in range(bb)], axis=1).astype(BF16)
        v_t = jnp.concatenate([nv_ref[i, cs, :] for i in range(bb)], axis=1).astype(BF16)
        km = jnp.concatenate([mk_ref[i, :, cs] for i in range(bb)], axis=0).astype(BF16)
        vm = jnp.concatenate([mv_ref[i, :, cs] for i in range(bb)], axis=0).astype(BF16)
        qh = jnp.concatenate([q_ref[i, hs, :] for i in range(bb)], axis=0)
        qh = (qh * (SWA_HD ** -0.5)).astype(BF16)
        s = jnp.concatenate([_dot(qh, k_t), _dot_nt(qh, km)], axis=1)
        s = jnp.where(own, s - slope_ref[kvh] * distf, MASK_NEG)
        sink = sink_ref[kvh]
        mx = jnp.maximum(jnp.max(s, axis=-1, keepdims=True), sink)
        p = jnp.exp(s - mx)
        den = jnp.sum(p, axis=-1, keepdims=True) + jnp.exp(sink - mx)
        p = p.astype(BF16)
        oh = (_dot_nt(p[:, :wcols], v_t) + _dot(p[:, wcols:], vm)) / den
        for i in range(bb):
            o_ref[i, hs, :] = oh[SWA_GROUP * i:SWA_GROUP * (i + 1)]


def _swa_step(q3, k_old, v_old, u3, mk, mv, sinks, *, batch, bb, name):
    def per_row(x16):
        return jnp.tile(x16.reshape(SWA_KV_HEADS, 1, SWA_GROUP), (1, bb, 1)).reshape(
            SWA_KV_HEADS, bb * SWA_GROUP, 1)

    slopes = jnp.asarray([_alibi_slope(h) for h in range(SWA_HEADS)], F32)
    head_spec = pl.BlockSpec((SWA_KV_HEADS, bb * SWA_GROUP, 1), lambda b: (0, 0, 0))
    win_spec = pl.BlockSpec((bb, SWA_KV, WINDOW), lambda b: (b, 0, 0))
    meta_spec = pl.BlockSpec((bb, N_META, SWA_KV), lambda b: (b, 0, 0))
    return pl.pallas_call(
        functools.partial(_swa_step_kernel, bb=bb),
        grid=(batch // bb,),
        in_specs=[
            pl.BlockSpec((bb, SWA_HEADS, SWA_HD), lambda b: (b, 0, 0)),
            win_spec, win_spec,
            pl.BlockSpec((bb, 1, SWA_KV), lambda b: (b, 0, COL_SK // SWA_KV)),
            pl.BlockSpec((bb, 1, SWA_KV), lambda b: (b, 0, COL_SV // SWA_KV)),
            meta_spec, meta_spec, head_spec, head_spec,
        ],
        out_specs=[pl.BlockSpec((bb, SWA_HEADS, SWA_HD), lambda b: (b, 0, 0)), win_spec, win_spec],
        out_shape=[
            jax.ShapeDtypeStruct((batch, SWA_HEADS, SWA_HD), F32),
            jax.ShapeDtypeStruct((batch, SWA_KV, WINDOW), F32),
            jax.ShapeDtypeStruct((batch, SWA_KV, WINDOW), F32),
        ],
        compiler_params=pltpu.CompilerParams(dimension_semantics=("parallel",)),
        name=name,
    )(q3, k_old, v_old, u3, u3, mk, mv, per_row(sinks), per_row(slopes))


def _mix_kernel(og_ref, gr_ref, os_ref, ga_ref, gb_ref, ogs_ref, grs_ref, oss_ref, gas_ref, gbs_ref,
                onorm_ref, gbias_ref, wg_ref, ws_ref, mix_ref, mixs_ref):
    _mix_rows(og_ref, gr_ref, os_ref, ga_ref, gb_ref, onorm_ref, gbias_ref, wg_ref, ws_ref, mix_ref)

    @pl.when(pl.program_id(0) == 0)
    def _():
        _mix_rows(ogs_ref, grs_ref, oss_ref, gas_ref, gbs_ref, onorm_ref, gbias_ref, wg_ref, ws_ref,
                  mixs_ref)


def _proj_kernel(h_ref, mix_ref, hs_ref, mixs_ref, wo_ref, out_ref, outs_ref):
    out_ref[...] = h_ref[...] + _dot(mix_ref[...], wo_ref[...])

    @pl.when(pl.program_id(0) == 0)
    def _():
        outs_ref[...] = hs_ref[...] + _dot(mixs_ref[...], wo_ref[...])


def _mix_rows(og_ref, gr_ref, os_ref, ga_ref, gb_ref, onorm_ref, gbias_ref, wg_ref, ws_ref, mix_ref):
    og = og_ref[...]
    parts = []
    for h in range(GLA_HEADS):
        oh = og[:, GLA_DV * h:GLA_DV * (h + 1)]
        parts.append(oh * lax.rsqrt(jnp.mean(oh * oh, axis=-1, keepdims=True) + EPS))
    gr = gr_ref[...]
    of = jnp.concatenate(parts, axis=1) * onorm_ref[...] * (gr * jax.nn.sigmoid(gr))
    gla_b = _dot(of.astype(BF16), wg_ref[...])
    swa_b = _dot(os_ref[...].astype(BF16), ws_ref[...])
    gbias = gbias_ref[...]
    mix = (jax.nn.sigmoid(ga_ref[...] + gbias[:, :D_MODEL]) * gla_b
           + jax.nn.sigmoid(gb_ref[...] + gbias[:, D_MODEL:]) * swa_b)
    mix_ref[...] = mix.astype(BF16)


def _merge(h, o_gla, u, o_swa, hs, o_gla_s, us, o_swa_s, onorm, gbias, w_gla_o, w_swa_o, w_out,
           *, tm, name):
    rows, rs = h.shape[0], o_gla_s.shape[0]
    const = dict(pipeline_mode=pl.Buffered(1))

    def row_specs(r, idx):
        return [
            pl.BlockSpec((r, GLA_V), lambda i: (idx(i), 0)),
            pl.BlockSpec((r, GLA_V), lambda i: (idx(i), COL_GR // GLA_V)),
            pl.BlockSpec((r, SWA_Q), lambda i: (idx(i), 0)),
            pl.BlockSpec((r, D_MODEL), lambda i: (idx(i), COL_GA // D_MODEL)),
            pl.BlockSpec((r, D_MODEL), lambda i: (idx(i), COL_GB // D_MODEL)),
        ]

    mix, mix_s = pl.pallas_call(
        _mix_kernel,
        grid=(rows // tm,),
        in_specs=row_specs(tm, lambda i: i) + row_specs(rs, lambda i: 0) + [
            pl.BlockSpec((1, GLA_V), lambda i: (0, 0)),
            pl.BlockSpec((1, 2 * D_MODEL), lambda i: (0, 0)),
            pl.BlockSpec((GLA_V, D_MODEL), lambda i: (0, 0), **const),
            pl.BlockSpec((SWA_Q, D_MODEL), lambda i: (0, 0), **const),
        ],
        out_specs=[
            pl.BlockSpec((tm, D_MODEL), lambda i: (i, 0)),
            pl.BlockSpec((rs, D_MODEL), lambda i: (0, 0)),
        ],
        out_shape=[
            jax.ShapeDtypeStruct((rows, D_MODEL), BF16),
            jax.ShapeDtypeStruct((rs, D_MODEL), BF16),
        ],
        compiler_params=pltpu.CompilerParams(
            dimension_semantics=("arbitrary",), vmem_limit_bytes=MERGE_VMEM),
        name=name + "_mix",
    )(o_gla, u, o_swa, u, u, o_gla_s, us, o_swa_s, us, us, onorm, gbias, w_gla_o, w_swa_o)

    tp = PROJ_TM
    return pl.pallas_call(
        _proj_kernel,
        grid=(rows // tp,),
        in_specs=[
            pl.BlockSpec((tp, D_MODEL), lambda i: (i, 0)),
            pl.BlockSpec((tp, D_MODEL), lambda i: (i, 0)),
            pl.BlockSpec((rs, D_MODEL), lambda i: (0, 0)),
            pl.BlockSpec((rs, D_MODEL), lambda i: (0, 0)),
            pl.BlockSpec((D_MODEL, D_MODEL), lambda i: (0, 0), **const),
        ],
        out_specs=[
            pl.BlockSpec((tp, D_MODEL), lambda i: (i, 0)),
            pl.BlockSpec((rs, D_MODEL), lambda i: (0, 0)),
        ],
        out_shape=[
            jax.ShapeDtypeStruct((rows, D_MODEL), F32),
            jax.ShapeDtypeStruct((rs, D_MODEL), F32),
        ],
        compiler_params=pltpu.CompilerParams(
            dimension_semantics=("arbitrary",), vmem_limit_bytes=MERGE_VMEM),
        name=name + "_proj",
    )(h, mix, hs, mix_s, w_out)


def kernel(x_prompt, x_sample, cache_meta_k, cache_meta_v, cache_win_k, cache_win_v, state_gla,
           meta_tokens, ffn1_norm, ffn1_w_in, ffn1_w_out, mix_norm, w_in, gla_a_up, gla_a_bias,
           gla_out_norm, w_gla_o, swa_sinks, w_swa_o, gate_bias, w_out, ffn2_norm, ffn2_w_in,
           ffn2_w_out, final_norm):
    B, S, _ = x_prompt.shape
    DB = x_sample.shape[0]
    assert x_sample.shape[1] == 1 and ffn1_norm.shape[0] == 1
    n_small = DB + N_META

    wgo, wso, wout = w_gla_o[0].astype(BF16), w_swa_o[0].astype(BF16), w_out[0].astype(BF16)
    g1, gm, g2 = ffn1_norm, mix_norm, ffn2_norm
    gf = final_norm.reshape(1, D_MODEL)
    a_up, a_bias = gla_a_up[0], gla_a_bias
    onorm, gbias = gla_out_norm, gate_bias
    sinks = swa_sinks[0]

    xs = jnp.concatenate([x_sample.reshape(DB, D_MODEL), meta_tokens.astype(F32)], axis=0)
    xp = x_prompt.reshape(B * S, D_MODEL)
    ffn_tiles = dict(tm=FFN_TM, tf_first=FFN_TF_FIRST, tf=FFN_TF)
    hp1, hs1 = _ffn(xp, xs, g1, ffn1_w_in[0], ffn1_w_out[0], gf, final_norm=False, name="ffn1", **ffn_tiles)
    up, gadp, us, gads = _inproj(hp1, hs1, gm, w_in[0].T, tm=INPROJ_TM, tn_first=INPROJ_TN_FIRST,
                                 tn=INPROJ_TN, name="inproj")

    meta_blk = DB // N_META
    _, st_meta = _gla(us, gads, a_up, a_bias, jnp.zeros((GLA_HEADS, GLA_DK, GLA_DV), F32),
                      batch=1, seq=N_META, C=N_META, sub=1, row_off=meta_blk, name="gla_meta")
    st_meta = st_meta[0]
    o_gla, st_fin = _gla(up, gadp, a_up, a_bias, st_meta, batch=B, seq=S, C=GLA_CHUNK,
                         sub=GLA_CHUNKS_PER_STEP, row_off=0, name="gla")
    o_swa = _swa(sinks, up, us, batch=B, seq=S, meta_row_block=meta_blk, name="swa")

    us3 = us.reshape(n_small, 1, N_MAIN)
    og_s, st_s = _gla_step(us3, gads.reshape(n_small, 1, GLA_RANK),
                           a_up, a_bias, state_gla[0], batch=DB, bb=GLA_STEP_BATCH, name="gla_step")
    WB = cache_win_k.shape[2]
    assert WB == WINDOW
    os_s, new_kt, new_vt = _swa_step(
        us[:DB, COL_SQ:COL_SQ + SWA_Q].reshape(DB, SWA_HEADS, SWA_HD),
        jnp.swapaxes(cache_win_k[0].reshape(DB, WB, SWA_KV), 1, 2),
        jnp.swapaxes(cache_win_v[0].reshape(DB, WB, SWA_KV), 1, 2), us3,
        cache_meta_k[0].reshape(DB, N_META, SWA_KV), cache_meta_v[0].reshape(DB, N_META, SWA_KV),
        sinks, batch=DB, bb=SWA_STEP_BATCH, name="swa_step")
    hp2, hs2 = _merge(hp1, o_gla, up, o_swa, hs1, og_s.reshape(DB, GLA_V), us, os_s.reshape(DB, SWA_Q),
                      onorm, gbias, wgo, wso, wout, tm=MERGE_TM, name="merge")
    y_prompt, y_sample = _ffn(hp2, hs2, g2, ffn2_w_in[0], ffn2_w_out[0], gf, final_norm=True, name="ffn2",
                              **ffn_tiles)

    up3 = up.reshape(B, S, N_MAIN)
    kv_shape = (SWA_KV_HEADS, SWA_HD)
    p_meta_k = jnp.broadcast_to(us[DB:, COL_SK:COL_SK + SWA_KV].reshape(1, 1, N_META, *kv_shape),
                                (1, B, N_META, *kv_shape))
    p_meta_v = jnp.broadcast_to(us[DB:, COL_SV:COL_SV + SWA_KV].reshape(1, 1, N_META, *kv_shape),
                                (1, B, N_META, *kv_shape))
    p_win_k = up3[:, S - WINDOW:, COL_SK:COL_SK + SWA_KV].reshape(1, B, WINDOW, *kv_shape)
    p_win_v = up3[:, S - WINDOW:, COL_SV:COL_SV + SWA_KV].reshape(1, B, WINDOW, *kv_shape)
    return (y_prompt.reshape(B, S, D_MODEL), y_sample.reshape(DB, 1, D_MODEL),
            p_meta_k, p_meta_v, p_win_k, p_win_v, st_fin[None],
            jnp.swapaxes(new_kt, 1, 2).reshape(1, DB, WB, *kv_shape),
            jnp.swapaxes(new_vt, 1, 2).reshape(1, DB, WB, *kv_shape), st_s[None])
```
